```python
import math
import jax, jax.numpy as jnp
from jax import lax
import numpy as np

D_MODEL = 1024
BATCH = 16
SEQ = 2048
DEPTH = 2

MEM_LEN = 256
GRID_W = 64
Q_BLOCK = 128
ROPE_THETA = 10000.0
NORM_EPS = 1e-6

DIFF_HEADS = 4
DIFF_HEAD_DIM = 64
MLA_HEADS = 8
MLA_Q_RANK = 256
MLA_KV_RANK = 128
MLA_NOPE_DIM = 64
MLA_ROPE_DIM = 32
MLA_V_DIM = 64
GQA_HEADS = 8
GQA_KV_HEADS = 2
GQA_HEAD_DIM = 64
MEM_HEADS = 4
MEM_HEAD_DIM = 128

N_BRANCHES = 4
BRANCH_WIDTH = 512

DIFF_Q_COLS = 2 * DIFF_HEADS * DIFF_HEAD_DIM
DIFF_K_COLS = 2 * DIFF_HEADS * DIFF_HEAD_DIM
DIFF_V_COLS = DIFF_HEADS * 2 * DIFF_HEAD_DIM
GQA_Q_COLS = GQA_HEADS * GQA_HEAD_DIM
GQA_KV_COLS = GQA_KV_HEADS * GQA_HEAD_DIM
MEM_Q_COLS = MEM_HEADS * MEM_HEAD_DIM
IN_COL_SIZES = (DIFF_Q_COLS, DIFF_K_COLS, DIFF_V_COLS,
                MLA_Q_RANK, MLA_KV_RANK, MLA_ROPE_DIM,
                GQA_Q_COLS, GQA_KV_COLS, GQA_KV_COLS,
                MEM_Q_COLS)
IN_COLS = 3232

FFN_DIM = 3584
N_EXPERTS = 8
TOP_K = 2
MOE_ROWS = 512
N_DENSE = (DEPTH + 1) // 2
N_MOE = DEPTH // 2

kernel_name = 'hybrid_gated_encoder'


def rms_norm(x, g):
    xf = x.astype(jnp.float32)
    y = xf * lax.rsqrt(jnp.mean(xf * xf, axis=-1, keepdims=True) + NORM_EPS)
    return (y * g.astype(jnp.float32)).astype(x.dtype)


def rope_angles(pos, dim):
    inv_freq = 1.0 / (ROPE_THETA ** (jnp.arange(0, dim, 2, dtype=jnp.float32) / dim))
    return pos.astype(jnp.float32)[:, None] * inv_freq[None, :]


def apply_rope(x, ang):
    half = x.shape[-1] // 2
    shape = (ang.shape[0],) + (1,) * (x.ndim - 3) + (half,)
    cos = jnp.cos(ang).reshape(shape)
    sin = jnp.sin(ang).reshape(shape)
    xf = x.astype(jnp.float32)
    x1, x2 = xf[..., :half], xf[..., half:]
    return jnp.concatenate([x1 * cos - x2 * sin, x2 * cos + x1 * sin], axis=-1).astype(x.dtype)


def apply_axial_rope(x, ang_row, ang_col):
    half = x.shape[-1] // 2
    return jnp.concatenate([apply_rope(x[..., :half], ang_row),
                            apply_rope(x[..., half:], ang_col)], axis=-1)


def sweep_query_blocks(q, block_fn):
    B, S = q.shape[:2]
    nb = S // Q_BLOCK
    qb = jnp.moveaxis(q.reshape((B, nb, Q_BLOCK) + q.shape[2:]), 1, 0)
    out = lax.map(block_fn, qb)
    out = jnp.moveaxis(out, 0, 1)
    return out.reshape((B, S) + out.shape[3:])


def mha_blocks(q, k, v):
    scale = q.shape[-1] ** -0.5

    def block(qb):
        s = jnp.einsum('bqhd,bkhd->bhqk', qb, k, preferred_element_type=jnp.float32) * scale
        p = jax.nn.softmax(s, axis=-1)
        return jnp.einsum('bhqk,bkhe->bqhe', p.astype(v.dtype), v)

    return sweep_query_blocks(q, block)


def diff_attention(q, k, v, q_g, k_g, lam_params, subln_g, lambda_init, ang):
    B, S = q.shape[:2]
    q = apply_rope(rms_norm(q.reshape(B, S, 2 * DIFF_HEADS, DIFF_HEAD_DIM), q_g), ang)
    k = apply_rope(rms_norm(k.reshape(B, S, 2 * DIFF_HEADS, DIFF_HEAD_DIM), k_g), ang)
    v = v.reshape(B, S, DIFF_HEADS, 2 * DIFF_HEAD_DIM)
    lp = lam_params.astype(jnp.float32)
    lam = jnp.exp(jnp.sum(lp[0] * lp[1])) - jnp.exp(jnp.sum(lp[2] * lp[3])) + lambda_init
    scale = DIFF_HEAD_DIM ** -0.5

    def block(qb):
        s = jnp.einsum('bqhd,bkhd->bhqk', qb, k, preferred_element_type=jnp.float32) * scale
        p = jax.nn.softmax(s, axis=-1).reshape(B, DIFF_HEADS, 2, qb.shape[1], S)
        a = p[:, :, 0] - lam * p[:, :, 1]
        return jnp.einsum('bhqk,bkhe->bqhe', a.astype(v.dtype), v)

    o = sweep_query_blocks(q, block)
    o = rms_norm(o, subln_g) * (1.0 - lambda_init)
    return o.reshape(B, S, BRANCH_WIDTH)


def latent_attention(q_lat, kv_lat, k_rope, q_lat_g, kv_lat_g, w_q_up, w_kv_up, q_g, k_g, ang):
    B, S = q_lat.shape[:2]
    q = (rms_norm(q_lat, q_lat_g) @ w_q_up).reshape(B, S, MLA_HEADS, MLA_NOPE_DIM + MLA_ROPE_DIM)
    kv = (rms_norm(kv_lat, kv_lat_g) @ w_kv_up).reshape(B, S, MLA_HEADS, MLA_NOPE_DIM + MLA_V_DIM)
    k_nope, v = kv[..., :MLA_NOPE_DIM], kv[..., MLA_NOPE_DIM:]
    k_r = jnp.broadcast_to(k_rope[:, :, None, :], (B, S, MLA_HEADS, MLA_ROPE_DIM))
    k = jnp.concatenate([k_nope, k_r], axis=-1)
    q = rms_norm(q, q_g)
    k = rms_norm(k, k_g)
    q = jnp.concatenate([q[..., :MLA_NOPE_DIM], apply_rope(q[..., MLA_NOPE_DIM:], ang)], axis=-1)
    k = jnp.concatenate([k[..., :MLA_NOPE_DIM], apply_rope(k[..., MLA_NOPE_DIM:], ang)], axis=-1)
    o = mha_blocks(q, k, v)
    return o.reshape(B, S, BRANCH_WIDTH)


def axial_gqa(q, k, v, q_g, k_g, ang_row, ang_col):
    B, S = q.shape[:2]
    G = GQA_HEADS // GQA_KV_HEADS
    q = apply_axial_rope(rms_norm(q.reshape(B, S, GQA_KV_HEADS, G, GQA_HEAD_DIM), q_g), ang_row, ang_col)
    k = apply_axial_rope(rms_norm(k.reshape(B, S, GQA_KV_HEADS, GQA_HEAD_DIM), k_g), ang_row, ang_col)
    v = v.reshape(B, S, GQA_KV_HEADS, GQA_HEAD_DIM)
    scale = GQA_HEAD_DIM ** -0.5

    def block(qb):
        s = jnp.einsum('bqngd,bsnd->bngqs', qb, k, preferred_element_type=jnp.float32) * scale
        p = jax.nn.softmax(s, axis=-1)
        return jnp.einsum('bngqs,bsnd->bqngd', p.astype(v.dtype), v)

    o = sweep_query_blocks(q, block)
    return o.reshape(B, S, BRANCH_WIDTH)


def memory_cross_attention(q, mem_n, w_mem_kv, q_g, k_g, ang_q, ang_m):
    B, S = q.shape[:2]
    M = mem_n.shape[1]
    q = apply_rope(rms_norm(q.reshape(B, S, MEM_HEADS, MEM_HEAD_DIM), q_g), ang_q)
    kv = (mem_n @ w_mem_kv).reshape(B, M, MEM_HEADS, 2 * MEM_HEAD_DIM)
    k = apply_rope(rms_norm(kv[..., :MEM_HEAD_DIM], k_g), ang_m)
    v = kv[..., MEM_HEAD_DIM:]
    s = jnp.einsum('bqhd,bkhd->bhqk', q, k, preferred_element_type=jnp.float32) * (MEM_HEAD_DIM ** -0.5)
    p = jax.nn.softmax(s, axis=-1)
    o = jnp.einsum('bhqk,bkhe->bqhe', p.astype(v.dtype), v)
    return o.reshape(B, S, BRANCH_WIDTH)


def swiglu(h, w1, w3, w2):
    return (jax.nn.silu(h @ w1) * (h @ w3)) @ w2


def moe_swiglu(h, w_router, w1, w3, w2):
    B, S, D = h.shape
    N = B * S
    hf = h.reshape(N, D)
    logits = jnp.matmul(hf, w_router, preferred_element_type=jnp.float32)
    top_logit, top_idx = lax.top_k(logits, TOP_K)
    gates = jax.nn.softmax(top_logit, axis=-1)
    n_assign = N * TOP_K
    flat_e = top_idx.reshape(-1).astype(jnp.int32)
    flat_tok = jnp.arange(n_assign, dtype=jnp.int32) // TOP_K
    flat_gate = gates.reshape(-1)
    order = jnp.argsort(flat_e)
    se, stok, sgate = flat_e[order], flat_tok[order], flat_gate[order]
    counts = jnp.bincount(flat_e, length=N_EXPERTS).astype(jnp.int32)
    padded = (counts + MOE_ROWS - 1) // MOE_ROWS * MOE_ROWS
    pad_end = jnp.cumsum(padded)
    pad_start = pad_end - padded
    grp_start = jnp.cumsum(counts) - counts
    slot = pad_start[se] + jnp.arange(n_assign, dtype=jnp.int32) - grp_start[se]
    n_blocks = (n_assign + N_EXPERTS * (MOE_ROWS - 1)) // MOE_ROWS
    n_slots = n_blocks * MOE_ROWS
    slot_tok = jnp.full((n_slots,), N, jnp.int32).at[slot].set(stok)
    slot_gate = jnp.zeros((n_slots,), jnp.float32).at[slot].set(sgate)
    block_start = jnp.arange(n_blocks, dtype=jnp.int32) * MOE_ROWS
    block_expert = jnp.minimum(jnp.searchsorted(pad_end, block_start, side='right'), N_EXPERTS - 1)
    h_pad = jnp.concatenate([hf, jnp.zeros((1, D), hf.dtype)], axis=0)
    xs = h_pad[slot_tok].reshape(n_blocks, MOE_ROWS, D)

    def expert_block(args):
        xb, e = args
        return swiglu(xb, w1[e], w3[e], w2[e])

    ys = lax.map(expert_block, (xs, block_expert)).reshape(n_slots, D)
    ys = ys * slot_gate[:, None].astype(ys.dtype)
    out = jax.ops.segment_sum(ys, slot_tok, num_segments=N + 1)[:N]
    return out.reshape(B, S, D)


def setup_inputs(seed: int = 0) -> dict:
    key = jax.random.key(seed)
    ks = iter(jax.random.split(key, 48))
    f32 = jnp.float32

    def nrm(shape, fan_in):
        return jax.random.normal(next(ks), shape, f32) * (fan_in ** -0.5)

    def gain(shape):
        return 1.0 + 0.02 * jax.random.normal(next(ks), shape, f32)

    D = D_MODEL
    return {
        'x': jax.random.normal(next(ks), (BATCH, SEQ, D), f32),
        'mem': jax.random.normal(next(ks), (BATCH, MEM_LEN, D), f32),
        'norm_attn_g': gain((DEPTH, D)),
        'norm_mem_g': gain((DEPTH, D)),
        'w_in': nrm((DEPTH, D, IN_COLS), D),
        'diff_q_norm_g': gain((DEPTH, DIFF_HEAD_DIM)),
        'diff_k_norm_g': gain((DEPTH, DIFF_HEAD_DIM)),
        'diff_lambda': 0.1 * jax.random.normal(next(ks), (DEPTH, 4, DIFF_HEAD_DIM), f32),
        'diff_subln_g': gain((DEPTH, 2 * DIFF_HEAD_DIM)),
        'mla_q_lat_g': gain((DEPTH, MLA_Q_RANK)),
        'mla_kv_lat_g': gain((DEPTH, MLA_KV_RANK)),
        'w_mla_q_up': nrm((DEPTH, MLA_Q_RANK, MLA_HEADS * (MLA_NOPE_DIM + MLA_ROPE_DIM)), MLA_Q_RANK),
        'w_mla_kv_up': nrm((DEPTH, MLA_KV_RANK, MLA_HEADS * (MLA_NOPE_DIM + MLA_V_DIM)), MLA_KV_RANK),
        'mla_q_norm_g': gain((DEPTH, MLA_NOPE_DIM + MLA_ROPE_DIM)),
        'mla_k_norm_g': gain((DEPTH, MLA_NOPE_DIM + MLA_ROPE_DIM)),
        'gqa_q_norm_g': gain((DEPTH, GQA_HEAD_DIM)),
        'gqa_k_norm_g': gain((DEPTH, GQA_HEAD_DIM)),
        'w_mem_kv': nrm((DEPTH, D, 2 * MEM_HEADS * MEM_HEAD_DIM), D),
        'mem_q_norm_g': gain((DEPTH, MEM_HEAD_DIM)),
        'mem_k_norm_g': gain((DEPTH, MEM_HEAD_DIM)),
        'w_branch': nrm((DEPTH, N_BRANCHES, BRANCH_WIDTH, D), BRANCH_WIDTH),
        'w_gate': nrm((DEPTH, N_BRANCHES, D, D), D),
        'b_gate': 0.02 * jax.random.normal(next(ks), (DEPTH, N_BRANCHES, D), f32),
        'w_out': nrm((DEPTH, D, D), D),
        'norm_ffn_g': gain((DEPTH, D)),
        'dense_w1': nrm((N_DENSE, D, FFN_DIM), D),
        'dense_w3': nrm((N_DENSE, D, FFN_DIM), D),
        'dense_w2': nrm((N_DENSE, FFN_DIM, D), FFN_DIM),
        'moe_router': nrm((N_MOE, D, N_EXPERTS), D),
        'moe_w1': nrm((N_MOE, N_EXPERTS, D, FFN_DIM), D),
        'moe_w3': nrm((N_MOE, N_EXPERTS, D, FFN_DIM), D),
        'moe_w2': nrm((N_MOE, N_EXPERTS, FFN_DIM, D), FFN_DIM),
    }


def reference(x, mem, norm_attn_g, norm_mem_g, w_in, diff_q_norm_g, diff_k_norm_g, diff_lambda,
              diff_subln_g, mla_q_lat_g, mla_kv_lat_g, w_mla_q_up, w_mla_kv_up, mla_q_norm_g,
              mla_k_norm_g, gqa_q_norm_g, gqa_k_norm_g, w_mem_kv, mem_q_norm_g, mem_k_norm_g,
              w_branch, w_gate, b_gate, w_out, norm_ffn_g, dense_w1, dense_w3, dense_w2,
              moe_router, moe_w1, moe_w3, moe_w2):
    B, S, _ = x.shape
    ROWS = S // GRID_W
    pos = jnp.arange(S, dtype=jnp.int32)
    rows = jnp.repeat(jnp.arange(ROWS, dtype=jnp.int32), GRID_W)
    cols = jnp.tile(jnp.arange(GRID_W, dtype=jnp.int32), ROWS)
    ang_diff = rope_angles(pos, DIFF_HEAD_DIM)
    ang_mla = rope_angles(pos, MLA_ROPE_DIM)
    ang_row = rope_angles(rows, GQA_HEAD_DIM // 2)
    ang_col = rope_angles(cols, GQA_HEAD_DIM // 2)
    ang_mem_q = rope_angles(pos, MEM_HEAD_DIM)
    ang_mem_k = rope_angles(jnp.arange(mem.shape[1], dtype=jnp.int32), MEM_HEAD_DIM)
    split_at = [int(c) for c in np.cumsum(IN_COL_SIZES)[:-1]]

    for layer in range(DEPTH):
        lambda_init = 0.8 - 0.6 * math.exp(-0.3 * layer)
        h = rms_norm(x, norm_attn_g[layer])
        proj = h @ w_in[layer]
        dq, dk, dv, mq_lat, mkv_lat, mk_rope, gq, gk, gv, memq = jnp.split(proj, split_at, axis=-1)

        o_a = diff_attention(dq, dk, dv, diff_q_norm_g[layer], diff_k_norm_g[layer],
                             diff_lambda[layer], diff_subln_g[layer], lambda_init, ang_diff)
        o_b = latent_attention(mq_lat, mkv_lat, mk_rope, mla_q_lat_g[layer], mla_kv_lat_g[layer],
                               w_mla_q_up[layer], w_mla_kv_up[layer], mla_q_norm_g[layer],
                               mla_k_norm_g[layer], ang_mla)
        o_c = axial_gqa(gq, gk, gv, gqa_q_norm_g[layer], gqa_k_norm_g[layer], ang_row, ang_col)
        mem_n = rms_norm(mem, norm_mem_g[layer])
        o_d = memory_cross_attention(memq, mem_n, w_mem_kv[layer], mem_q_norm_g[layer],
                                     mem_k_norm_g[layer], ang_mem_q, ang_mem_k)

        merged = None
        for b, o in enumerate((o_a, o_b, o_c, o_d)):
            gate = jax.nn.sigmoid(h @ w_gate[layer, b] + b_gate[layer, b])
            term = gate * (o @ w_branch[layer, b])
            merged = term if merged is None else merged + term
        x = x + merged @ w_out[layer]

        h2 = rms_norm(x, norm_ffn_g[layer])
        i = layer // 2
        if layer % 2 == 0:
            ffn = swiglu(h2, dense_w1[i], dense_w3[i], dense_w2[i])
        else:
            ffn = moe_swiglu(h2, moe_router[i], moe_w1[i], moe_w3[i], moe_w2[i])
        x = x + ffn
    return x
```

```python
import functools
import math

import numpy as np
import jax
import jax.numpy as jnp
from jax import lax
from jax.experimental import pallas as pl
from jax.experimental.pallas import tpu as pltpu

F32 = jnp.float32
BF16 = jnp.bfloat16

D_MODEL = 1024
MEM_GRID_W = 64
ROPE_THETA = 10000.0
NORM_EPS = 1e-6
DIFF_HEADS = 4
MLA_HEADS = 8
MLA_QK_DIM = 96
FFN_DIM = 3584
N_EXPERTS = 8
TOP_K = 2
MOE_ROWS = 512
LANES = 128
MXU_DIM = 256
VMEM_LIMIT = 56 * 1024 * 1024

C_DQ, C_DK, C_DV = 0, 512, 1024
C_MQL, C_MKVL, C_KROPE = 1536, 1792, 1920
C_GQ, C_GK, C_GV, C_MEMQ = 2048, 2560, 2816, 3072
IN_COLS_PADDED = 3584
T_A, T_B, T_C, T_D = 0, 256, 512, 768


def _const_spec(shape):
    return pl.BlockSpec(shape, lambda *_: (0,) * len(shape), pipeline_mode=pl.Buffered(1))


def _rms_rows(x, g):
    ms = jnp.mean(x * x, axis=-1, keepdims=True)
    return x * lax.rsqrt(ms + NORM_EPS) * g


def _segment_mean_sq(y, seg_ref, inv_count):
    sq = (y * y).astype(BF16)
    parts = [jnp.dot(sq[:, c:c + MXU_DIM], seg_ref[...], preferred_element_type=F32)
             for c in range(0, y.shape[1], MXU_DIM)]
    ss = parts[0] if len(parts) == 1 else jnp.concatenate(parts, axis=1)
    return ss * inv_count


def _rope_lanes(y, cos, sin_signed, half):
    lane = lax.broadcasted_iota(jnp.int32, y.shape, 1)
    first = (lane % (2 * half)) < half
    rot = jnp.where(first, pltpu.roll(y, LANES - half, 1), pltpu.roll(y, half, 1))
    return y * cos + rot * sin_signed


def _norm_rope_store(y, ms, gain, tab_ref, t_off, half, scale, out_ref):
    yn = y * lax.rsqrt(ms + NORM_EPS)
    cos = tab_ref[:, t_off:t_off + LANES]
    sin = tab_ref[:, t_off + LANES:t_off + 2 * LANES]
    for c in range(0, y.shape[1], LANES):
        r = _rope_lanes(yn[:, c:c + LANES] * gain, cos, sin, half)
        if scale != 1.0:
            r = r * scale
        out_ref[:, c:c + LANES] = r.astype(out_ref.dtype)


def _inproj_body(x_ref, g_ref, w_ref, wq_ref, wkv_ref, seg64_ref, seg128_ref, gains_ref, tab_ref,
                 qa_ref, ka_ref, va_ref, qb_ref, kb_ref, vb_ref, qc_ref, kc_ref, vc_ref, qd_ref):
    h = _rms_rows(x_ref[...], g_ref[...]).astype(BF16)

    def proj(c0, width):
        return jnp.dot(h, w_ref[:, c0:c0 + width], preferred_element_type=F32)

    def gain(row):
        return gains_ref[row:row + 1, :]

    y = proj(C_DQ, 512)
    _norm_rope_store(y, _segment_mean_sq(y, seg64_ref, 1.0 / 64), gain(0), tab_ref, T_A, 32,
                     64 ** -0.5, qa_ref)
    y = proj(C_DK, 512)
    _norm_rope_store(y, _segment_mean_sq(y, seg64_ref, 1.0 / 64), gain(1), tab_ref, T_A, 32, 1.0, ka_ref)
    va_ref[...] = proj(C_DV, 512).astype(BF16)

    g_ql = jnp.concatenate([gain(2), gain(3)], axis=1)
    ql = _rms_rows(proj(C_MQL, 256), g_ql).astype(BF16)
    y = jnp.dot(ql, wq_ref[...], preferred_element_type=F32)
    _norm_rope_store(y, _segment_mean_sq(y, seg128_ref, 1.0 / MLA_QK_DIM), gain(5), tab_ref, T_B, 16,
                     MLA_QK_DIM ** -0.5, qb_ref)
    kvl = _rms_rows(proj(C_MKVL, 128), gain(4)).astype(BF16)
    kv = jnp.dot(kvl, wkv_ref[...], preferred_element_type=F32)
    k_rope = proj(C_KROPE, 128)
    y = kv[:, :1024] + jnp.concatenate([k_rope] * MLA_HEADS, axis=1)
    _norm_rope_store(y, _segment_mean_sq(y, seg128_ref, 1.0 / MLA_QK_DIM), gain(6), tab_ref, T_B, 16,
                     1.0, kb_ref)
    vb_ref[...] = kv[:, 1024:].astype(BF16)

    y = proj(C_GQ, 512)
    _norm_rope_store(y, _segment_mean_sq(y, seg64_ref, 1.0 / 64), gain(7), tab_ref, T_C, 16,
                     64 ** -0.5, qc_ref)
    y = proj(C_GK, 256)
    _norm_rope_store(y, _segment_mean_sq(y, seg64_ref, 1.0 / 64), gain(8), tab_ref, T_C, 16, 1.0, kc_ref)
    vc_ref[...] = proj(C_GV, 256).astype(BF16)

    y = proj(C_MEMQ, 512)
    _norm_rope_store(y, _segment_mean_sq(y, seg128_ref, 1.0 / 128), gain(9), tab_ref, T_D, 64,
                     128 ** -0.5, qd_ref)


def _inproj(x2, g, w_big, wq_up, wkv_up, seg64, seg128, gains, tab, seq):
    n = x2.shape[0]
    tm = min(512, seq)
    tiles_per_seq = seq // tm
    row = lambda w: pl.BlockSpec((tm, w), lambda i: (i, 0))
    out_widths = (512, 512, 512, 1024, 1024, 512, 512, 256, 256, 512)
    return pl.pallas_call(
        _inproj_body,
        name="inproj",
        grid=(n // tm,),
        in_specs=[row(D_MODEL), _const_spec((1, D_MODEL)), _const_spec(w_big.shape),
                  _const_spec(wq_up.shape), _const_spec(wkv_up.shape), _const_spec(seg64.shape),
                  _const_spec(seg128.shape), _const_spec(gains.shape),
                  pl.BlockSpec((tm, tab.shape[1]), lambda i: (i % tiles_per_seq, 0))],
        out_specs=[row(w) for w in out_widths],
        out_shape=[jax.ShapeDtypeStruct((n, w), BF16) for w in out_widths],
        compiler_params=pltpu.CompilerParams(dimension_semantics=("parallel",),
                                             vmem_limit_bytes=VMEM_LIMIT),
    )(x2, g, w_big, wq_up, wkv_up, seg64, seg128, gains, tab)


def _memprep_body(m_ref, g_ref, w_ref, seg128_ref, gain_ref, tab_ref, k_ref, v_ref):
    mn = _rms_rows(m_ref[...], g_ref[...]).astype(BF16)
    kv = jnp.dot(mn, w_ref[...], preferred_element_type=F32)
    y = kv[:, :512]
    _norm_rope_store(y, _segment_mean_sq(y, seg128_ref, 1.0 / 128), gain_ref[...], tab_ref, 0, 64, 1.0, k_ref)
    v_ref[...] = kv[:, 512:].astype(BF16)


def _memprep(mem2, g, w_kv, seg128, gain, tab_m, mem_len):
    n = mem2.shape[0]
    row = lambda w: pl.BlockSpec((mem_len, w), lambda i: (i, 0))
    return pl.pallas_call(
        _memprep_body,
        name="memprep",
        grid=(n // mem_len,),
        in_specs=[row(D_MODEL), _const_spec((1, D_MODEL)), _const_spec(w_kv.shape),
                  _const_spec(seg128.shape), _const_spec((1, LANES)), _const_spec(tab_m.shape)],
        out_specs=[row(512), row(512)],
        out_shape=[jax.ShapeDtypeStruct((n, 512), BF16)] * 2,
        compiler_params=pltpu.CompilerParams(dimension_semantics=("parallel",),
                                             vmem_limit_bytes=VMEM_LIMIT),
    )(mem2, g, w_kv, seg128, gain, tab_m)


def _scores(q, k):
    return lax.dot_general(q, k, (((1,), (1,)), ((), ())), preferred_element_type=F32)


def _exp_rowsum(s):
    e = jnp.exp(s - jnp.max(s, axis=-1, keepdims=True))
    return e, jnp.sum(e, axis=-1, keepdims=True)


def _attend(q, k, v):
    e, l = _exp_rowsum(_scores(q, k))
    return jnp.dot(e.astype(BF16), v, preferred_element_type=F32) / l


def _split_halves(q):
    lo = lax.broadcasted_iota(jnp.int32, q.shape, 1) < 64
    zero = jnp.zeros_like(q)
    return jnp.where(lo, q, zero), jnp.where(lo, zero, q)


def _join_halves(o_lo, o_hi):
    lo = lax.broadcasted_iota(jnp.int32, o_lo.shape, 1) < 64
    return jnp.where(lo, o_lo, o_hi)


def _attn_diff_body(lam_ref, g_ref, q_ref, k_ref, v_ref, o_ref, *, lambda_init):
    q1, q2 = _split_halves(q_ref[...])
    k = k_ref[...]
    e1, l1 = _exp_rowsum(_scores(q1, k))
    e2, l2 = _exp_rowsum(_scores(q2, k))
    lp = lam_ref[...]
    lam = (jnp.exp(jnp.sum(lp[0:1] * lp[1:2], keepdims=True))
           - jnp.exp(jnp.sum(lp[2:3] * lp[3:4], keepdims=True)) + lambda_init)
    a = e1 * (1.0 / l1) - e2 * (lam / l2)
    o = jnp.dot(a.astype(BF16), v_ref[...], preferred_element_type=F32)
    o_ref[...] = (_rms_rows(o, g_ref[...]) * (1.0 - lambda_init)).astype(o_ref.dtype)


def _attn_latent_body(q_ref, k_ref, v_ref, o_ref):
    v = v_ref[...]
    o = [_attend(q_ref[:, c:c + LANES], k_ref[:, c:c + LANES], v) for c in (0, LANES)]
    o_ref[...] = _join_halves(o[0], o[1]).astype(o_ref.dtype)


def _attn_gqa_body(q_ref, k_ref, v_ref, o_ref):
    k, v = k_ref[...], v_ref[...]
    for c in (0, LANES):
        q_lo, q_hi = _split_halves(q_ref[:, c:c + LANES])
        o_ref[:, c:c + LANES] = _join_halves(_attend(q_lo, k, v), _attend(q_hi, k, v)).astype(o_ref.dtype)


def _attn_mem_body(q_ref, k_ref, v_ref, o_ref):
    o_ref[...] = _attend(q_ref[...], k_ref[...], v_ref[...]).astype(o_ref.dtype)


def _attention(name, body, q, k, v, *, groups, q_width, k_width, v_width, o_width, tq, extra=()):
    b, s, _ = q.shape
    kv_len = k.shape[1]
    tq = min(tq, s)
    in_specs = [_const_spec(e.shape) for e in extra] + [
        pl.BlockSpec((None, tq, q_width), lambda bi, gi, qi: (bi, qi, gi)),
        pl.BlockSpec((None, kv_len, k_width), lambda bi, gi, qi: (bi, 0, gi)),
        pl.BlockSpec((None, kv_len, v_width), lambda bi, gi, qi: (bi, 0, gi)),
    ]
    return pl.pallas_call(
        body,
        name=name,
        grid=(b, groups, s // tq),
        in_specs=in_specs,
        out_specs=pl.BlockSpec((None, tq, o_width), lambda bi, gi, qi: (bi, qi, gi)),
        out_shape=jax.ShapeDtypeStruct((b, s, groups * o_width), BF16),
        compiler_params=pltpu.CompilerParams(
            dimension_semantics=("parallel", "parallel", "arbitrary"), vmem_limit_bytes=VMEM_LIMIT),
    )(*extra, q, k, v)


def _merge_body(*refs, with_router):
    if with_router:
        (x_ref, oa_ref, ob_ref, oc_ref, od_ref, g_ref, wg_ref, bg_ref, wb_ref, wo_ref, gf_ref, wr_ref,
         xo_ref, h2_ref, route_ref) = refs
    else:
        (x_ref, oa_ref, ob_ref, oc_ref, od_ref, g_ref, wg_ref, bg_ref, wb_ref, wo_ref, gf_ref,
         xo_ref, h2_ref) = refs
    x = x_ref[...]
    h = _rms_rows(x, g_ref[...]).astype(BF16)
    merged = None
    for b, o_ref in enumerate((oa_ref, ob_ref, oc_ref, od_ref)):
        gate = jax.nn.sigmoid(jnp.dot(h, wg_ref[b], preferred_element_type=F32) + bg_ref[b])
        term = gate * jnp.dot(o_ref[...], wb_ref[b], preferred_element_type=F32)
        merged = term if merged is None else merged + term
    x_new = x + jnp.dot(merged.astype(BF16), wo_ref[...], preferred_element_type=F32)
    xo_ref[...] = x_new
    h2 = _rms_rows(x_new, gf_ref[...])
    h2_ref[...] = h2.astype(h2_ref.dtype)
    if with_router:
        logits = jnp.dot(h2, wr_ref[...], preferred_element_type=F32, precision=lax.Precision.HIGHEST)
        lane = lax.broadcasted_iota(jnp.int32, logits.shape, 1)
        neg = jnp.float32(-jnp.inf)
        l1 = jnp.where(lane < N_EXPERTS, logits, neg)
        m1 = jnp.max(l1, axis=-1, keepdims=True)
        i1 = jnp.min(jnp.where(l1 == m1, lane, LANES), axis=-1, keepdims=True)
        l2 = jnp.where(lane == i1, neg, l1)
        m2 = jnp.max(l2, axis=-1, keepdims=True)
        i2 = jnp.min(jnp.where(l2 == m2, lane, LANES), axis=-1, keepdims=True)
        e2 = jnp.exp(m2 - m1)
        g1 = 1.0 / (1.0 + e2)
        g2 = e2 / (1.0 + e2)
        route = jnp.where(lane == 0, i1.astype(F32),
                          jnp.where(lane == 1, i2.astype(F32),
                                    jnp.where(lane == 2, g1, jnp.where(lane == 3, g2, 0.0))))
        route_ref[...] = route


def _merge(x2, branches, g, wg, bg, wb, wo, gf, w_router=None):
    n = x2.shape[0]
    tm = min(512, n)
    with_router = w_router is not None
    row = lambda w: pl.BlockSpec((tm, w), lambda i: (i, 0))
    in_specs = ([row(D_MODEL)] + [row(512)] * 4
                + [_const_spec((1, D_MODEL)), _const_spec(wg.shape), _const_spec(bg.shape),
                   _const_spec(wb.shape), _const_spec(wo.shape), _const_spec((1, D_MODEL))])
    args = [x2, *branches, g, wg, bg, wb, wo, gf]
    out_specs = [row(D_MODEL), row(D_MODEL)]
    out_shape = [jax.ShapeDtypeStruct((n, D_MODEL), F32),
                 jax.ShapeDtypeStruct((n, D_MODEL), F32 if with_router else BF16)]
    if with_router:
        in_specs.append(_const_spec(w_router.shape))
        args.append(w_router)
        out_specs.append(row(LANES))
        out_shape.append(jax.ShapeDtypeStruct((n, LANES), F32))
    return pl.pallas_call(
        functools.partial(_merge_body, with_router=with_router),
        name="merge_router" if with_router else "merge",
        grid=(n // tm,),
        in_specs=in_specs, out_specs=out_specs, out_shape=out_shape,
        compiler_params=pltpu.CompilerParams(dimension_semantics=("parallel",),
                                             vmem_limit_bytes=VMEM_LIMIT),
    )(*args)


def _swiglu_chunk(xb, w1, w3, w2):
    a = jnp.dot(xb, w1, preferred_element_type=F32)
    b = jnp.dot(xb, w3, preferred_element_type=F32)
    return jnp.dot((jax.nn.silu(a) * b).astype(BF16), w2, preferred_element_type=F32)


def _ffn_body(x_ref, h_ref, w1_ref, w3_ref, w2_ref, o_ref, acc_ref):
    f = pl.program_id(1)
    y = _swiglu_chunk(h_ref[...], w1_ref[...], w3_ref[...], w2_ref[...])

    @pl.when(f == 0)
    def _():
        acc_ref[...] = x_ref[...] + y

    @pl.when(f > 0)
    def _():
        acc_ref[...] += y

    @pl.when(f == pl.num_programs(1) - 1)
    def _():
        o_ref[...] = acc_ref[...]


def _dense_ffn(x2, h2, w1, w3, w2):
    n = x2.shape[0]
    tm = min(1024, n)
    tf = 512
    return pl.pallas_call(
        _ffn_body,
        name="dense_ffn",
        grid=(n // tm, FFN_DIM // tf),
        in_specs=[pl.BlockSpec((tm, D_MODEL), lambda i, f: (i, 0)),
                  pl.BlockSpec((tm, D_MODEL), lambda i, f: (i, 0)),
                  pl.BlockSpec((D_MODEL, tf), lambda i, f: (0, f)),
                  pl.BlockSpec((D_MODEL, tf), lambda i, f: (0, f)),
                  pl.BlockSpec((tf, D_MODEL), lambda i, f: (f, 0))],
        out_specs=pl.BlockSpec((tm, D_MODEL), lambda i, f: (i, 0)),
        out_shape=jax.ShapeDtypeStruct((n, D_MODEL), F32),
        scratch_shapes=[pltpu.VMEM((tm, D_MODEL), F32)],
        compiler_params=pltpu.CompilerParams(dimension_semantics=("parallel", "arbitrary"),
                                             vmem_limit_bytes=VMEM_LIMIT),
    )(x2, h2, w1, w3, w2)


MOE_TF = 896


def _moe_body(expert_ref, count_ref, src_ref, dst_ref, h_hbm, w1_ref, w3_ref, w2_ref, y_hbm,
              xg_ref, xb_ref, acc_ref, gsem, ssem):
    j = pl.program_id(0)
    f = pl.program_id(1)
    count = count_ref[j]
    base = j * MOE_ROWS

    def gather_copy(r):
        tok = src_ref[base + r]
        return pltpu.make_async_copy(h_hbm.at[pl.ds(tok, 1), :], xg_ref.at[pl.ds(r, 1), :], gsem)

    def scatter_copy(r):
        dst = dst_ref[base + r]
        return pltpu.make_async_copy(acc_ref.at[pl.ds(r, 1), :], y_hbm.at[pl.ds(dst, 1), :], ssem)

    @pl.when(jnp.logical_and(count > 0, f == 0))
    def _():
        def start(r, c):
            gather_copy(r).start()
            return c
        lax.fori_loop(0, MOE_ROWS, start, 0)

        def wait(r, c):
            gather_copy(r).wait()
            return c
        lax.fori_loop(0, MOE_ROWS, wait, 0)
        xb_ref[...] = xg_ref[...].astype(BF16)

    @pl.when(count > 0)
    def _():
        y = _swiglu_chunk(xb_ref[...], w1_ref[...], w3_ref[...], w2_ref[...])

        @pl.when(f == 0)
        def _():
            acc_ref[...] = y

        @pl.when(f > 0)
        def _():
            acc_ref[...] += y

    @pl.when(jnp.logical_and(count > 0, f == pl.num_programs(1) - 1))
    def _():
        def start(r, c):
            scatter_copy(r).start()
            return c
        lax.fori_loop(0, count, start, 0)

        def wait(r, c):
            scatter_copy(r).wait()
            return c
        lax.fori_loop(0, count, wait, 0)


def _moe_ffn(h2, block_expert, block_count, slot_src, slot_dst, w1, w3, w2):
    n = h2.shape[0]
    n_blocks = block_expert.shape[0]
    wspec = lambda shape, imap: pl.BlockSpec(shape, imap)
    grid_spec = pltpu.PrefetchScalarGridSpec(
        num_scalar_prefetch=4,
        grid=(n_blocks, FFN_DIM // MOE_TF),
        in_specs=[pl.BlockSpec(memory_space=pl.ANY),
                  wspec((None, D_MODEL, MOE_TF), lambda j, f, e, c, s, d: (e[j], 0, f)),
                  wspec((None, D_MODEL, MOE_TF), lambda j, f, e, c, s, d: (e[j], 0, f)),
                  wspec((None, MOE_TF, D_MODEL), lambda j, f, e, c, s, d: (e[j], f, 0))],
        out_specs=pl.BlockSpec(memory_space=pl.ANY),
        scratch_shapes=[pltpu.VMEM((MOE_ROWS, D_MODEL), F32),
                        pltpu.VMEM((MOE_ROWS, D_MODEL), BF16),
                        pltpu.VMEM((MOE_ROWS, D_MODEL), F32),
                        pltpu.SemaphoreType.DMA(()),
                        pltpu.SemaphoreType.DMA(())],
    )
    return pl.pallas_call(
        _moe_body,
        name="moe_ffn",
        grid_spec=grid_spec,
        out_shape=jax.ShapeDtypeStruct((TOP_K * n, D_MODEL), F32),
        compiler_params=pltpu.CompilerParams(dimension_semantics=("arbitrary", "arbitrary"),
                                             vmem_limit_bytes=VMEM_LIMIT),
    )(block_expert, block_count, slot_src, slot_dst, h2, w1, w3, w2)


def _combine_body(x_ref, y_ref, route_ref, o_ref):
    g1 = route_ref[:, 2:3]
    g2 = route_ref[:, 3:4]
    o_ref[...] = x_ref[...] + (y_ref[:, :D_MODEL] * g1 + y_ref[:, D_MODEL:] * g2)


def _combine(x2, y_pairs, route):
    n = x2.shape[0]
    tm = min(512, n)
    row = lambda w: pl.BlockSpec((tm, w), lambda i: (i, 0))
    return pl.pallas_call(
        _combine_body,
        name="moe_combine",
        grid=(n // tm,),
        in_specs=[row(D_MODEL), row(TOP_K * D_MODEL), row(LANES)],
        out_specs=row(D_MODEL),
        out_shape=jax.ShapeDtypeStruct((n, D_MODEL), F32),
        compiler_params=pltpu.CompilerParams(dimension_semantics=("parallel",),
                                             vmem_limit_bytes=VMEM_LIMIT),
    )(x2, y_pairs, route)


def _route_plan(route, n):
    n_assign = n * TOP_K
    flat_e = route[:, :TOP_K].astype(jnp.int32).reshape(-1)
    onehot = (flat_e[:, None] == jnp.arange(N_EXPERTS, dtype=jnp.int32)[None, :]).astype(jnp.int32)
    ranks = jnp.cumsum(onehot, axis=0) - onehot
    counts = jnp.sum(onehot, axis=0)
    rank = jnp.sum(ranks * onehot, axis=1)
    padded = (counts + MOE_ROWS - 1) // MOE_ROWS * MOE_ROWS
    pad_end = jnp.cumsum(padded)
    pad_start = pad_end - padded
    slot = pad_start[flat_e] + rank
    n_blocks = (n_assign + N_EXPERTS * (MOE_ROWS - 1)) // MOE_ROWS
    slot_assign = jnp.zeros((n_blocks * MOE_ROWS,), jnp.int32).at[slot].set(
        jnp.arange(n_assign, dtype=jnp.int32))
    block_start = jnp.arange(n_blocks, dtype=jnp.int32) * MOE_ROWS
    block_expert = jnp.minimum(jnp.searchsorted(pad_end, block_start, side='right'),
                               N_EXPERTS - 1).astype(jnp.int32)
    filled = counts[block_expert] - (block_start - pad_start[block_expert])
    block_count = jnp.clip(filled, 0, MOE_ROWS).astype(jnp.int32)
    return block_expert, block_count, slot_assign // TOP_K, slot_assign


def _inv_freq(dim):
    return 1.0 / (ROPE_THETA ** (jnp.arange(0, dim, 2, dtype=F32) / dim))


def _rope_table(pos_of_lane, freq_of_lane, first_half, active=None):
    ang = pos_of_lane * freq_of_lane[None, :]
    cos, sin = jnp.cos(ang), jnp.sin(ang) * jnp.where(first_half, -1.0, 1.0)[None, :]
    if active is not None:
        cos = jnp.where(active[None, :], cos, 1.0)
        sin = jnp.where(active[None, :], sin, 0.0)
    return jnp.concatenate([cos, sin], axis=1).astype(F32)


def _rope_tables(seq, mem_len):
    lane = np.arange(LANES)
    pos = jnp.arange(seq, dtype=jnp.int32)
    posf = jnp.broadcast_to(pos.astype(F32)[:, None], (seq, LANES))
    tab_a = _rope_table(posf, _inv_freq(64)[lane % 32], (lane % 64) < 32)
    tab_b = _rope_table(posf, _inv_freq(32)[lane % 16], (lane % 32) < 16, (lane >= 64) & (lane < 96))
    rows = (pos // MEM_GRID_W).astype(F32)[:, None]
    cols = (pos % MEM_GRID_W).astype(F32)[:, None]
    pos_c = jnp.where(((lane % 64) < 32)[None, :], rows, cols)
    tab_c = _rope_table(pos_c, _inv_freq(32)[lane % 16], (lane % 32) < 16)
    f_d = _inv_freq(128)[lane % 64]
    tab_d = _rope_table(posf, f_d, lane < 64)
    mposf = jnp.broadcast_to(jnp.arange(mem_len, dtype=F32)[:, None], (mem_len, LANES))
    tab_m = _rope_table(mposf, f_d, lane < 64)
    return jnp.concatenate([tab_a, tab_b, tab_c, tab_d], axis=1), tab_m


def _segment_matrix(width):
    i = np.arange(MXU_DIM)
    return jnp.asarray((i[:, None] // width) == (i[None, :] // width), dtype=BF16)


def _tile_row(g, reps):
    return jnp.tile(g.astype(F32), reps).reshape(1, -1)


def _layer_params(layer, w_in, w_mla_q_up, w_mla_kv_up, w_mem_kv, gains):
    w = w_in[layer]
    z = lambda c: jnp.zeros((D_MODEL, c), F32)
    gk, gv = w[:, 2464:2592], w[:, 2592:2720]
    dup = lambda m: jnp.concatenate([m[:, :64], m[:, :64], m[:, 64:], m[:, 64:]], axis=1)
    w_big = jnp.concatenate([
        w[:, 0:1536], w[:, 1536:1792], w[:, 1792:1920],
        z(64), w[:, 1920:1952], z(32),
        w[:, 1952:2464], dup(gk), dup(gv), w[:, 2720:3232]], axis=1)
    wq = w_mla_q_up[layer].reshape(256, MLA_HEADS, MLA_QK_DIM)
    wq = jnp.pad(wq, ((0, 0), (0, 0), (0, LANES - MLA_QK_DIM))).reshape(256, MLA_HEADS * LANES)
    wkv = w_mla_kv_up[layer].reshape(128, MLA_HEADS, 128)
    wk = jnp.pad(wkv[:, :, :64], ((0, 0), (0, 0), (0, 64))).reshape(128, MLA_HEADS * LANES)
    wv = wkv[:, :, 64:].reshape(128, MLA_HEADS * 64)
    wm = w_mem_kv[layer].reshape(D_MODEL, 4, 256)
    w_mem = jnp.concatenate([wm[:, :, :128].reshape(D_MODEL, 512), wm[:, :, 128:].reshape(D_MODEL, 512)], axis=1)
    pad96 = lambda g: jnp.pad(g.astype(F32), (0, LANES - MLA_QK_DIM)).reshape(1, LANES)
    (dq_g, dk_g, ql_g, kvl_g, mq_g, mk_g, gq_g, gk_g, memq_g) = gains
    gain_rows = jnp.concatenate([
        _tile_row(dq_g, 2), _tile_row(dk_g, 2), ql_g.astype(F32).reshape(2, LANES),
        kvl_g.astype(F32).reshape(1, LANES), pad96(mq_g), pad96(mk_g),
        _tile_row(gq_g, 2), _tile_row(gk_g, 2), memq_g.astype(F32).reshape(1, LANES),
        jnp.zeros((6, LANES), F32)], axis=0)
    return (w_big.astype(BF16), wq.astype(BF16), jnp.concatenate([wk, wv], axis=1).astype(BF16),
            w_mem.astype(BF16), gain_rows)


def kernel(x, mem, norm_attn_g, norm_mem_g, w_in, diff_q_norm_g, diff_k_norm_g, diff_lambda, diff_subln_g,
           mla_q_lat_g, mla_kv_lat_g, w_mla_q_up, w_mla_kv_up, mla_q_norm_g, mla_k_norm_g, gqa_q_norm_g,
           gqa_k_norm_g, w_mem_kv, mem_q_norm_g, mem_k_norm_g, w_branch, w_gate, b_gate, w_out, norm_ffn_g,
           dense_w1, dense_w3, dense_w2, moe_router, moe_w1, moe_w3, moe_w2):
    batch, seq, _ = x.shape
    mem_len = mem.shape[1]
    n = batch * seq
    depth = w_in.shape[0]
    tab, tab_m = _rope_tables(seq, mem_len)
    seg64, seg128 = _segment_matrix(64), _segment_matrix(128)
    x2 = x.reshape(n, D_MODEL)
    mem2 = mem.reshape(batch * mem_len, D_MODEL)
    row = lambda g: g.astype(F32).reshape(1, -1)
    b3 = lambda a, w: a.reshape(batch, -1, w)

    for layer in range(depth):
        lambda_init = 0.8 - 0.6 * math.exp(-0.3 * layer)
        w_big, wq_up, wkv_up, w_mem, gain_rows = _layer_params(
            layer, w_in, w_mla_q_up, w_mla_kv_up, w_mem_kv,
            (diff_q_norm_g[layer], diff_k_norm_g[layer], mla_q_lat_g[layer], mla_kv_lat_g[layer],
             mla_q_norm_g[layer], mla_k_norm_g[layer], gqa_q_norm_g[layer], gqa_k_norm_g[layer],
             mem_q_norm_g[layer]))
        qa, ka, va, qb, kb, vb, qc, kc, vc, qd = _inproj(
            x2, row(norm_attn_g[layer]), w_big, wq_up, wkv_up, seg64, seg128, gain_rows, tab, seq)
        kd, vd = _memprep(mem2, row(norm_mem_g[layer]), w_mem, seg128, row(mem_k_norm_g[layer]), tab_m, mem_len)

        o_a = _attention("attn_diff", functools.partial(_attn_diff_body, lambda_init=lambda_init),
                         b3(qa, 512), b3(ka, 512), b3(va, 512), groups=DIFF_HEADS, q_width=128,
                         k_width=128, v_width=128, o_width=128, tq=256,
                         extra=(diff_lambda[layer].astype(F32), row(diff_subln_g[layer])))
        o_b = _attention("attn_latent", _attn_latent_body, b3(qb, 1024), b3(kb, 1024), b3(vb, 512), groups=4,
                         q_width=256, k_width=256, v_width=128, o_width=128, tq=256)
        o_c = _attention("attn_gqa", _attn_gqa_body, b3(qc, 512), b3(kc, 256), b3(vc, 256), groups=2,
                         q_width=256, k_width=128, v_width=128, o_width=256, tq=256)
        o_d = _attention("attn_mem", _attn_mem_body, b3(qd, 512), b3(kd, 512), b3(vd, 512), groups=4,
                         q_width=128, k_width=128, v_width=128, o_width=128, tq=512)
        branches = [o.reshape(n, 512) for o in (o_a, o_b, o_c, o_d)]

        i = layer // 2
        routed = layer % 2 == 1
        w_router = None
        if routed:
            w_router = jnp.pad(moe_router[i].astype(F32), ((0, 0), (0, LANES - N_EXPERTS)))
        merged = _merge(x2, branches, row(norm_attn_g[layer]), w_gate[layer].astype(BF16),
                        b_gate[layer].astype(F32).reshape(4, 1, D_MODEL), w_branch[layer].astype(BF16),
                        w_out[layer].astype(BF16), row(norm_ffn_g[layer]), w_router)
        if not routed:
            x_mid, h2 = merged
            x2 = _dense_ffn(x_mid, h2, dense_w1[i].astype(BF16), dense_w3[i].astype(BF16),
                            dense_w2[i].astype(BF16))
        else:
            x_mid, h2, route = merged
            block_expert, block_count, slot_src, slot_dst = _route_plan(route, n)
            y = _moe_ffn(h2, block_expert, block_count, slot_src, slot_dst, moe_w1[i].astype(BF16),
                         moe_w3[i].astype(BF16), moe_w2[i].astype(BF16))
            x2 = _combine(x_mid, y.reshape(n, TOP_K * D_MODEL), route)
    return x2.reshape(batch, seq, D_MODEL)
```

```python
import functools
import math

import numpy as np
import jax
import jax.numpy as jnp
from jax import lax
from jax.experimental import pallas as pl
from jax.experimental.pallas import tpu as pltpu

F32 = jnp.float32
BF16 = jnp.bfloat16

D_MODEL = 1024
MEM_GRID_W = 64
ROPE_THETA = 10000.0
NORM_EPS = 1e-6
LOG2_E = math.log2(math.e)
DIFF_HEADS = 4
MLA_HEADS = 8
MLA_QK_DIM = 96
FFN_DIM = 3584
N_EXPERTS = 8
TOP_K = 2
MOE_ROWS = 512
LANES = 128
MXU_DIM = 256
VMEM_LIMIT = 56 * 1024 * 1024

C_DQ, C_DK, C_DV = 0, 512, 1024
C_MQL, C_MKVL, C_KROPE = 1536, 1792, 1920
C_GQ, C_GK, C_GV, C_MEMQ = 2048, 2560, 2816, 3072
IN_COLS_PADDED = 3584
T_A, T_B, T_C, T_D = 0, 256, 512, 768


def _const_spec(shape):
    return pl.BlockSpec(shape, lambda *_: (0,) * len(shape), pipeline_mode=pl.Buffered(1))


def _rms_rows(x, g):
    ms = jnp.mean(x * x, axis=-1, keepdims=True)
    return x * lax.rsqrt(ms + NORM_EPS) * g


def _segment_mean_sq(y, seg_ref, inv_count):
    sq = (y * y).astype(BF16)
    parts = [jnp.dot(sq[:, c:c + MXU_DIM], seg_ref[...], preferred_element_type=F32)
             for c in range(0, y.shape[1], MXU_DIM)]
    ss = parts[0] if len(parts) == 1 else jnp.concatenate(parts, axis=1)
    return ss * inv_count


def _rope_lanes(y, cos, sin_signed, half):
    lane = lax.broadcasted_iota(jnp.int32, y.shape, 1)
    first = (lane % (2 * half)) < half
    rot = jnp.where(first, pltpu.roll(y, LANES - half, 1), pltpu.roll(y, half, 1))
    return y * cos + rot * sin_signed


def _norm_rope_store(y, ms, gain, tab_ref, t_off, half, scale, out_ref):
    yn = y * lax.rsqrt(ms + NORM_EPS)
    cos = tab_ref[:, t_off:t_off + LANES]
    sin = tab_ref[:, t_off + LANES:t_off + 2 * LANES]
    for c in range(0, y.shape[1], LANES):
        r = _rope_lanes(yn[:, c:c + LANES] * gain, cos, sin, half)
        if scale != 1.0:
            r = r * scale
        out_ref[:, c:c + LANES] = r.astype(out_ref.dtype)


def _inproj_body(x_ref, g_ref, w_ref, wq_ref, wkv_ref, seg64_ref, seg128_ref, gains_ref, tab_ref,
                 qa_ref, ka_ref, va_ref, qb_ref, kb_ref, vb_ref, qc_ref, kc_ref, vc_ref, qd_ref):
    h = _rms_rows(x_ref[...], g_ref[...]).astype(BF16)

    def proj(c0, width):
        return jnp.dot(h, w_ref[:, c0:c0 + width], preferred_element_type=F32)

    def gain(row):
        return gains_ref[row:row + 1, :]

    y = proj(C_DQ, 512)
    _norm_rope_store(y, _segment_mean_sq(y, seg64_ref, 1.0 / 64), gain(0), tab_ref, T_A, 32,
                     LOG2_E * 64 ** -0.5, qa_ref)
    y = proj(C_DK, 512)
    _norm_rope_store(y, _segment_mean_sq(y, seg64_ref, 1.0 / 64), gain(1), tab_ref, T_A, 32, 1.0, ka_ref)
    va_ref[...] = proj(C_DV, 512).astype(BF16)

    g_ql = jnp.concatenate([gain(2), gain(3)], axis=1)
    ql = _rms_rows(proj(C_MQL, 256), g_ql).astype(BF16)
    y = jnp.dot(ql, wq_ref[...], preferred_element_type=F32)
    _norm_rope_store(y, _segment_mean_sq(y, seg128_ref, 1.0 / MLA_QK_DIM), gain(5), tab_ref, T_B, 16,
                     LOG2_E * MLA_QK_DIM ** -0.5, qb_ref)
    kvl = _rms_rows(proj(C_MKVL, 128), gain(4)).astype(BF16)
    kv = jnp.dot(kvl, wkv_ref[...], preferred_element_type=F32)
    k_rope = proj(C_KROPE, 128)
    y = kv[:, :1024] + jnp.concatenate([k_rope] * MLA_HEADS, axis=1)
    _norm_rope_store(y, _segment_mean_sq(y, seg128_ref, 1.0 / MLA_QK_DIM), gain(6), tab_ref, T_B, 16,
                     1.0, kb_ref)
    vb_ref[...] = kv[:, 1024:].astype(BF16)

    y = proj(C_GQ, 512)
    _norm_rope_store(y, _segment_mean_sq(y, seg64_ref, 1.0 / 64), gain(7), tab_ref, T_C, 16,
                     LOG2_E * 64 ** -0.5, qc_ref)
    y = proj(C_GK, 256)
    _norm_rope_store(y, _segment_mean_sq(y, seg64_ref, 1.0 / 64), gain(8), tab_ref, T_C, 16, 1.0, kc_ref)
    vc_ref[...] = proj(C_GV, 256).astype(BF16)

    y = proj(C_MEMQ, 512)
    _norm_rope_store(y, _segment_mean_sq(y, seg128_ref, 1.0 / 128), gain(9), tab_ref, T_D, 64,
                     LOG2_E * 128 ** -0.5, qd_ref)


def _inproj(x2, g, w_big, wq_up, wkv_up, seg64, seg128, gains, tab, seq):
    n = x2.shape[0]
    tm = min(512, seq)
    tiles_per_seq = seq // tm
    row = lambda w: pl.BlockSpec((tm, w), lambda i: (i, 0))
    out_widths = (512, 512, 512, 1024, 1024, 512, 512, 256, 256, 512)
    return pl.pallas_call(
        _inproj_body,
        name="inproj",
        grid=(n // tm,),
        in_specs=[row(D_MODEL), _const_spec((1, D_MODEL)), _const_spec(w_big.shape),
                  _const_spec(wq_up.shape), _const_spec(wkv_up.shape), _const_spec(seg64.shape),
                  _const_spec(seg128.shape), _const_spec(gains.shape),
                  pl.BlockSpec((tm, tab.shape[1]), lambda i: (i % tiles_per_seq, 0))],
        out_specs=[row(w) for w in out_widths],
        out_shape=[jax.ShapeDtypeStruct((n, w), BF16) for w in out_widths],
        compiler_params=pltpu.CompilerParams(dimension_semantics=("parallel",),
                                             vmem_limit_bytes=VMEM_LIMIT),
    )(x2, g, w_big, wq_up, wkv_up, seg64, seg128, gains, tab)


def _memprep_body(m_ref, g_ref, w_ref, seg128_ref, gain_ref, tab_ref, k_ref, v_ref):
    mn = _rms_rows(m_ref[...], g_ref[...]).astype(BF16)
    kv = jnp.dot(mn, w_ref[...], preferred_element_type=F32)
    y = kv[:, :512]
    _norm_rope_store(y, _segment_mean_sq(y, seg128_ref, 1.0 / 128), gain_ref[...], tab_ref, 0, 64, 1.0, k_ref)
    v_ref[...] = kv[:, 512:].astype(BF16)


def _memprep(mem2, g, w_kv, seg128, gain, tab_m, mem_len):
    n = mem2.shape[0]
    row = lambda w: pl.BlockSpec((mem_len, w), lambda i: (i, 0))
    return pl.pallas_call(
        _memprep_body,
        name="memprep",
        grid=(n // mem_len,),
        in_specs=[row(D_MODEL), _const_spec((1, D_MODEL)), _const_spec(w_kv.shape),
                  _const_spec(seg128.shape), _const_spec((1, LANES)), _const_spec(tab_m.shape)],
        out_specs=[row(512), row(512)],
        out_shape=[jax.ShapeDtypeStruct((n, 512), BF16)] * 2,
        compiler_params=pltpu.CompilerParams(dimension_semantics=("parallel",),
                                             vmem_limit_bytes=VMEM_LIMIT),
    )(mem2, g, w_kv, seg128, gain, tab_m)


def _scores(q, k):
    return lax.dot_general(q, k, (((1,), (1,)), ((), ())), preferred_element_type=F32)


def _exp_rowsum(s):
    e = jnp.exp2(s - jnp.max(s, axis=-1, keepdims=True))
    return e, jnp.sum(e, axis=-1, keepdims=True)


def _attend(q, k, v):
    e, l = _exp_rowsum(_scores(q, k))
    return jnp.dot(e.astype(BF16), v, preferred_element_type=F32) / l


def _split_halves(q):
    lo = lax.broadcasted_iota(jnp.int32, q.shape, 1) < 64
    zero = jnp.zeros_like(q)
    return jnp.where(lo, q, zero), jnp.where(lo, zero, q)


def _join_halves(o_lo, o_hi):
    lo = lax.broadcasted_iota(jnp.int32, o_lo.shape, 1) < 64
    return jnp.where(lo, o_lo, o_hi)


def _attn_diff_body(lam_ref, g_ref, q_ref, k_ref, v_ref, o_ref, *, lambda_init):
    q1, q2 = _split_halves(q_ref[...])
    k = k_ref[...]
    e1, l1 = _exp_rowsum(_scores(q1, k))
    e2, l2 = _exp_rowsum(_scores(q2, k))
    lp = lam_ref[...]
    lam = (jnp.exp(jnp.sum(lp[0:1] * lp[1:2], keepdims=True))
           - jnp.exp(jnp.sum(lp[2:3] * lp[3:4], keepdims=True)) + lambda_init)
    a = e1 * (1.0 / l1) - e2 * (lam / l2)
    o = jnp.dot(a.astype(BF16), v_ref[...], preferred_element_type=F32)
    o_ref[...] = (_rms_rows(o, g_ref[...]) * (1.0 - lambda_init)).astype(o_ref.dtype)


def _attn_latent_body(q_ref, k_ref, v_ref, o_ref):
    v = v_ref[...]
    o = [_attend(q_ref[:, c:c + LANES], k_ref[:, c:c + LANES], v) for c in (0, LANES)]
    o_ref[...] = _join_halves(o[0], o[1]).astype(o_ref.dtype)


def _attn_gqa_body(q_ref, k_ref, v_ref, o_ref):
    k, v = k_ref[...], v_ref[...]
    for c in (0, LANES):
        q_lo, q_hi = _split_halves(q_ref[:, c:c + LANES])
        o_ref[:, c:c + LANES] = _join_halves(_attend(q_lo, k, v), _attend(q_hi, k, v)).astype(o_ref.dtype)


def _attn_mem_body(q_ref, k_ref, v_ref, o_ref):
    o_ref[...] = _attend(q_ref[...], k_ref[...], v_ref[...]).astype(o_ref.dtype)


def _attention(name, body, q, k, v, *, groups, q_width, k_width, v_width, o_width, tq, extra=()):
    b, s, _ = q.shape
    kv_len = k.shape[1]
    tq = min(tq, s)
    in_specs = [_const_spec(e.shape) for e in extra] + [
        pl.BlockSpec((None, tq, q_width), lambda bi, gi, qi: (bi, qi, gi)),
        pl.BlockSpec((None, kv_len, k_width), lambda bi, gi, qi: (bi, 0, gi)),
        pl.BlockSpec((None, kv_len, v_width), lambda bi, gi, qi: (bi, 0, gi)),
    ]
    return pl.pallas_call(
        body,
        name=name,
        grid=(b, groups, s // tq),
        in_specs=in_specs,
        out_specs=pl.BlockSpec((None, tq, o_width), lambda bi, gi, qi: (bi, qi, gi)),
        out_shape=jax.ShapeDtypeStruct((b, s, groups * o_width), BF16),
        compiler_params=pltpu.CompilerParams(
            dimension_semantics=("parallel", "parallel", "arbitrary"), vmem_limit_bytes=VMEM_LIMIT),
    )(*extra, q, k, v)


def _merge_body(*refs, with_router):
    if with_router:
        (x_ref, oa_ref, ob_ref, oc_ref, od_ref, g_ref, wg_ref, bg_ref, wb_ref, wo_ref, gf_ref, wr_ref,
         xo_ref, h2_ref, route_ref) = refs
    else:
        (x_ref, oa_ref, ob_ref, oc_ref, od_ref, g_ref, wg_ref, bg_ref, wb_ref, wo_ref, gf_ref,
         xo_ref, h2_ref) = refs
    x = x_ref[...]
    h = _rms_rows(x, g_ref[...]).astype(BF16)
    merged = None
    for b, o_ref in enumerate((oa_ref, ob_ref, oc_ref, od_ref)):
        gate = jax.nn.sigmoid(jnp.dot(h, wg_ref[b], preferred_element_type=F32) + bg_ref[b])
        term = gate * jnp.dot(o_ref[...], wb_ref[b], preferred_element_type=F32)
        merged = term if merged is None else merged + term
    x_new = x + jnp.dot(merged.astype(BF16), wo_ref[...], preferred_element_type=F32)
    xo_ref[...] = x_new
    h2 = _rms_rows(x_new, gf_ref[...])
    h2_ref[...] = h2.astype(h2_ref.dtype)
    if with_router:
        h_hi = h2.astype(BF16)
        h_lo = (h2 - h_hi.astype(F32)).astype(BF16)
        logits = (jnp.dot(h_hi, wr_ref[0], preferred_element_type=F32)
                  + (jnp.dot(h_hi, wr_ref[1], preferred_element_type=F32)
                     + jnp.dot(h_lo, wr_ref[0], preferred_element_type=F32)))
        lane = lax.broadcasted_iota(jnp.int32, logits.shape, 1)
        neg = jnp.float32(-jnp.inf)
        l1 = jnp.where(lane < N_EXPERTS, logits, neg)
        m1 = jnp.max(l1, axis=-1, keepdims=True)
        i1 = jnp.min(jnp.where(l1 == m1, lane, LANES), axis=-1, keepdims=True)
        l2 = jnp.where(lane == i1, neg, l1)
        m2 = jnp.max(l2, axis=-1, keepdims=True)
        i2 = jnp.min(jnp.where(l2 == m2, lane, LANES), axis=-1, keepdims=True)
        e2 = jnp.exp(m2 - m1)
        g1 = 1.0 / (1.0 + e2)
        g2 = e2 / (1.0 + e2)
        route = jnp.where(lane == 0, i1.astype(F32),
                          jnp.where(lane == 1, i2.astype(F32),
                                    jnp.where(lane == 2, g1, jnp.where(lane == 3, g2, 0.0))))
        route_ref[...] = route


def _merge(x2, branches, g, wg, bg, wb, wo, gf, w_router=None):
    n = x2.shape[0]
    tm = min(512, n)
    with_router = w_router is not None
    row = lambda w: pl.BlockSpec((tm, w), lambda i: (i, 0))
    in_specs = ([row(D_MODEL)] + [row(512)] * 4
                + [_const_spec((1, D_MODEL)), _const_spec(wg.shape), _const_spec(bg.shape),
                   _const_spec(wb.shape), _const_spec(wo.shape), _const_spec((1, D_MODEL))])
    args = [x2, *branches, g, wg, bg, wb, wo, gf]
    out_specs = [row(D_MODEL), row(D_MODEL)]
    out_shape = [jax.ShapeDtypeStruct((n, D_MODEL), F32),
                 jax.ShapeDtypeStruct((n, D_MODEL), F32 if with_router else BF16)]
    if with_router:
        in_specs.append(_const_spec(w_router.shape))
        args.append(w_router)
        out_specs.append(row(LANES))
        out_shape.append(jax.ShapeDtypeStruct((n, LANES), F32))
    return pl.pallas_call(
        functools.partial(_merge_body, with_router=with_router),
        name="merge_router" if with_router else "merge",
        grid=(n // tm,),
        in_specs=in_specs, out_specs=out_specs, out_shape=out_shape,
        compiler_params=pltpu.CompilerParams(dimension_semantics=("parallel",),
                                             vmem_limit_bytes=VMEM_LIMIT),
    )(*args)


def _swiglu_chunk(xb, w1, w3, w2):
    a = jnp.dot(xb, w1, preferred_element_type=F32)
    b = jnp.dot(xb, w3, preferred_element_type=F32)
    return jnp.dot((jax.nn.silu(a) * b).astype(BF16), w2, preferred_element_type=F32)


def _ffn_body(x_ref, h_ref, w1_ref, w3_ref, w2_ref, o_ref, acc_ref):
    f = pl.program_id(1)
    y = _swiglu_chunk(h_ref[...], w1_ref[...], w3_ref[...], w2_ref[...])

    @pl.when(f == 0)
    def _():
        acc_ref[...] = x_ref[...] + y

    @pl.when(f > 0)
    def _():
        acc_ref[...] += y

    @pl.when(f == pl.num_programs(1) - 1)
    def _():
        o_ref[...] = acc_ref[...]


def _dense_ffn(x2, h2, w1, w3, w2):
    n = x2.shape[0]
    tm = min(1024, n)
    tf = 512
    return pl.pallas_call(
        _ffn_body,
        name="dense_ffn",
        grid=(n // tm, FFN_DIM // tf),
        in_specs=[pl.BlockSpec((tm, D_MODEL), lambda i, f: (i, 0)),
                  pl.BlockSpec((tm, D_MODEL), lambda i, f: (i, 0)),
                  pl.BlockSpec((D_MODEL, tf), lambda i, f: (0, f)),
                  pl.BlockSpec((D_MODEL, tf), lambda i, f: (0, f)),
                  pl.BlockSpec((tf, D_MODEL), lambda i, f: (f, 0))],
        out_specs=pl.BlockSpec((tm, D_MODEL), lambda i, f: (i, 0)),
        out_shape=jax.ShapeDtypeStruct((n, D_MODEL), F32),
        scratch_shapes=[pltpu.VMEM((tm, D_MODEL), F32)],
        compiler_params=pltpu.CompilerParams(dimension_semantics=("parallel", "arbitrary"),
                                             vmem_limit_bytes=VMEM_LIMIT),
    )(x2, h2, w1, w3, w2)


MOE_TF = 896


def _moe_body(expert_ref, src_ref, dst_ref, h_hbm, w1_ref, w3_ref, w2_ref, y_hbm,
              xg_ref, xb_ref, acc_ref, gsem, ssem):
    j = pl.program_id(0)
    f = pl.program_id(1)
    n_f = pl.num_programs(1)
    slot = j % 2
    other = 1 - slot
    rows_per_step = MOE_ROWS // (FFN_DIM // MOE_TF)

    def gather_copy(block, r, buf):
        tok = src_ref[block * MOE_ROWS + r]
        return pltpu.make_async_copy(h_hbm.at[pl.ds(tok, 1), :], xg_ref.at[buf, pl.ds(r, 1), :],
                                     gsem.at[buf])

    def scatter_copy(block, r, buf):
        dst = dst_ref[(block + 1) * MOE_ROWS + r]
        return pltpu.make_async_copy(acc_ref.at[buf, pl.ds(r, 1), :], y_hbm.at[pl.ds(dst, 1), :], ssem)

    def wait_rows(copy):
        for _ in range(MOE_ROWS):
            copy.wait()

    @pl.when(jnp.logical_and(j == 0, f == 0))
    def _():
        def start(r, c):
            gather_copy(0, r, 0).start()
            return c
        lax.fori_loop(0, MOE_ROWS, start, 0)
        acc_ref[1] = jnp.zeros((MOE_ROWS, D_MODEL), F32)

    @pl.when(f == 0)
    def _():
        wait_rows(gather_copy(j, 0, slot))
        xb_ref[...] = xg_ref[slot].astype(BF16)

    for r in range(rows_per_step):
        row = f * rows_per_step + r
        gather_copy(j + 1, row, other).start()
        scatter_copy(j - 1, row, other).start()
    y = _swiglu_chunk(xb_ref[...], w1_ref[...], w3_ref[...], w2_ref[...])

    @pl.when(f == 0)
    def _():
        acc_ref[slot] = y

    @pl.when(f > 0)
    def _():
        acc_ref[slot] += y

    @pl.when(f == n_f - 1)
    def _():
        wait_rows(scatter_copy(j - 1, 0, other))

    @pl.when(jnp.logical_and(j == pl.num_programs(0) - 1, f == n_f - 1))
    def _():
        def start(r, c):
            scatter_copy(j, r, slot).start()
            return c
        lax.fori_loop(0, MOE_ROWS, start, 0)
        wait_rows(scatter_copy(j, 0, slot))
        wait_rows(gather_copy(j + 1, 0, other))


def _moe_ffn(h2, block_expert, slot_src, slot_dst, w1, w3, w2):
    n = h2.shape[0]
    n_blocks = block_expert.shape[0]
    wspec = lambda shape, imap: pl.BlockSpec(shape, imap)
    grid_spec = pltpu.PrefetchScalarGridSpec(
        num_scalar_prefetch=3,
        grid=(n_blocks, FFN_DIM // MOE_TF),
        in_specs=[pl.BlockSpec(memory_space=pl.ANY),
                  wspec((None, D_MODEL, MOE_TF), lambda j, f, e, s, d: (e[j], 0, f)),
                  wspec((None, D_MODEL, MOE_TF), lambda j, f, e, s, d: (e[j], 0, f)),
                  wspec((None, MOE_TF, D_MODEL), lambda j, f, e, s, d: (e[j], f, 0))],
        out_specs=pl.BlockSpec(memory_space=pl.ANY),
        scratch_shapes=[pltpu.VMEM((2, MOE_ROWS, D_MODEL), F32),
                        pltpu.VMEM((MOE_ROWS, D_MODEL), BF16),
                        pltpu.VMEM((2, MOE_ROWS, D_MODEL), F32),
                        pltpu.SemaphoreType.DMA((2,)),
                        pltpu.SemaphoreType.DMA(())],
    )
    return pl.pallas_call(
        _moe_body,
        name="moe_ffn",
        grid_spec=grid_spec,
        out_shape=jax.ShapeDtypeStruct((TOP_K * n + MOE_ROWS, D_MODEL), F32),
        compiler_params=pltpu.CompilerParams(dimension_semantics=("arbitrary", "arbitrary"),
                                             vmem_limit_bytes=VMEM_LIMIT),
    )(block_expert, slot_src, slot_dst, h2, w1, w3, w2)


def _combine_body(x_ref, y1_ref, y2_ref, route_ref, o_ref):
    g1 = route_ref[:, 2:3]
    g2 = route_ref[:, 3:4]
    o_ref[...] = x_ref[...] + (y1_ref[...] * g1 + y2_ref[...] * g2)


def _combine(x2, y, route):
    n = x2.shape[0]
    tm = min(512, n)
    row = lambda w: pl.BlockSpec((tm, w), lambda i: (i, 0))
    return pl.pallas_call(
        _combine_body,
        name="moe_combine",
        grid=(n // tm,),
        in_specs=[row(D_MODEL), row(D_MODEL), pl.BlockSpec((tm, D_MODEL), lambda i: (i + n // tm, 0)),
                  row(LANES)],
        out_specs=row(D_MODEL),
        out_shape=jax.ShapeDtypeStruct((n, D_MODEL), F32),
        compiler_params=pltpu.CompilerParams(dimension_semantics=("parallel",),
                                             vmem_limit_bytes=VMEM_LIMIT),
    )(x2, y, y, route)


def _route_plan(route, n):
    n_assign = n * TOP_K
    flat_e = route[:, :TOP_K].astype(jnp.int32).reshape(-1)
    onehot = (flat_e[:, None] == jnp.arange(N_EXPERTS, dtype=jnp.int32)[None, :]).astype(jnp.int32)
    ranks = jnp.cumsum(onehot, axis=0) - onehot
    counts = jnp.sum(onehot, axis=0)
    rank = jnp.sum(ranks * onehot, axis=1)
    padded = (counts + MOE_ROWS - 1) // MOE_ROWS * MOE_ROWS
    pad_end = jnp.cumsum(padded)
    pad_start = pad_end - padded
    slot = pad_start[flat_e] + rank
    n_blocks = (n_assign + N_EXPERTS * (MOE_ROWS - 1)) // MOE_ROWS
    pad_ids = n_assign + jnp.arange(MOE_ROWS, dtype=jnp.int32)
    slot_assign = jnp.tile(pad_ids, n_blocks).at[slot].set(jnp.arange(n_assign, dtype=jnp.int32))
    valid = slot_assign < n_assign
    slot_src = jnp.where(valid, slot_assign // TOP_K, 0)
    slot_dst = jnp.where(valid, (slot_assign % TOP_K) * n + slot_assign // TOP_K, slot_assign)
    block_start = jnp.arange(n_blocks, dtype=jnp.int32) * MOE_ROWS
    block_expert = jnp.minimum(jnp.searchsorted(pad_end, block_start, side='right'),
                               N_EXPERTS - 1).astype(jnp.int32)
    return (block_expert, jnp.concatenate([slot_src, jnp.zeros((MOE_ROWS,), jnp.int32)]),
            jnp.concatenate([pad_ids, slot_dst]))


def _inv_freq(dim):
    return 1.0 / (ROPE_THETA ** (jnp.arange(0, dim, 2, dtype=F32) / dim))


def _rope_table(pos_of_lane, freq_of_lane, first_half, active=None):
    ang = pos_of_lane * freq_of_lane[None, :]
    cos, sin = jnp.cos(ang), jnp.sin(ang) * jnp.where(first_half, -1.0, 1.0)[None, :]
    if active is not None:
        cos = jnp.where(active[None, :], cos, 1.0)
        sin = jnp.where(active[None, :], sin, 0.0)
    return jnp.concatenate([cos, sin], axis=1).astype(F32)


def _rope_tables(seq, mem_len):
    lane = np.arange(LANES)
    pos = jnp.arange(seq, dtype=jnp.int32)
    posf = jnp.broadcast_to(pos.astype(F32)[:, None], (seq, LANES))
    tab_a = _rope_table(posf, _inv_freq(64)[lane % 32], (lane % 64) < 32)
    tab_b = _rope_table(posf, _inv_freq(32)[lane % 16], (lane % 32) < 16, (lane >= 64) & (lane < 96))
    rows = (pos // MEM_GRID_W).astype(F32)[:, None]
    cols = (pos % MEM_GRID_W).astype(F32)[:, None]
    pos_c = jnp.where(((lane % 64) < 32)[None, :], rows, cols)
    tab_c = _rope_table(pos_c, _inv_freq(32)[lane % 16], (lane % 32) < 16)
    f_d = _inv_freq(128)[lane % 64]
    tab_d = _rope_table(posf, f_d, lane < 64)
    mposf = jnp.broadcast_to(jnp.arange(mem_len, dtype=F32)[:, None], (mem_len, LANES))
    tab_m = _rope_table(mposf, f_d, lane < 64)
    return jnp.concatenate([tab_a, tab_b, tab_c, tab_d], axis=1), tab_m


def _segment_matrix(width):
    i = np.arange(MXU_DIM)
    return jnp.asarray((i[:, None] // width) == (i[None, :] // width), dtype=BF16)


def _tile_row(g, reps):
    return jnp.tile(g.astype(F32), reps).reshape(1, -1)


def _layer_params(layer, w_in, w_mla_q_up, w_mla_kv_up, w_mem_kv, gains):
    w = w_in[layer]
    z = lambda c: jnp.zeros((D_MODEL, c), F32)
    gk, gv = w[:, 2464:2592], w[:, 2592:2720]
    dup = lambda m: jnp.concatenate([m[:, :64], m[:, :64], m[:, 64:], m[:, 64:]], axis=1)
    w_big = jnp.concatenate([
        w[:, 0:1536], w[:, 1536:1792], w[:, 1792:1920],
        z(64), w[:, 1920:1952], z(32),
        w[:, 1952:2464], dup(gk), dup(gv), w[:, 2720:3232]], axis=1)
    wq = w_mla_q_up[layer].reshape(256, MLA_HEADS, MLA_QK_DIM)
    wq = jnp.pad(wq, ((0, 0), (0, 0), (0, LANES - MLA_QK_DIM))).reshape(256, MLA_HEADS * LANES)
    wkv = w_mla_kv_up[layer].reshape(128, MLA_HEADS, 128)
    wk = jnp.pad(wkv[:, :, :64], ((0, 0), (0, 0), (0, 64))).reshape(128, MLA_HEADS * LANES)
    wv = wkv[:, :, 64:].reshape(128, MLA_HEADS * 64)
    wm = w_mem_kv[layer].reshape(D_MODEL, 4, 256)
    w_mem = jnp.concatenate([wm[:, :, :128].reshape(D_MODEL, 512), wm[:, :, 128:].reshape(D_MODEL, 512)], axis=1)
    pad96 = lambda g: jnp.pad(g.astype(F32), (0, LANES - MLA_QK_DIM)).reshape(1, LANES)
    (dq_g, dk_g, ql_g, kvl_g, mq_g, mk_g, gq_g, gk_g, memq_g) = gains
    gain_rows = jnp.concatenate([
        _tile_row(dq_g, 2), _tile_row(dk_g, 2), ql_g.astype(F32).reshape(2, LANES),
        kvl_g.astype(F32).reshape(1, LANES), pad96(mq_g), pad96(mk_g),
        _tile_row(gq_g, 2), _tile_row(gk_g, 2), memq_g.astype(F32).reshape(1, LANES),
        jnp.zeros((6, LANES), F32)], axis=0)
    return (w_big.astype(BF16), wq.astype(BF16), jnp.concatenate([wk, wv], axis=1).astype(BF16),
            w_mem.astype(BF16), gain_rows)


def kernel(x, mem, norm_attn_g, norm_mem_g, w_in, diff_q_norm_g, diff_k_norm_g, diff_lambda, diff_subln_g,
           mla_q_lat_g, mla_kv_lat_g, w_mla_q_up, w_mla_kv_up, mla_q_norm_g, mla_k_norm_g, gqa_q_norm_g,
           gqa_k_norm_g, w_mem_kv, mem_q_norm_g, mem_k_norm_g, w_branch, w_gate, b_gate, w_out, norm_ffn_g,
           dense_w1, dense_w3, dense_w2, moe_router, moe_w1, moe_w3, moe_w2):
    batch, seq, _ = x.shape
    mem_len = mem.shape[1]
    n = batch * seq
    depth = w_in.shape[0]
    tab, tab_m = _rope_tables(seq, mem_len)
    seg64, seg128 = _segment_matrix(64), _segment_matrix(128)
    x2 = x.reshape(n, D_MODEL)
    mem2 = mem.reshape(batch * mem_len, D_MODEL)
    row = lambda g: g.astype(F32).reshape(1, -1)
    b3 = lambda a, w: a.reshape(batch, -1, w)

    for layer in range(depth):
        lambda_init = 0.8 - 0.6 * math.exp(-0.3 * layer)
        w_big, wq_up, wkv_up, w_mem, gain_rows = _layer_params(
            layer, w_in, w_mla_q_up, w_mla_kv_up, w_mem_kv,
            (diff_q_norm_g[layer], diff_k_norm_g[layer], mla_q_lat_g[layer], mla_kv_lat_g[layer],
             mla_q_norm_g[layer], mla_k_norm_g[layer], gqa_q_norm_g[layer], gqa_k_norm_g[layer],
             mem_q_norm_g[layer]))
        qa, ka, va, qb, kb, vb, qc, kc, vc, qd = _inproj(
            x2, row(norm_attn_g[layer]), w_big, wq_up, wkv_up, seg64, seg128, gain_rows, tab, seq)
        kd, vd = _memprep(mem2, row(norm_mem_g[layer]), w_mem, seg128, row(mem_k_norm_g[layer]), tab_m, mem_len)

        o_a = _attention("attn_diff", functools.partial(_attn_diff_body, lambda_init=lambda_init),
                         b3(qa, 512), b3(ka, 512), b3(va, 512), groups=DIFF_HEADS, q_width=128,
                         k_width=128, v_width=128, o_width=128, tq=256,
                         extra=(diff_lambda[layer].astype(F32), row(diff_subln_g[layer])))
        o_b = _attention("attn_latent", _attn_latent_body, b3(qb, 1024), b3(kb, 1024), b3(vb, 512), groups=4,
                         q_width=256, k_width=256, v_width=128, o_width=128, tq=256)
        o_c = _attention("attn_gqa", _attn_gqa_body, b3(qc, 512), b3(kc, 256), b3(vc, 256), groups=2,
                         q_width=256, k_width=128, v_width=128, o_width=256, tq=256)
        o_d = _attention("attn_mem", _attn_mem_body, b3(qd, 512), b3(kd, 512), b3(vd, 512), groups=4,
                         q_width=128, k_width=128, v_width=128, o_width=128, tq=512)
        branches = [o.reshape(n, 512) for o in (o_a, o_b, o_c, o_d)]

        i = layer // 2
        routed = layer % 2 == 1
        w_router = None
        if routed:
            w_r = jnp.pad(moe_router[i].astype(F32), ((0, 0), (0, LANES - N_EXPERTS)))
            w_r_hi = w_r.astype(BF16)
            w_router = jnp.stack([w_r_hi, (w_r - w_r_hi.astype(F32)).astype(BF16)])
        merged = _merge(x2, branches, row(norm_attn_g[layer]), w_gate[layer].astype(BF16),
                        b_gate[layer].astype(F32).reshape(4, 1, D_MODEL), w_branch[layer].astype(BF16),
                        w_out[layer].astype(BF16), row(norm_ffn_g[layer]), w_router)
        if not routed:
            x_mid, h2 = merged
            x2 = _dense_ffn(x_mid, h2, dense_w1[i].astype(BF16), dense_w3[i].astype(BF16),
                            dense_w2[i].astype(BF16))
        else:
            x_mid, h2, route = merged
            block_expert, slot_src, slot_dst = _route_plan(route, n)
            y = _moe_ffn(h2, block_expert, slot_src, slot_dst, moe_w1[i].astype(BF16),
                         moe_w3[i].astype(BF16), moe_w2[i].astype(BF16))
            x2 = _combine(x_mid, y, route)
    return x2.reshape(batch, seq, D_MODEL)
```

```python
import functools
import math

import numpy as np
import jax
import jax.numpy as jnp
from jax import lax
from jax.experimental import pallas as pl
from jax.experimental.pallas import tpu as pltpu

F32 = jnp.float32
BF16 = jnp.bfloat16

D_MODEL = 1024
MEM_GRID_W = 64
ROPE_THETA = 10000.0
NORM_EPS = 1e-6
LOG2_E = math.log2(math.e)
DIFF_HEADS = 4
MLA_HEADS = 8
MLA_QK_DIM = 96
FFN_DIM = 3584
N_EXPERTS = 8
TOP_K = 2
MOE_ROWS = 512
LANES = 128
MXU_DIM = 256
VMEM_LIMIT = 56 * 1024 * 1024

C_DQ, C_DK, C_DV = 0, 512, 1024
C_MQL, C_MKVL, C_KROPE = 1536, 1792, 1920
C_GQ, C_GK, C_GV, C_MEMQ = 2048, 2560, 2816, 3072
IN_COLS_PADDED = 3584
T_A, T_B, T_C, T_D = 0, 256, 512, 768


def _const_spec(shape):
    return pl.BlockSpec(shape, lambda *_: (0,) * len(shape), pipeline_mode=pl.Buffered(1))


def _rms_rows(x, g):
    ms = jnp.mean(x * x, axis=-1, keepdims=True)
    return x * lax.rsqrt(ms + NORM_EPS) * g


def _segment_mean_sq(y, seg_ref, inv_count):
    sq = (y * y).astype(BF16)
    parts = [jnp.dot(sq[:, c:c + MXU_DIM], seg_ref[...], preferred_element_type=F32)
             for c in range(0, y.shape[1], MXU_DIM)]
    ss = parts[0] if len(parts) == 1 else jnp.concatenate(parts, axis=1)
    return ss * inv_count


def _rope_lanes(y, cos, sin_signed, half):
    lane = lax.broadcasted_iota(jnp.int32, y.shape, 1)
    first = (lane % (2 * half)) < half
    rot = jnp.where(first, pltpu.roll(y, LANES - half, 1), pltpu.roll(y, half, 1))
    return y * cos + rot * sin_signed


def _norm_rope_store(y, ms, gain, tab_ref, t_off, half, scale, out_ref):
    yn = y * lax.rsqrt(ms + NORM_EPS)
    cos = tab_ref[:, t_off:t_off + LANES]
    sin = tab_ref[:, t_off + LANES:t_off + 2 * LANES]
    for c in range(0, y.shape[1], LANES):
        r = _rope_lanes(yn[:, c:c + LANES] * gain, cos, sin, half)
        if scale != 1.0:
            r = r * scale
        out_ref[:, c:c + LANES] = r.astype(out_ref.dtype)


def _inproj_body(x_ref, g_ref, w_ref, wq_ref, wkv_ref, seg64_ref, seg128_ref, gains_ref, tab_ref,
                 qa_ref, ka_ref, va_ref, qb_ref, kb_ref, vb_ref, qc_ref, kc_ref, vc_ref, qd_ref):
    h = _rms_rows(x_ref[...], g_ref[...]).astype(BF16)

    def proj(c0, width):
        return jnp.dot(h, w_ref[:, c0:c0 + width], preferred_element_type=F32)

    def gain(row):
        return gains_ref[row:row + 1, :]

    y = proj(C_DQ, 512)
    _norm_rope_store(y, _segment_mean_sq(y, seg64_ref, 1.0 / 64), gain(0), tab_ref, T_A, 32,
                     LOG2_E * 64 ** -0.5, qa_ref)
    y = proj(C_DK, 512)
    _norm_rope_store(y, _segment_mean_sq(y, seg64_ref, 1.0 / 64), gain(1), tab_ref, T_A, 32, 1.0, ka_ref)
    va_ref[...] = proj(C_DV, 512).astype(BF16)

    g_ql = jnp.concatenate([gain(2), gain(3)], axis=1)
    ql = _rms_rows(proj(C_MQL, 256), g_ql).astype(BF16)
    y = jnp.dot(ql, wq_ref[...], preferred_element_type=F32)
    _norm_rope_store(y, _segment_mean_sq(y, seg128_ref, 1.0 / MLA_QK_DIM), gain(5), tab_ref, T_B, 16,
                     LOG2_E * MLA_QK_DIM ** -0.5, qb_ref)
    kvl = _rms_rows(proj(C_MKVL, 128), gain(4)).astype(BF16)
    kv = jnp.dot(kvl, wkv_ref[...], preferred_element_type=F32)
    k_rope = proj(C_KROPE, 128)
    y = kv[:, :1024] + jnp.concatenate([k_rope] * MLA_HEADS, axis=1)
    _norm_rope_store(y, _segment_mean_sq(y, seg128_ref, 1.0 / MLA_QK_DIM), gain(6), tab_ref, T_B, 16,
                     1.0, kb_ref)
    vb_ref[...] = kv[:, 1024:].astype(BF16)

    y = proj(C_GQ, 512)
    _norm_rope_store(y, _segment_mean_sq(y, seg64_ref, 1.0 / 64), gain(7), tab_ref, T_C, 16,
                     LOG2_E * 64 ** -0.5, qc_ref)
    y = proj(C_GK, 256)
    _norm_rope_store(y, _segment_mean_sq(y, seg64_ref, 1.0 / 64), gain(8), tab_ref, T_C, 16, 1.0, kc_ref)
    vc_ref[...] = proj(C_GV, 256).astype(BF16)

    y = proj(C_MEMQ, 512)
    _norm_rope_store(y, _segment_mean_sq(y, seg128_ref, 1.0 / 128), gain(9), tab_ref, T_D, 64,
                     LOG2_E * 128 ** -0.5, qd_ref)


def _inproj(x2, g, w_big, wq_up, wkv_up, seg64, seg128, gains, tab, seq):
    n = x2.shape[0]
    tm = min(512, seq)
    tiles_per_seq = seq // tm
    row = lambda w: pl.BlockSpec((tm, w), lambda i: (i, 0))
    out_widths = (512, 512, 512, 1024, 1024, 512, 512, 256, 256, 512)
    return pl.pallas_call(
        _inproj_body,
        name="inproj",
        grid=(n // tm,),
        in_specs=[row(D_MODEL), _const_spec((1, D_MODEL)), _const_spec(w_big.shape),
                  _const_spec(wq_up.shape), _const_spec(wkv_up.shape), _const_spec(seg64.shape),
                  _const_spec(seg128.shape), _const_spec(gains.shape),
                  pl.BlockSpec((tm, tab.shape[1]), lambda i: (i % tiles_per_seq, 0))],
        out_specs=[row(w) for w in out_widths],
        out_shape=[jax.ShapeDtypeStruct((n, w), BF16) for w in out_widths],
        compiler_params=pltpu.CompilerParams(dimension_semantics=("parallel",),
                                             vmem_limit_bytes=VMEM_LIMIT),
    )(x2, g, w_big, wq_up, wkv_up, seg64, seg128, gains, tab)


def _memprep_body(m_ref, g_ref, w_ref, seg128_ref, gain_ref, tab_ref, k_ref, v_ref):
    mn = _rms_rows(m_ref[...], g_ref[...]).astype(BF16)
    kv = jnp.dot(mn, w_ref[...], preferred_element_type=F32)
    y = kv[:, :512]
    _norm_rope_store(y, _segment_mean_sq(y, seg128_ref, 1.0 / 128), gain_ref[...], tab_ref, 0, 64, 1.0, k_ref)
    v_ref[...] = kv[:, 512:].astype(BF16)


def _memprep(mem2, g, w_kv, seg128, gain, tab_m, mem_len):
    n = mem2.shape[0]
    row = lambda w: pl.BlockSpec((mem_len, w), lambda i: (i, 0))
    return pl.pallas_call(
        _memprep_body,
        name="memprep",
        grid=(n // mem_len,),
        in_specs=[row(D_MODEL), _const_spec((1, D_MODEL)), _const_spec(w_kv.shape),
                  _const_spec(seg128.shape), _const_spec((1, LANES)), _const_spec(tab_m.shape)],
        out_specs=[row(512), row(512)],
        out_shape=[jax.ShapeDtypeStruct((n, 512), BF16)] * 2,
        compiler_params=pltpu.CompilerParams(dimension_semantics=("parallel",),
                                             vmem_limit_bytes=VMEM_LIMIT),
    )(mem2, g, w_kv, seg128, gain, tab_m)


def _attend(q, k, v):
    s = lax.dot_general(q, k, (((1,), (1,)), ((), ())), preferred_element_type=F32)
    e = jnp.exp2(s - jnp.max(s, axis=-1, keepdims=True))
    l = jnp.sum(e, axis=-1, keepdims=True)
    return jnp.dot(e.astype(BF16), v, preferred_element_type=F32) * (1.0 / l)


def _split_halves(q):
    lo = lax.broadcasted_iota(jnp.int32, q.shape, 1) < 64
    zero = jnp.zeros_like(q)
    return jnp.where(lo, q, zero), jnp.where(lo, zero, q)


def _join_halves(o_lo, o_hi):
    lo = lax.broadcasted_iota(jnp.int32, o_lo.shape, 1) < 64
    return jnp.where(lo, o_lo, o_hi)


def _attn_body(*refs, units, finish, n_extra):
    extra = refs[:n_extra]
    q_ref, k_ref, v_ref, o_ref = refs[n_extra:]
    halves = {}
    outs = []
    for q_lane, half, k_lane, v_lane in units:
        q = q_ref[:, q_lane:q_lane + LANES]
        if half is not None:
            if q_lane not in halves:
                halves[q_lane] = _split_halves(q)
            q = halves[q_lane][half]
        outs.append(_attend(q, k_ref[:, k_lane:k_lane + LANES], v_ref[:, v_lane:v_lane + LANES]))
    finish(outs, o_ref, *extra)


def _finish_pairs(outs, o_ref):
    for p in range(len(outs) // 2):
        o_ref[:, p * LANES:(p + 1) * LANES] = _join_halves(outs[2 * p], outs[2 * p + 1]).astype(o_ref.dtype)


def _finish_tiles(outs, o_ref):
    for p, o in enumerate(outs):
        o_ref[:, p * LANES:(p + 1) * LANES] = o.astype(o_ref.dtype)


def _finish_diff(outs, o_ref, lam_ref, g_ref, *, lambda_init):
    lp = lam_ref[...]
    lam = (jnp.exp(jnp.sum(lp[0:1] * lp[1:2], keepdims=True))
           - jnp.exp(jnp.sum(lp[2:3] * lp[3:4], keepdims=True)) + lambda_init)
    for p in range(len(outs) // 2):
        o = outs[2 * p] - outs[2 * p + 1] * lam
        o_ref[:, p * LANES:(p + 1) * LANES] = (
            _rms_rows(o, g_ref[...]) * (1.0 - lambda_init)).astype(o_ref.dtype)


def _attention(name, units, finish, q, k, v, *, groups, q_width, k_width, v_width, o_width, tq, extra=()):
    b, s, _ = q.shape
    kv_len = k.shape[1]
    tq = min(tq, s)
    in_specs = [_const_spec(e.shape) for e in extra] + [
        pl.BlockSpec((None, tq, q_width), lambda bi, gi, qi: (bi, qi, gi)),
        pl.BlockSpec((None, kv_len, k_width), lambda bi, gi, qi: (bi, 0, gi)),
        pl.BlockSpec((None, kv_len, v_width), lambda bi, gi, qi: (bi, 0, gi)),
    ]
    return pl.pallas_call(
        functools.partial(_attn_body, units=units, finish=finish, n_extra=len(extra)),
        name=name,
        grid=(b, groups, s // tq),
        in_specs=in_specs,
        out_specs=pl.BlockSpec((None, tq, o_width), lambda bi, gi, qi: (bi, qi, gi)),
        out_shape=jax.ShapeDtypeStruct((b, s, groups * o_width), BF16),
        compiler_params=pltpu.CompilerParams(
            dimension_semantics=("parallel", "parallel", "arbitrary"), vmem_limit_bytes=VMEM_LIMIT),
    )(*extra, q, k, v)


def _merge_body(*refs, with_router):
    if with_router:
        (x_ref, oa_ref, ob_ref, oc_ref, od_ref, g_ref, wg_ref, bg_ref, wb_ref, wo_ref, gf_ref, wr_ref,
         xo_ref, h2_ref, route_ref) = refs
    else:
        (x_ref, oa_ref, ob_ref, oc_ref, od_ref, g_ref, wg_ref, bg_ref, wb_ref, wo_ref, gf_ref,
         xo_ref, h2_ref) = refs
    x = x_ref[...]
    h = _rms_rows(x, g_ref[...]).astype(BF16)
    merged = None
    for b, o_ref in enumerate((oa_ref, ob_ref, oc_ref, od_ref)):
        gate = jax.nn.sigmoid(jnp.dot(h, wg_ref[b], preferred_element_type=F32) + bg_ref[b])
        term = gate * jnp.dot(o_ref[...], wb_ref[b], preferred_element_type=F32)
        merged = term if merged is None else merged + term
    x_new = x + jnp.dot(merged.astype(BF16), wo_ref[...], preferred_element_type=F32)
    xo_ref[...] = x_new
    h2 = _rms_rows(x_new, gf_ref[...])
    h2_ref[...] = h2.astype(h2_ref.dtype)
    if with_router:
        h_hi = h2.astype(BF16)
        h_lo = (h2 - h_hi.astype(F32)).astype(BF16)
        logits = (jnp.dot(h_hi, wr_ref[0], preferred_element_type=F32)
                  + (jnp.dot(h_hi, wr_ref[1], preferred_element_type=F32)
                     + jnp.dot(h_lo, wr_ref[0], preferred_element_type=F32)))
        lane = lax.broadcasted_iota(jnp.int32, logits.shape, 1)
        neg = jnp.float32(-jnp.inf)
        l1 = jnp.where(lane < N_EXPERTS, logits, neg)
        m1 = jnp.max(l1, axis=-1, keepdims=True)
        i1 = jnp.min(jnp.where(l1 == m1, lane, LANES), axis=-1, keepdims=True)
        l2 = jnp.where(lane == i1, neg, l1)
        m2 = jnp.max(l2, axis=-1, keepdims=True)
        i2 = jnp.min(jnp.where(l2 == m2, lane, LANES), axis=-1, keepdims=True)
        e2 = jnp.exp(m2 - m1)
        g1 = 1.0 / (1.0 + e2)
        g2 = e2 / (1.0 + e2)
        route = jnp.where(lane == 0, i1.astype(F32),
                          jnp.where(lane == 1, i2.astype(F32),
                                    jnp.where(lane == 2, g1, jnp.where(lane == 3, g2, 0.0))))
        route_ref[...] = route


def _merge(x2, branches, g, wg, bg, wb, wo, gf, w_router=None):
    n = x2.shape[0]
    tm = min(512, n)
    with_router = w_router is not None
    row = lambda w: pl.BlockSpec((tm, w), lambda i: (i, 0))
    in_specs = ([row(D_MODEL)] + [row(512)] * 4
                + [_const_spec((1, D_MODEL)), _const_spec(wg.shape), _const_spec(bg.shape),
                   _const_spec(wb.shape), _const_spec(wo.shape), _const_spec((1, D_MODEL))])
    args = [x2, *branches, g, wg, bg, wb, wo, gf]
    out_specs = [row(D_MODEL), row(D_MODEL)]
    out_shape = [jax.ShapeDtypeStruct((n, D_MODEL), F32),
                 jax.ShapeDtypeStruct((n, D_MODEL), F32 if with_router else BF16)]
    if with_router:
        in_specs.append(_const_spec(w_router.shape))
        args.append(w_router)
        out_specs.append(row(LANES))
        out_shape.append(jax.ShapeDtypeStruct((n, LANES), F32))
    return pl.pallas_call(
        functools.partial(_merge_body, with_router=with_router),
        name="merge_router" if with_router else "merge",
        grid=(n // tm,),
        in_specs=in_specs, out_specs=out_specs, out_shape=out_shape,
        compiler_params=pltpu.CompilerParams(dimension_semantics=("parallel",),
                                             vmem_limit_bytes=VMEM_LIMIT),
    )(*args)


def _swiglu_chunk(xb, w1, w3, w2):
    a = jnp.dot(xb, w1, preferred_element_type=F32)
    b = jnp.dot(xb, w3, preferred_element_type=F32)
    return jnp.dot((jax.nn.silu(a) * b).astype(BF16), w2, preferred_element_type=F32)


def _ffn_body(x_ref, h_ref, w1_ref, w3_ref, w2_ref, o_ref, acc_ref):
    f = pl.program_id(1)
    y = _swiglu_chunk(h_ref[...], w1_ref[...], w3_ref[...], w2_ref[...])

    @pl.when(f == 0)
    def _():
        acc_ref[...] = x_ref[...] + y

    @pl.when(f > 0)
    def _():
        acc_ref[...] += y

    @pl.when(f == pl.num_programs(1) - 1)
    def _():
        o_ref[...] = acc_ref[...]


def _dense_ffn(x2, h2, w1, w3, w2):
    n = x2.shape[0]
    tm = min(1024, n)
    tf = 512
    return pl.pallas_call(
        _ffn_body,
        name="dense_ffn",
        grid=(n // tm, FFN_DIM // tf),
        in_specs=[pl.BlockSpec((tm, D_MODEL), lambda i, f: (i, 0)),
                  pl.BlockSpec((tm, D_MODEL), lambda i, f: (i, 0)),
                  pl.BlockSpec((D_MODEL, tf), lambda i, f: (0, f)),
                  pl.BlockSpec((D_MODEL, tf), lambda i, f: (0, f)),
                  pl.BlockSpec((tf, D_MODEL), lambda i, f: (f, 0))],
        out_specs=pl.BlockSpec((tm, D_MODEL), lambda i, f: (i, 0)),
        out_shape=jax.ShapeDtypeStruct((n, D_MODEL), F32),
        scratch_shapes=[pltpu.VMEM((tm, D_MODEL), F32)],
        compiler_params=pltpu.CompilerParams(dimension_semantics=("parallel", "arbitrary"),
                                             vmem_limit_bytes=VMEM_LIMIT),
    )(x2, h2, w1, w3, w2)


MOE_TF = 896


def _moe_body(expert_ref, src_ref, dst_ref, h_hbm, w1_ref, w3_ref, w2_ref, y_hbm,
              xg_ref, xb_ref, acc_ref, gsem, ssem):
    j = pl.program_id(0)
    f = pl.program_id(1)
    n_f = pl.num_programs(1)
    slot = j % 2
    other = 1 - slot
    rows_per_step = MOE_ROWS // (FFN_DIM // MOE_TF)

    def gather_copy(block, r, buf):
        tok = src_ref[block * MOE_ROWS + r]
        return pltpu.make_async_copy(h_hbm.at[pl.ds(tok, 1), :], xg_ref.at[buf, pl.ds(r, 1), :],
                                     gsem.at[buf])

    def scatter_copy(block, r, buf):
        dst = dst_ref[(block + 1) * MOE_ROWS + r]
        return pltpu.make_async_copy(acc_ref.at[buf, pl.ds(r, 1), :], y_hbm.at[pl.ds(dst, 1), :], ssem)

    def wait_rows(copy):
        for _ in range(MOE_ROWS):
            copy.wait()

    @pl.when(jnp.logical_and(j == 0, f == 0))
    def _():
        def start(r, c):
            gather_copy(0, r, 0).start()
            return c
        lax.fori_loop(0, MOE_ROWS, start, 0)
        acc_ref[1] = jnp.zeros((MOE_ROWS, D_MODEL), F32)

    @pl.when(f == 0)
    def _():
        wait_rows(gather_copy(j, 0, slot))
        xb_ref[...] = xg_ref[slot].astype(BF16)

    for r in range(rows_per_step):
        row = f * rows_per_step + r
        gather_copy(j + 1, row, other).start()
        scatter_copy(j - 1, row, other).start()
    y = _swiglu_chunk(xb_ref[...], w1_ref[...], w3_ref[...], w2_ref[...])

    @pl.when(f == 0)
    def _():
        acc_ref[slot] = y

    @pl.when(f > 0)
    def _():
        acc_ref[slot] += y

    @pl.when(f == n_f - 1)
    def _():
        wait_rows(scatter_copy(j - 1, 0, other))

    @pl.when(jnp.logical_and(j == pl.num_programs(0) - 1, f == n_f - 1))
    def _():
        def start(r, c):
            scatter_copy(j, r, slot).start()
            return c
        lax.fori_loop(0, MOE_ROWS, start, 0)
        wait_rows(scatter_copy(j, 0, slot))
        wait_rows(gather_copy(j + 1, 0, other))


def _moe_ffn(h2, block_expert, slot_src, slot_dst, w1, w3, w2):
    n = h2.shape[0]
    n_blocks = block_expert.shape[0]
    wspec = lambda shape, imap: pl.BlockSpec(shape, imap)
    grid_spec = pltpu.PrefetchScalarGridSpec(
        num_scalar_prefetch=3,
        grid=(n_blocks, FFN_DIM // MOE_TF),
        in_specs=[pl.BlockSpec(memory_space=pl.ANY),
                  wspec((None, D_MODEL, MOE_TF), lambda j, f, e, s, d: (e[j], 0, f)),
                  wspec((None, D_MODEL, MOE_TF), lambda j, f, e, s, d: (e[j], 0, f)),
                  wspec((None, MOE_TF, D_MODEL), lambda j, f, e, s, d: (e[j], f, 0))],
        out_specs=pl.BlockSpec(memory_space=pl.ANY),
        scratch_shapes=[pltpu.VMEM((2, MOE_ROWS, D_MODEL), F32),
                        pltpu.VMEM((MOE_ROWS, D_MODEL), BF16),
                        pltpu.VMEM((2, MOE_ROWS, D_MODEL), F32),
                        pltpu.SemaphoreType.DMA((2,)),
                        pltpu.SemaphoreType.DMA(())],
    )
    return pl.pallas_call(
        _moe_body,
        name="moe_ffn",
        grid_spec=grid_spec,
        out_shape=jax.ShapeDtypeStruct((TOP_K * n + MOE_ROWS, D_MODEL), F32),
        compiler_params=pltpu.CompilerParams(dimension_semantics=("arbitrary", "arbitrary"),
                                             vmem_limit_bytes=VMEM_LIMIT),
    )(block_expert, slot_src, slot_dst, h2, w1, w3, w2)


def _combine_body(x_ref, y1_ref, y2_ref, route_ref, o_ref):
    g1 = route_ref[:, 2:3]
    g2 = route_ref[:, 3:4]
    o_ref[...] = x_ref[...] + (y1_ref[...] * g1 + y2_ref[...] * g2)


def _combine(x2, y, route):
    n = x2.shape[0]
    tm = min(512, n)
    row = lambda w: pl.BlockSpec((tm, w), lambda i: (i, 0))
    return pl.pallas_call(
        _combine_body,
        name="moe_combine",
        grid=(n // tm,),
        in_specs=[row(D_MODEL), row(D_MODEL), pl.BlockSpec((tm, D_MODEL), lambda i: (i + n // tm, 0)),
                  row(LANES)],
        out_specs=row(D_MODEL),
        out_shape=jax.ShapeDtypeStruct((n, D_MODEL), F32),
        compiler_params=pltpu.CompilerParams(dimension_semantics=("parallel",),
                                             vmem_limit_bytes=VMEM_LIMIT),
    )(x2, y, y, route)


def _route_plan(route, n):
    n_assign = n * TOP_K
    flat_e = route[:, :TOP_K].astype(jnp.int32).reshape(-1)
    onehot = (flat_e[:, None] == jnp.arange(N_EXPERTS, dtype=jnp.int32)[None, :]).astype(jnp.int32)
    ranks = jnp.cumsum(onehot, axis=0) - onehot
    counts = jnp.sum(onehot, axis=0)
    rank = jnp.sum(ranks * onehot, axis=1)
    padded = (counts + MOE_ROWS - 1) // MOE_ROWS * MOE_ROWS
    pad_end = jnp.cumsum(padded)
    pad_start = pad_end - padded
    slot = pad_start[flat_e] + rank
    n_blocks = (n_assign + N_EXPERTS * (MOE_ROWS - 1)) // MOE_ROWS
    pad_ids = n_assign + jnp.arange(MOE_ROWS, dtype=jnp.int32)
    slot_assign = jnp.tile(pad_ids, n_blocks).at[slot].set(jnp.arange(n_assign, dtype=jnp.int32))
    valid = slot_assign < n_assign
    slot_src = jnp.where(valid, slot_assign // TOP_K, 0)
    slot_dst = jnp.where(valid, (slot_assign % TOP_K) * n + slot_assign // TOP_K, slot_assign)
    block_start = jnp.arange(n_blocks, dtype=jnp.int32) * MOE_ROWS
    block_expert = jnp.minimum(jnp.searchsorted(pad_end, block_start, side='right'),
                               N_EXPERTS - 1).astype(jnp.int32)
    return (block_expert, jnp.concatenate([slot_src, jnp.zeros((MOE_ROWS,), jnp.int32)]),
            jnp.concatenate([pad_ids, slot_dst]))


def _inv_freq(dim):
    return 1.0 / (ROPE_THETA ** (jnp.arange(0, dim, 2, dtype=F32) / dim))


def _rope_table(pos_of_lane, freq_of_lane, first_half, active=None):
    ang = pos_of_lane * freq_of_lane[None, :]
    cos, sin = jnp.cos(ang), jnp.sin(ang) * jnp.where(first_half, -1.0, 1.0)[None, :]
    if active is not None:
        cos = jnp.where(active[None, :], cos, 1.0)
        sin = jnp.where(active[None, :], sin, 0.0)
    return jnp.concatenate([cos, sin], axis=1).astype(F32)


def _rope_tables(seq, mem_len):
    lane = np.arange(LANES)
    pos = jnp.arange(seq, dtype=jnp.int32)
    posf = jnp.broadcast_to(pos.astype(F32)[:, None], (seq, LANES))
    tab_a = _rope_table(posf, _inv_freq(64)[lane % 32], (lane % 64) < 32)
    tab_b = _rope_table(posf, _inv_freq(32)[lane % 16], (lane % 32) < 16, (lane >= 64) & (lane < 96))
    rows = (pos // MEM_GRID_W).astype(F32)[:, None]
    cols = (pos % MEM_GRID_W).astype(F32)[:, None]
    pos_c = jnp.where(((lane % 64) < 32)[None, :], rows, cols)
    tab_c = _rope_table(pos_c, _inv_freq(32)[lane % 16], (lane % 32) < 16)
    f_d = _inv_freq(128)[lane % 64]
    tab_d = _rope_table(posf, f_d, lane < 64)
    mposf = jnp.broadcast_to(jnp.arange(mem_len, dtype=F32)[:, None], (mem_len, LANES))
    tab_m = _rope_table(mposf, f_d, lane < 64)
    return jnp.concatenate([tab_a, tab_b, tab_c, tab_d], axis=1), tab_m


def _segment_matrix(width):
    i = np.arange(MXU_DIM)
    return jnp.asarray((i[:, None] // width) == (i[None, :] // width), dtype=BF16)


def _tile_row(g, reps):
    return jnp.tile(g.astype(F32), reps).reshape(1, -1)


def _layer_params(layer, w_in, w_mla_q_up, w_mla_kv_up, w_mem_kv, gains):
    w = w_in[layer]
    z = lambda c: jnp.zeros((D_MODEL, c), F32)
    gk, gv = w[:, 2464:2592], w[:, 2592:2720]
    dup = lambda m: jnp.concatenate([m[:, :64], m[:, :64], m[:, 64:], m[:, 64:]], axis=1)
    w_big = jnp.concatenate([
        w[:, 0:1536], w[:, 1536:1792], w[:, 1792:1920],
        z(64), w[:, 1920:1952], z(32),
        w[:, 1952:2464], dup(gk), dup(gv), w[:, 2720:3232]], axis=1)
    wq = w_mla_q_up[layer].reshape(256, MLA_HEADS, MLA_QK_DIM)
    wq = jnp.pad(wq, ((0, 0), (0, 0), (0, LANES - MLA_QK_DIM))).reshape(256, MLA_HEADS * LANES)
    wkv = w_mla_kv_up[layer].reshape(128, MLA_HEADS, 128)
    wk = jnp.pad(wkv[:, :, :64], ((0, 0), (0, 0), (0, 64))).reshape(128, MLA_HEADS * LANES)
    wv = wkv[:, :, 64:].reshape(128, MLA_HEADS * 64)
    wm = w_mem_kv[layer].reshape(D_MODEL, 4, 256)
    w_mem = jnp.concatenate([wm[:, :, :128].reshape(D_MODEL, 512), wm[:, :, 128:].reshape(D_MODEL, 512)], axis=1)
    pad96 = lambda g: jnp.pad(g.astype(F32), (0, LANES - MLA_QK_DIM)).reshape(1, LANES)
    (dq_g, dk_g, ql_g, kvl_g, mq_g, mk_g, gq_g, gk_g, memq_g) = gains
    gain_rows = jnp.concatenate([
        _tile_row(dq_g, 2), _tile_row(dk_g, 2), ql_g.astype(F32).reshape(2, LANES),
        kvl_g.astype(F32).reshape(1, LANES), pad96(mq_g), pad96(mk_g),
        _tile_row(gq_g, 2), _tile_row(gk_g, 2), memq_g.astype(F32).reshape(1, LANES),
        jnp.zeros((6, LANES), F32)], axis=0)
    return (w_big.astype(BF16), wq.astype(BF16), jnp.concatenate([wk, wv], axis=1).astype(BF16),
            w_mem.astype(BF16), gain_rows)


def kernel(x, mem, norm_attn_g, norm_mem_g, w_in, diff_q_norm_g, diff_k_norm_g, diff_lambda, diff_subln_g,
           mla_q_lat_g, mla_kv_lat_g, w_mla_q_up, w_mla_kv_up, mla_q_norm_g, mla_k_norm_g, gqa_q_norm_g,
           gqa_k_norm_g, w_mem_kv, mem_q_norm_g, mem_k_norm_g, w_branch, w_gate, b_gate, w_out, norm_ffn_g,
           dense_w1, dense_w3, dense_w2, moe_router, moe_w1, moe_w3, moe_w2):
    batch, seq, _ = x.shape
    mem_len = mem.shape[1]
    n = batch * seq
    depth = w_in.shape[0]
    tab, tab_m = _rope_tables(seq, mem_len)
    seg64, seg128 = _segment_matrix(64), _segment_matrix(128)
    x2 = x.reshape(n, D_MODEL)
    mem2 = mem.reshape(batch * mem_len, D_MODEL)
    row = lambda g: g.astype(F32).reshape(1, -1)
    b3 = lambda a, w: a.reshape(batch, -1, w)

    for layer in range(depth):
        lambda_init = 0.8 - 0.6 * math.exp(-0.3 * layer)
        w_big, wq_up, wkv_up, w_mem, gain_rows = _layer_params(
            layer, w_in, w_mla_q_up, w_mla_kv_up, w_mem_kv,
            (diff_q_norm_g[layer], diff_k_norm_g[layer], mla_q_lat_g[layer], mla_kv_lat_g[layer],
             mla_q_norm_g[layer], mla_k_norm_g[layer], gqa_q_norm_g[layer], gqa_k_norm_g[layer],
             mem_q_norm_g[layer]))
        qa, ka, va, qb, kb, vb, qc, kc, vc, qd = _inproj(
            x2, row(norm_attn_g[layer]), w_big, wq_up, wkv_up, seg64, seg128, gain_rows, tab, seq)
        kd, vd = _memprep(mem2, row(norm_mem_g[layer]), w_mem, seg128, row(mem_k_norm_g[layer]), tab_m, mem_len)

        o_a = _attention("attn_diff", [(0, 0, 0, 0), (0, 1, 0, 0), (128, 0, 128, 128), (128, 1, 128, 128)],
                         functools.partial(_finish_diff, lambda_init=lambda_init),
                         b3(qa, 512), b3(ka, 512), b3(va, 512), groups=2, q_width=256,
                         k_width=256, v_width=256, o_width=256, tq=512,
                         extra=(diff_lambda[layer].astype(F32), row(diff_subln_g[layer])))
        o_b = _attention("attn_latent", [(128 * u, None, 128 * u, 128 * (u // 2)) for u in range(4)],
                         _finish_pairs, b3(qb, 1024), b3(kb, 1024), b3(vb, 512), groups=2,
                         q_width=512, k_width=512, v_width=256, o_width=256, tq=512)
        o_c = _attention("attn_gqa", [(0, 0, 0, 0), (0, 1, 0, 0), (128, 0, 0, 0), (128, 1, 0, 0)],
                         _finish_pairs, b3(qc, 512), b3(kc, 256), b3(vc, 256), groups=2,
                         q_width=256, k_width=128, v_width=128, o_width=256, tq=512)
        o_d = _attention("attn_mem", [(128 * u, None, 128 * u, 128 * u) for u in range(4)],
                         _finish_tiles, b3(qd, 512), b3(kd, 512), b3(vd, 512), groups=1,
                         q_width=512, k_width=512, v_width=512, o_width=512, tq=512)
        branches = [o.reshape(n, 512) for o in (o_a, o_b, o_c, o_d)]

        i = layer // 2
        routed = layer % 2 == 1
        w_router = None
        if routed:
            w_r = jnp.pad(moe_router[i].astype(F32), ((0, 0), (0, LANES - N_EXPERTS)))
            w_r_hi = w_r.astype(BF16)
            w_router = jnp.stack([w_r_hi, (w_r - w_r_hi.astype(F32)).astype(BF16)])
        merged = _merge(x2, branches, row(norm_attn_g[layer]), w_gate[layer].astype(BF16),
                        b_gate[layer].astype(F32).reshape(4, 1, D_MODEL), w_branch[layer].astype(BF16),
                        w_out[layer].astype(BF16), row(norm_ffn_g[layer]), w_router)
        if not routed:
            x_mid, h2 = merged
            x2 = _dense_ffn(x_mid, h2, dense_w1[i].astype(BF16), dense_w3[i].astype(BF16),
                            dense_w2[i].astype(BF16))
        else:
            x_mid, h2, route = merged
            block_expert, slot_src, slot_dst = _route_plan(route, n)
            y = _moe_ffn(h2, block_expert, slot_src, slot_dst, moe_w1[i].astype(BF16),
                         moe_w3[i].astype(BF16), moe_w2[i].astype(BF16))
            x2 = _combine(x_mid, y, route)
    return x2.reshape(batch, seq, D_MODEL)
```

```python
import functools
import math

import numpy as np
import jax
import jax.numpy as jnp
from jax import lax
from jax.experimental import pallas as pl
from jax.experimental.pallas import tpu as pltpu

F32 = jnp.float32
BF16 = jnp.bfloat16

D_MODEL = 1024
MEM_GRID_W = 64
ROPE_THETA = 10000.0
NORM_EPS = 1e-6
LOG2_E = math.log2(math.e)
DIFF_HEADS = 4
MLA_HEADS = 8
MLA_QK_DIM = 96
FFN_DIM = 3584
N_EXPERTS = 8
TOP_K = 2
MOE_ROWS = 512
LANES = 128
MXU_DIM = 256
VMEM_LIMIT = 56 * 1024 * 1024

C_DQ, C_DK, C_DV = 0, 512, 1024
C_MQL, C_MKVL, C_KROPE = 1536, 1792, 1920
C_GQ, C_GK, C_GV, C_MEMQ = 2048, 2560, 2816, 3072
IN_COLS_PADDED = 3584
T_A, T_B, T_C, T_D = 0, 256, 512, 768


def _const_spec(shape):
    return pl.BlockSpec(shape, lambda *_: (0,) * len(shape), pipeline_mode=pl.Buffered(1))


def _rms_rows(x, g):
    ms = jnp.mean(x * x, axis=-1, keepdims=True)
    return x * lax.rsqrt(ms + NORM_EPS) * g


def _segment_mean_sq(y, seg_ref, inv_count):
    sq = (y * y).astype(BF16)
    parts = [jnp.dot(sq[:, c:c + MXU_DIM], seg_ref[...], preferred_element_type=F32)
             for c in range(0, y.shape[1], MXU_DIM)]
    ss = parts[0] if len(parts) == 1 else jnp.concatenate(parts, axis=1)
    return ss * inv_count


def _rope_lanes(y, cos, sin_signed, half):
    lane = lax.broadcasted_iota(jnp.int32, y.shape, 1)
    first = (lane % (2 * half)) < half
    rot = jnp.where(first, pltpu.roll(y, LANES - half, 1), pltpu.roll(y, half, 1))
    return y * cos + rot * sin_signed


def _norm_rope_store(y, ms, gain, tab_ref, t_off, half, scale, out_ref):
    yn = y * lax.rsqrt(ms + NORM_EPS)
    cos = tab_ref[:, t_off:t_off + LANES]
    sin = tab_ref[:, t_off + LANES:t_off + 2 * LANES]
    for c in range(0, y.shape[1], LANES):
        r = _rope_lanes(yn[:, c:c + LANES] * gain, cos, sin, half)
        if scale != 1.0:
            r = r * scale
        out_ref[:, c:c + LANES] = r.astype(out_ref.dtype)


def _inproj_body(x_ref, g_ref, w_ref, wq_ref, wkv_ref, seg64_ref, seg128_ref, gains_ref, tab_ref,
                 qa_ref, ka_ref, va_ref, qb_ref, kb_ref, vb_ref, qc_ref, kc_ref, vc_ref, qd_ref):
    h = _rms_rows(x_ref[...], g_ref[...]).astype(BF16)

    def proj(c0, width):
        return jnp.dot(h, w_ref[:, c0:c0 + width], preferred_element_type=F32)

    def gain(row):
        return gains_ref[row:row + 1, :]

    y = proj(C_DQ, 512)
    _norm_rope_store(y, _segment_mean_sq(y, seg64_ref, 1.0 / 64), gain(0), tab_ref, T_A, 32,
                     LOG2_E * 64 ** -0.5, qa_ref)
    y = proj(C_DK, 512)
    _norm_rope_store(y, _segment_mean_sq(y, seg64_ref, 1.0 / 64), gain(1), tab_ref, T_A, 32, 1.0, ka_ref)
    va_ref[...] = proj(C_DV, 512).astype(BF16)

    g_ql = jnp.concatenate([gain(2), gain(3)], axis=1)
    ql = _rms_rows(proj(C_MQL, 256), g_ql).astype(BF16)
    y = jnp.dot(ql, wq_ref[...], preferred_element_type=F32)
    _norm_rope_store(y, _segment_mean_sq(y, seg128_ref, 1.0 / MLA_QK_DIM), gain(5), tab_ref, T_B, 16,
                     LOG2_E * MLA_QK_DIM ** -0.5, qb_ref)
    kvl = _rms_rows(proj(C_MKVL, 128), gain(4)).astype(BF16)
    kv = jnp.dot(kvl, wkv_ref[...], preferred_element_type=F32)
    k_rope = proj(C_KROPE, 128)
    y = kv[:, :1024] + jnp.concatenate([k_rope] * MLA_HEADS, axis=1)
    _norm_rope_store(y, _segment_mean_sq(y, seg128_ref, 1.0 / MLA_QK_DIM), gain(6), tab_ref, T_B, 16,
                     1.0, kb_ref)
    vb_ref[...] = kv[:, 1024:].astype(BF16)

    y = proj(C_GQ, 512)
    _norm_rope_store(y, _segment_mean_sq(y, seg64_ref, 1.0 / 64), gain(7), tab_ref, T_C, 16,
                     LOG2_E * 64 ** -0.5, qc_ref)
    y = proj(C_GK, 256)
    _norm_rope_store(y, _segment_mean_sq(y, seg64_ref, 1.0 / 64), gain(8), tab_ref, T_C, 16, 1.0, kc_ref)
    vc_ref[...] = proj(C_GV, 256).astype(BF16)

    y = proj(C_MEMQ, 512)
    _norm_rope_store(y, _segment_mean_sq(y, seg128_ref, 1.0 / 128), gain(9), tab_ref, T_D, 64,
                     LOG2_E * 128 ** -0.5, qd_ref)


def _inproj(x2, g, w_big, wq_up, wkv_up, seg64, seg128, gains, tab, seq):
    n = x2.shape[0]
    tm = min(512, seq)
    tiles_per_seq = seq // tm
    row = lambda w: pl.BlockSpec((tm, w), lambda i: (i, 0))
    out_widths = (512, 512, 512, 1024, 1024, 512, 512, 256, 256, 512)
    return pl.pallas_call(
        _inproj_body,
        name="inproj",
        grid=(n // tm,),
        in_specs=[row(D_MODEL), _const_spec((1, D_MODEL)), _const_spec(w_big.shape),
                  _const_spec(wq_up.shape), _const_spec(wkv_up.shape), _const_spec(seg64.shape),
                  _const_spec(seg128.shape), _const_spec(gains.shape),
                  pl.BlockSpec((tm, tab.shape[1]), lambda i: (i % tiles_per_seq, 0))],
        out_specs=[row(w) for w in out_widths],
        out_shape=[jax.ShapeDtypeStruct((n, w), BF16) for w in out_widths],
        compiler_params=pltpu.CompilerParams(dimension_semantics=("parallel",),
                                             vmem_limit_bytes=VMEM_LIMIT),
    )(x2, g, w_big, wq_up, wkv_up, seg64, seg128, gains, tab)


def _memprep_body(m_ref, g_ref, w_ref, seg128_ref, gain_ref, tab_ref, k_ref, v_ref):
    mn = _rms_rows(m_ref[...], g_ref[...]).astype(BF16)
    kv = jnp.dot(mn, w_ref[...], preferred_element_type=F32)
    y = kv[:, :512]
    _norm_rope_store(y, _segment_mean_sq(y, seg128_ref, 1.0 / 128), gain_ref[...], tab_ref, 0, 64, 1.0, k_ref)
    v_ref[...] = kv[:, 512:].astype(BF16)


def _memprep(mem2, g, w_kv, seg128, gain, tab_m, mem_len):
    n = mem2.shape[0]
    row = lambda w: pl.BlockSpec((mem_len, w), lambda i: (i, 0))
    return pl.pallas_call(
        _memprep_body,
        name="memprep",
        grid=(n // mem_len,),
        in_specs=[row(D_MODEL), _const_spec((1, D_MODEL)), _const_spec(w_kv.shape),
                  _const_spec(seg128.shape), _const_spec((1, LANES)), _const_spec(tab_m.shape)],
        out_specs=[row(512), row(512)],
        out_shape=[jax.ShapeDtypeStruct((n, 512), BF16)] * 2,
        compiler_params=pltpu.CompilerParams(dimension_semantics=("parallel",),
                                             vmem_limit_bytes=VMEM_LIMIT),
    )(mem2, g, w_kv, seg128, gain, tab_m)


def _attend(q, k, v):
    s = lax.dot_general(q, k, (((1,), (1,)), ((), ())), preferred_element_type=F32)
    e = jnp.exp2(s - jnp.max(s, axis=-1, keepdims=True))
    l = jnp.sum(e, axis=-1, keepdims=True)
    return jnp.dot(e.astype(BF16), v, preferred_element_type=F32) * (1.0 / l)


def _split_halves(q):
    lo = lax.broadcasted_iota(jnp.int32, q.shape, 1) < 64
    zero = jnp.zeros_like(q)
    return jnp.where(lo, q, zero), jnp.where(lo, zero, q)


def _join_halves(o_lo, o_hi):
    lo = lax.broadcasted_iota(jnp.int32, o_lo.shape, 1) < 64
    return jnp.where(lo, o_lo, o_hi)


def _attn_body(*refs, units, finish, n_extra):
    extra = refs[:n_extra]
    q_ref, k_ref, v_ref, o_ref = refs[n_extra:]
    halves = {}
    outs = []
    for q_lane, half, k_lane, v_lane in units:
        q = q_ref[:, q_lane:q_lane + LANES]
        if half is not None:
            if q_lane not in halves:
                halves[q_lane] = _split_halves(q)
            q = halves[q_lane][half]
        outs.append(_attend(q, k_ref[:, k_lane:k_lane + LANES], v_ref[:, v_lane:v_lane + LANES]))
    finish(outs, o_ref, *extra)


def _finish_pairs(outs, o_ref):
    for p in range(len(outs) // 2):
        o_ref[:, p * LANES:(p + 1) * LANES] = _join_halves(outs[2 * p], outs[2 * p + 1]).astype(o_ref.dtype)


def _finish_tiles(outs, o_ref):
    for p, o in enumerate(outs):
        o_ref[:, p * LANES:(p + 1) * LANES] = o.astype(o_ref.dtype)


def _finish_diff(outs, o_ref, lam_ref, g_ref, *, lambda_init):
    lp = lam_ref[...]
    lam = (jnp.exp(jnp.sum(lp[0:1] * lp[1:2], keepdims=True))
           - jnp.exp(jnp.sum(lp[2:3] * lp[3:4], keepdims=True)) + lambda_init)
    for p in range(len(outs) // 2):
        o = outs[2 * p] - outs[2 * p + 1] * lam
        o_ref[:, p * LANES:(p + 1) * LANES] = (
            _rms_rows(o, g_ref[...]) * (1.0 - lambda_init)).astype(o_ref.dtype)


def _attention(name, units, finish, q, k, v, *, groups, q_width, k_width, v_width, o_width, tq, extra=()):
    b, s, _ = q.shape
    kv_len = k.shape[1]
    tq = min(tq, s)
    in_specs = [_const_spec(e.shape) for e in extra] + [
        pl.BlockSpec((None, tq, q_width), lambda bi, gi, qi: (bi, qi, gi)),
        pl.BlockSpec((None, kv_len, k_width), lambda bi, gi, qi: (bi, 0, gi)),
        pl.BlockSpec((None, kv_len, v_width), lambda bi, gi, qi: (bi, 0, gi)),
    ]
    return pl.pallas_call(
        functools.partial(_attn_body, units=units, finish=finish, n_extra=len(extra)),
        name=name,
        grid=(b, groups, s // tq),
        in_specs=in_specs,
        out_specs=pl.BlockSpec((None, tq, o_width), lambda bi, gi, qi: (bi, qi, gi)),
        out_shape=jax.ShapeDtypeStruct((b, s, groups * o_width), BF16),
        compiler_params=pltpu.CompilerParams(
            dimension_semantics=("parallel", "parallel", "arbitrary"), vmem_limit_bytes=VMEM_LIMIT),
    )(*extra, q, k, v)


def _route_rows(h2, wr_ref, route_ref):
    h_hi = h2.astype(BF16)
    h_lo = (h2 - h_hi.astype(F32)).astype(BF16)
    logits = (jnp.dot(h_hi, wr_ref[0], preferred_element_type=F32)
              + (jnp.dot(h_hi, wr_ref[1], preferred_element_type=F32)
                 + jnp.dot(h_lo, wr_ref[0], preferred_element_type=F32)))
    l1 = logits.T[:N_EXPERTS, :]
    row = lax.broadcasted_iota(jnp.int32, l1.shape, 0)
    neg = jnp.float32(-jnp.inf)
    m1 = jnp.max(l1, axis=0, keepdims=True)
    i1 = jnp.min(jnp.where(l1 == m1, row, N_EXPERTS), axis=0, keepdims=True)
    l2 = jnp.where(row == i1, neg, l1)
    m2 = jnp.max(l2, axis=0, keepdims=True)
    i2 = jnp.min(jnp.where(l2 == m2, row, N_EXPERTS), axis=0, keepdims=True)
    e2 = jnp.exp(m2 - m1)
    g1 = 1.0 / (1.0 + e2)
    g2 = e2 / (1.0 + e2)
    route_t = jnp.where(row == 0, i1.astype(F32),
                        jnp.where(row == 1, i2.astype(F32),
                                  jnp.where(row == 2, g1, jnp.where(row == 3, g2, 0.0))))
    pad = jnp.zeros((LANES - N_EXPERTS, route_t.shape[1]), F32)
    route_ref[...] = jnp.concatenate([route_t, pad], axis=0).T


def _merge_body(*refs, with_router):
    if with_router:
        (x_ref, oa_ref, ob_ref, oc_ref, od_ref, g_ref, wg_ref, bg_ref, wb_ref, wo_ref, gf_ref, wr_ref,
         xo_ref, h2_ref, route_ref, hprev_ref) = refs

        @pl.when(pl.program_id(0) == 0)
        def _():
            hprev_ref[...] = jnp.zeros(hprev_ref.shape, F32)

        _route_rows(hprev_ref[...], wr_ref, route_ref)
    else:
        (x_ref, oa_ref, ob_ref, oc_ref, od_ref, g_ref, wg_ref, bg_ref, wb_ref, wo_ref, gf_ref,
         xo_ref, h2_ref) = refs
    x = x_ref[...]
    h = _rms_rows(x, g_ref[...]).astype(BF16)
    merged = None
    for b, o_ref in enumerate((oa_ref, ob_ref, oc_ref, od_ref)):
        gate = jax.nn.sigmoid(jnp.dot(h, wg_ref[b], preferred_element_type=F32) + bg_ref[b])
        term = gate * jnp.dot(o_ref[...], wb_ref[b], preferred_element_type=F32)
        merged = term if merged is None else merged + term
    x_new = x + jnp.dot(merged.astype(BF16), wo_ref[...], preferred_element_type=F32)
    xo_ref[...] = x_new
    h2 = _rms_rows(x_new, gf_ref[...])
    h2_ref[...] = h2.astype(h2_ref.dtype)
    if with_router:
        hprev_ref[...] = h2


def _merge(x2, branches, g, wg, bg, wb, wo, gf, w_router=None):
    n = x2.shape[0]
    tm = min(512, n)
    n_tiles = n // tm
    with_router = w_router is not None
    row = lambda w: pl.BlockSpec((tm, w), lambda i: (jnp.minimum(i, n_tiles - 1), 0))
    in_specs = ([row(D_MODEL)] + [row(512)] * 4
                + [_const_spec((1, D_MODEL)), _const_spec(wg.shape), _const_spec(bg.shape),
                   _const_spec(wb.shape), _const_spec(wo.shape), _const_spec((1, D_MODEL))])
    args = [x2, *branches, g, wg, bg, wb, wo, gf]
    out_specs = [row(D_MODEL), row(D_MODEL)]
    out_shape = [jax.ShapeDtypeStruct((n, D_MODEL), F32),
                 jax.ShapeDtypeStruct((n, D_MODEL), F32 if with_router else BF16)]
    scratch_shapes = []
    if with_router:
        in_specs.append(_const_spec(w_router.shape))
        args.append(w_router)
        out_specs.append(pl.BlockSpec((tm, LANES), lambda i: (jnp.maximum(i - 1, 0), 0)))
        out_shape.append(jax.ShapeDtypeStruct((n, LANES), F32))
        scratch_shapes.append(pltpu.VMEM((tm, D_MODEL), F32))
    return pl.pallas_call(
        functools.partial(_merge_body, with_router=with_router),
        name="merge_router" if with_router else "merge",
        grid=(n_tiles + 1 if with_router else n_tiles,),
        in_specs=in_specs, out_specs=out_specs, out_shape=out_shape, scratch_shapes=scratch_shapes,
        compiler_params=pltpu.CompilerParams(
            dimension_semantics=("arbitrary",) if with_router else ("parallel",),
            vmem_limit_bytes=VMEM_LIMIT),
    )(*args)


def _swiglu_chunk(xb, w1, w3, w2):
    a = jnp.dot(xb, w1, preferred_element_type=F32)
    b = jnp.dot(xb, w3, preferred_element_type=F32)
    return jnp.dot((jax.nn.silu(a) * b).astype(BF16), w2, preferred_element_type=F32)


def _ffn_body(x_ref, h_ref, w1_ref, w3_ref, w2_ref, o_ref, acc_ref):
    f = pl.program_id(1)
    y = _swiglu_chunk(h_ref[...], w1_ref[...], w3_ref[...], w2_ref[...])

    @pl.when(f == 0)
    def _():
        acc_ref[...] = x_ref[...] + y

    @pl.when(f > 0)
    def _():
        acc_ref[...] += y

    @pl.when(f == pl.num_programs(1) - 1)
    def _():
        o_ref[...] = acc_ref[...]


def _dense_ffn(x2, h2, w1, w3, w2):
    n = x2.shape[0]
    tm = min(1024, n)
    tf = 896
    return pl.pallas_call(
        _ffn_body,
        name="dense_ffn",
        grid=(n // tm, FFN_DIM // tf),
        in_specs=[pl.BlockSpec((tm, D_MODEL), lambda i, f: (i, 0)),
                  pl.BlockSpec((tm, D_MODEL), lambda i, f: (i, 0)),
                  pl.BlockSpec((D_MODEL, tf), lambda i, f: (0, f)),
                  pl.BlockSpec((D_MODEL, tf), lambda i, f: (0, f)),
                  pl.BlockSpec((tf, D_MODEL), lambda i, f: (f, 0))],
        out_specs=pl.BlockSpec((tm, D_MODEL), lambda i, f: (i, 0)),
        out_shape=jax.ShapeDtypeStruct((n, D_MODEL), F32),
        scratch_shapes=[pltpu.VMEM((tm, D_MODEL), F32)],
        compiler_params=pltpu.CompilerParams(dimension_semantics=("parallel", "arbitrary"),
                                             vmem_limit_bytes=VMEM_LIMIT),
    )(x2, h2, w1, w3, w2)


MOE_TF = 896


def _moe_body(expert_ref, src_ref, dst_ref, h_hbm, w1_ref, w3_ref, w2_ref, y_hbm,
              xg_ref, xb_ref, acc_ref, gsem, ssem):
    j = pl.program_id(0)
    f = pl.program_id(1)
    n_f = pl.num_programs(1)
    slot = j % 2
    other = 1 - slot
    rows_per_step = MOE_ROWS // (FFN_DIM // MOE_TF)

    def gather_copy(block, r, buf):
        tok = src_ref[block * MOE_ROWS + r]
        return pltpu.make_async_copy(h_hbm.at[pl.ds(tok, 1), :], xg_ref.at[buf, pl.ds(r, 1), :],
                                     gsem.at[buf])

    def scatter_copy(block, r, buf):
        dst = dst_ref[(block + 1) * MOE_ROWS + r]
        return pltpu.make_async_copy(acc_ref.at[buf, pl.ds(r, 1), :], y_hbm.at[pl.ds(dst, 1), :], ssem)

    def wait_rows(copy):
        for _ in range(MOE_ROWS):
            copy.wait()

    @pl.when(jnp.logical_and(j == 0, f == 0))
    def _():
        def start(r, c):
            gather_copy(0, r, 0).start()
            return c
        lax.fori_loop(0, MOE_ROWS, start, 0)
        acc_ref[1] = jnp.zeros((MOE_ROWS, D_MODEL), F32)

    @pl.when(f == 0)
    def _():
        wait_rows(gather_copy(j, 0, slot))
        xb_ref[...] = xg_ref[slot].astype(BF16)

    for step in range(FFN_DIM // MOE_TF):
        @pl.when(f == step)
        def _():
            for r in range(step * rows_per_step, (step + 1) * rows_per_step):
                gather_copy(j + 1, r, other).start()
                scatter_copy(j - 1, r, other).start()

    @pl.when(f == 0)
    def _():
        acc_ref[slot] = jnp.zeros((MOE_ROWS, D_MODEL), F32)

    acc_ref[slot] += _swiglu_chunk(xb_ref[...], w1_ref[...], w3_ref[...], w2_ref[...])

    @pl.when(f == n_f - 1)
    def _():
        wait_rows(scatter_copy(j - 1, 0, other))

    @pl.when(jnp.logical_and(j == pl.num_programs(0) - 1, f == n_f - 1))
    def _():
        def start(r, c):
            scatter_copy(j, r, slot).start()
            return c
        lax.fori_loop(0, MOE_ROWS, start, 0)
        wait_rows(scatter_copy(j, 0, slot))
        wait_rows(gather_copy(j + 1, 0, other))


def _moe_ffn(h2, block_expert, slot_src, slot_dst, w1, w3, w2):
    n = h2.shape[0]
    n_blocks = block_expert.shape[0]
    wspec = lambda shape, imap: pl.BlockSpec(shape, imap)
    grid_spec = pltpu.PrefetchScalarGridSpec(
        num_scalar_prefetch=3,
        grid=(n_blocks, FFN_DIM // MOE_TF),
        in_specs=[pl.BlockSpec(memory_space=pl.ANY),
                  wspec((None, D_MODEL, MOE_TF), lambda j, f, e, s, d: (e[j], 0, f)),
                  wspec((None, D_MODEL, MOE_TF), lambda j, f, e, s, d: (e[j], 0, f)),
                  wspec((None, MOE_TF, D_MODEL), lambda j, f, e, s, d: (e[j], f, 0))],
        out_specs=pl.BlockSpec(memory_space=pl.ANY),
        scratch_shapes=[pltpu.VMEM((2, MOE_ROWS, D_MODEL), F32),
                        pltpu.VMEM((MOE_ROWS, D_MODEL), BF16),
                        pltpu.VMEM((2, MOE_ROWS, D_MODEL), F32),
                        pltpu.SemaphoreType.DMA((2,)),
                        pltpu.SemaphoreType.DMA(())],
    )
    return pl.pallas_call(
        _moe_body,
        name="moe_ffn",
        grid_spec=grid_spec,
        out_shape=jax.ShapeDtypeStruct((TOP_K * n + MOE_ROWS, D_MODEL), F32),
        compiler_params=pltpu.CompilerParams(dimension_semantics=("arbitrary", "arbitrary"),
                                             vmem_limit_bytes=VMEM_LIMIT),
    )(block_expert, slot_src, slot_dst, h2, w1, w3, w2)


def _combine_body(x_ref, y1_ref, y2_ref, route_ref, o_ref):
    g1 = route_ref[:, 2:3]
    g2 = route_ref[:, 3:4]
    o_ref[...] = x_ref[...] + (y1_ref[...] * g1 + y2_ref[...] * g2)


def _combine(x2, y, route):
    n = x2.shape[0]
    tm = min(512, n)
    row = lambda w: pl.BlockSpec((tm, w), lambda i: (i, 0))
    return pl.pallas_call(
        _combine_body,
        name="moe_combine",
        grid=(n // tm,),
        in_specs=[row(D_MODEL), row(D_MODEL), pl.BlockSpec((tm, D_MODEL), lambda i: (i + n // tm, 0)),
                  row(LANES)],
        out_specs=row(D_MODEL),
        out_shape=jax.ShapeDtypeStruct((n, D_MODEL), F32),
        compiler_params=pltpu.CompilerParams(dimension_semantics=("parallel",),
                                             vmem_limit_bytes=VMEM_LIMIT),
    )(x2, y, y, route)


def _route_plan(route, n):
    n_assign = n * TOP_K
    flat_e = route[:, :TOP_K].astype(jnp.int32).reshape(-1)
    onehot = (flat_e[:, None] == jnp.arange(N_EXPERTS, dtype=jnp.int32)[None, :]).astype(jnp.int32)
    ranks = jnp.cumsum(onehot, axis=0) - onehot
    counts = jnp.sum(onehot, axis=0)
    rank = jnp.sum(ranks * onehot, axis=1)
    padded = (counts + MOE_ROWS - 1) // MOE_ROWS * MOE_ROWS
    pad_end = jnp.cumsum(padded)
    pad_start = pad_end - padded
    slot = pad_start[flat_e] + rank
    n_blocks = (n_assign + N_EXPERTS * (MOE_ROWS - 1)) // MOE_ROWS
    pad_ids = n_assign + jnp.arange(MOE_ROWS, dtype=jnp.int32)
    slot_assign = jnp.tile(pad_ids, n_blocks).at[slot].set(jnp.arange(n_assign, dtype=jnp.int32))
    valid = slot_assign < n_assign
    slot_src = jnp.where(valid, slot_assign // TOP_K, 0)
    slot_dst = jnp.where(valid, (slot_assign % TOP_K) * n + slot_assign // TOP_K, slot_assign)
    block_start = jnp.arange(n_blocks, dtype=jnp.int32) * MOE_ROWS
    block_expert = jnp.minimum(jnp.searchsorted(pad_end, block_start, side='right'),
                               N_EXPERTS - 1).astype(jnp.int32)
    return (block_expert, jnp.concatenate([slot_src, jnp.zeros((MOE_ROWS,), jnp.int32)]),
            jnp.concatenate([pad_ids, slot_dst]))


def _inv_freq(dim):
    return 1.0 / (ROPE_THETA ** (jnp.arange(0, dim, 2, dtype=F32) / dim))


def _rope_table(pos_of_lane, freq_of_lane, first_half, active=None):
    ang = pos_of_lane * freq_of_lane[None, :]
    cos, sin = jnp.cos(ang), jnp.sin(ang) * jnp.where(first_half, -1.0, 1.0)[None, :]
    if active is not None:
        cos = jnp.where(active[None, :], cos, 1.0)
        sin = jnp.where(active[None, :], sin, 0.0)
    return jnp.concatenate([cos, sin], axis=1).astype(F32)


def _rope_tables(seq, mem_len):
    lane = np.arange(LANES)
    pos = jnp.arange(seq, dtype=jnp.int32)
    posf = jnp.broadcast_to(pos.astype(F32)[:, None], (seq, LANES))
    tab_a = _rope_table(posf, _inv_freq(64)[lane % 32], (lane % 64) < 32)
    tab_b = _rope_table(posf, _inv_freq(32)[lane % 16], (lane % 32) < 16, (lane >= 64) & (lane < 96))
    rows = (pos // MEM_GRID_W).astype(F32)[:, None]
    cols = (pos % MEM_GRID_W).astype(F32)[:, None]
    pos_c = jnp.where(((lane % 64) < 32)[None, :], rows, cols)
    tab_c = _rope_table(pos_c, _inv_freq(32)[lane % 16], (lane % 32) < 16)
    f_d = _inv_freq(128)[lane % 64]
    tab_d = _rope_table(posf, f_d, lane < 64)
    mposf = jnp.broadcast_to(jnp.arange(mem_len, dtype=F32)[:, None], (mem_len, LANES))
    tab_m = _rope_table(mposf, f_d, lane < 64)
    return jnp.concatenate([tab_a, tab_b, tab_c, tab_d], axis=1), tab_m


def _segment_matrix(width):
    i = np.arange(MXU_DIM)
    return jnp.asarray((i[:, None] // width) == (i[None, :] // width), dtype=BF16)


def _tile_row(g, reps):
    return jnp.tile(g.astype(F32), reps).reshape(1, -1)


def _layer_params(layer, w_in, w_mla_q_up, w_mla_kv_up, w_mem_kv, gains):
    w = w_in[layer]
    z = lambda c: jnp.zeros((D_MODEL, c), F32)
    gk, gv = w[:, 2464:2592], w[:, 2592:2720]
    dup = lambda m: jnp.concatenate([m[:, :64], m[:, :64], m[:, 64:], m[:, 64:]], axis=1)
    w_big = jnp.concatenate([
        w[:, 0:1536], w[:, 1536:1792], w[:, 1792:1920],
        z(64), w[:, 1920:1952], z(32),
        w[:, 1952:2464], dup(gk), dup(gv), w[:, 2720:3232]], axis=1)
    wq = w_mla_q_up[layer].reshape(256, MLA_HEADS, MLA_QK_DIM)
    wq = jnp.pad(wq, ((0, 0), (0, 0), (0, LANES - MLA_QK_DIM))).reshape(256, MLA_HEADS * LANES)
    wkv = w_mla_kv_up[layer].reshape(128, MLA_HEADS, 128)
    wk = jnp.pad(wkv[:, :, :64], ((0, 0), (0, 0), (0, 64))).reshape(128, MLA_HEADS * LANES)
    wv = wkv[:, :, 64:].reshape(128, MLA_HEADS * 64)
    wm = w_mem_kv[layer].reshape(D_MODEL, 4, 256)
    w_mem = jnp.concatenate([wm[:, :, :128].reshape(D_MODEL, 512), wm[:, :, 128:].reshape(D_MODEL, 512)], axis=1)
    pad96 = lambda g: jnp.pad(g.astype(F32), (0, LANES - MLA_QK_DIM)).reshape(1, LANES)
    (dq_g, dk_g, ql_g, kvl_g, mq_g, mk_g, gq_g, gk_g, memq_g) = gains
    gain_rows = jnp.concatenate([
        _tile_row(dq_g, 2), _tile_row(dk_g, 2), ql_g.astype(F32).reshape(2, LANES),
        kvl_g.astype(F32).reshape(1, LANES), pad96(mq_g), pad96(mk_g),
        _tile_row(gq_g, 2), _tile_row(gk_g, 2), memq_g.astype(F32).reshape(1, LANES),
        jnp.zeros((6, LANES), F32)], axis=0)
    return (w_big.astype(BF16), wq.astype(BF16), jnp.concatenate([wk, wv], axis=1).astype(BF16),
            w_mem.astype(BF16), gain_rows)


def kernel(x, mem, norm_attn_g, norm_mem_g, w_in, diff_q_norm_g, diff_k_norm_g, diff_lambda, diff_subln_g,
           mla_q_lat_g, mla_kv_lat_g, w_mla_q_up, w_mla_kv_up, mla_q_norm_g, mla_k_norm_g, gqa_q_norm_g,
           gqa_k_norm_g, w_mem_kv, mem_q_norm_g, mem_k_norm_g, w_branch, w_gate, b_gate, w_out, norm_ffn_g,
           dense_w1, dense_w3, dense_w2, moe_router, moe_w1, moe_w3, moe_w2):
    batch, seq, _ = x.shape
    mem_len = mem.shape[1]
    n = batch * seq
    depth = w_in.shape[0]
    tab, tab_m = _rope_tables(seq, mem_len)
    seg64, seg128 = _segment_matrix(64), _segment_matrix(128)
    x2 = x.reshape(n, D_MODEL)
    mem2 = mem.reshape(batch * mem_len, D_MODEL)
    row = lambda g: g.astype(F32).reshape(1, -1)
    b3 = lambda a, w: a.reshape(batch, -1, w)

    for layer in range(depth):
        lambda_init = 0.8 - 0.6 * math.exp(-0.3 * layer)
        w_big, wq_up, wkv_up, w_mem, gain_rows = _layer_params(
            layer, w_in, w_mla_q_up, w_mla_kv_up, w_mem_kv,
            (diff_q_norm_g[layer], diff_k_norm_g[layer], mla_q_lat_g[layer], mla_kv_lat_g[layer],
             mla_q_norm_g[layer], mla_k_norm_g[layer], gqa_q_norm_g[layer], gqa_k_norm_g[layer],
             mem_q_norm_g[layer]))
        qa, ka, va, qb, kb, vb, qc, kc, vc, qd = _inproj(
            x2, row(norm_attn_g[layer]), w_big, wq_up, wkv_up, seg64, seg128, gain_rows, tab, seq)
        kd, vd = _memprep(mem2, row(norm_mem_g[layer]), w_mem, seg128, row(mem_k_norm_g[layer]), tab_m, mem_len)

        o_a = _attention("attn_diff", [(0, 0, 0, 0), (0, 1, 0, 0), (128, 0, 128, 128), (128, 1, 128, 128)],
                         functools.partial(_finish_diff, lambda_init=lambda_init),
                         b3(qa, 512), b3(ka, 512), b3(va, 512), groups=2, q_width=256,
                         k_width=256, v_width=256, o_width=256, tq=512,
                         extra=(diff_lambda[layer].astype(F32), row(diff_subln_g[layer])))
        o_b = _attention("attn_latent", [(128 * u, None, 128 * u, 128 * (u // 2)) for u in range(4)],
                         _finish_pairs, b3(qb, 1024), b3(kb, 1024), b3(vb, 512), groups=2,
                         q_width=512, k_width=512, v_width=256, o_width=256, tq=512)
        o_c = _attention("attn_gqa", [(0, 0, 0, 0), (0, 1, 0, 0), (128, 0, 0, 0), (128, 1, 0, 0)],
                         _finish_pairs, b3(qc, 512), b3(kc, 256), b3(vc, 256), groups=2,
                         q_width=256, k_width=128, v_width=128, o_width=256, tq=512)
        o_d = _attention("attn_mem", [(128 * u, None, 128 * u, 128 * u) for u in range(4)],
                         _finish_tiles, b3(qd, 512), b3(kd, 512), b3(vd, 512), groups=1,
                         q_width=512, k_width=512, v_width=512, o_width=512, tq=512)
        branches = [o.reshape(n, 512) for o in (o_a, o_b, o_c, o_d)]

        i = layer // 2
        routed = layer % 2 == 1
        w_router = None
        if routed:
            w_r = jnp.pad(moe_router[i].astype(F32), ((0, 0), (0, LANES - N_EXPERTS)))
            w_r_hi = w_r.astype(BF16)
            w_router = jnp.stack([w_r_hi, (w_r - w_r_hi.astype(F32)).astype(BF16)])
        merged = _merge(x2, branches, row(norm_attn_g[layer]), w_gate[layer].astype(BF16),
                        b_gate[layer].astype(F32).reshape(4, 1, D_MODEL), w_branch[layer].astype(BF16),
                        w_out[layer].astype(BF16), row(norm_ffn_g[layer]), w_router)
        if not routed:
            x_mid, h2 = merged
            x2 = _dense_ffn(x_mid, h2, dense_w1[i].astype(BF16), dense_w3[i].astype(BF16),
                            dense_w2[i].astype(BF16))
        else:
            x_mid, h2, route = merged
            block_expert, slot_src, slot_dst = _route_plan(route, n)
            y = _moe_ffn(h2, block_expert, slot_src, slot_dst, moe_w1[i].astype(BF16),
                         moe_w3[i].astype(BF16), moe_w2[i].astype(BF16))
            x2 = _combine(x_mid, y, route)
    return x2.reshape(batch, seq, D_MODEL)
```

```python
import functools
import math

import numpy as np
import jax
import jax.numpy as jnp
from jax import lax
from jax.experimental import pallas as pl
from jax.experimental.pallas import tpu as pltpu

F32 = jnp.float32
BF16 = jnp.bfloat16

D_MODEL = 1024
MEM_GRID_W = 64
ROPE_THETA = 10000.0
NORM_EPS = 1e-6
LOG2_E = math.log2(math.e)
DIFF_HEADS = 4
MLA_HEADS = 8
MLA_QK_DIM = 96
FFN_DIM = 3584
N_EXPERTS = 8
TOP_K = 2
MOE_ROWS = 512
LANES = 128
MXU_DIM = 256
VMEM_LIMIT = 56 * 1024 * 1024

C_DQ, C_DK, C_DV = 0, 512, 1024
C_MQL, C_MKVL, C_KROPE = 1536, 1792, 1920
C_GQ, C_GK, C_GV, C_MEMQ = 2048, 2560, 2816, 3072
IN_COLS_PADDED = 3584
T_A, T_B, T_C, T_D = 0, 256, 512, 768


def _const_spec(shape):
    return pl.BlockSpec(shape, lambda *_: (0,) * len(shape), pipeline_mode=pl.Buffered(1))


def _rms_rows(x, g):
    ms = jnp.mean(x * x, axis=-1, keepdims=True)
    return x * lax.rsqrt(ms + NORM_EPS) * g


def _segment_mean_sq(y, seg_ref, inv_count):
    sq = (y * y).astype(BF16)
    parts = [jnp.dot(sq[:, c:c + MXU_DIM], seg_ref[...], preferred_element_type=F32)
             for c in range(0, y.shape[1], MXU_DIM)]
    ss = parts[0] if len(parts) == 1 else jnp.concatenate(parts, axis=1)
    return ss * inv_count


def _rope_lanes(y, cos, sin_signed, half):
    lane = lax.broadcasted_iota(jnp.int32, y.shape, 1)
    first = (lane % (2 * half)) < half
    rot = jnp.where(first, pltpu.roll(y, LANES - half, 1), pltpu.roll(y, half, 1))
    return y * cos + rot * sin_signed


def _norm_rope_store(y, ms, gain, tab_ref, t_off, half, scale, out_ref):
    yn = y * lax.rsqrt(ms + NORM_EPS)
    cos = tab_ref[:, t_off:t_off + LANES]
    sin = tab_ref[:, t_off + LANES:t_off + 2 * LANES]
    for c in range(0, y.shape[1], LANES):
        r = _rope_lanes(yn[:, c:c + LANES] * gain, cos, sin, half)
        if scale != 1.0:
            r = r * scale
        out_ref[:, c:c + LANES] = r.astype(out_ref.dtype)


def _inproj_body(x_ref, g_ref, w_ref, wq_ref, wkv_ref, seg64_ref, seg128_ref, gains_ref, tab_ref,
                 qa_ref, ka_ref, va_ref, qb_ref, kb_ref, vb_ref, qc_ref, kc_ref, vc_ref, qd_ref):
    h = _rms_rows(x_ref[...], g_ref[...]).astype(BF16)

    def proj(c0, width):
        return jnp.dot(h, w_ref[:, c0:c0 + width], preferred_element_type=F32)

    def gain(row):
        return gains_ref[row:row + 1, :]

    y = proj(C_DQ, 512)
    _norm_rope_store(y, _segment_mean_sq(y, seg64_ref, 1.0 / 64), gain(0), tab_ref, T_A, 32,
                     LOG2_E * 64 ** -0.5, qa_ref)
    y = proj(C_DK, 512)
    _norm_rope_store(y, _segment_mean_sq(y, seg64_ref, 1.0 / 64), gain(1), tab_ref, T_A, 32, 1.0, ka_ref)
    va_ref[...] = proj(C_DV, 512).astype(BF16)

    g_ql = jnp.concatenate([gain(2), gain(3)], axis=1)
    ql = _rms_rows(proj(C_MQL, 256), g_ql).astype(BF16)
    y = jnp.dot(ql, wq_ref[...], preferred_element_type=F32)
    _norm_rope_store(y, _segment_mean_sq(y, seg128_ref, 1.0 / MLA_QK_DIM), gain(5), tab_ref, T_B, 16,
                     LOG2_E * MLA_QK_DIM ** -0.5, qb_ref)
    kvl = _rms_rows(proj(C_MKVL, 128), gain(4)).astype(BF16)
    kv = jnp.dot(kvl, wkv_ref[...], preferred_element_type=F32)
    k_rope = proj(C_KROPE, 128)
    y = kv[:, :1024] + jnp.concatenate([k_rope] * MLA_HEADS, axis=1)
    _norm_rope_store(y, _segment_mean_sq(y, seg128_ref, 1.0 / MLA_QK_DIM), gain(6), tab_ref, T_B, 16,
                     1.0, kb_ref)
    vb_ref[...] = kv[:, 1024:].astype(BF16)

    y = proj(C_GQ, 512)
    _norm_rope_store(y, _segment_mean_sq(y, seg64_ref, 1.0 / 64), gain(7), tab_ref, T_C, 16,
                     LOG2_E * 64 ** -0.5, qc_ref)
    y = proj(C_GK, 256)
    _norm_rope_store(y, _segment_mean_sq(y, seg64_ref, 1.0 / 64), gain(8), tab_ref, T_C, 16, 1.0, kc_ref)
    vc_ref[...] = proj(C_GV, 256).astype(BF16)

    y = proj(C_MEMQ, 512)
    _norm_rope_store(y, _segment_mean_sq(y, seg128_ref, 1.0 / 128), gain(9), tab_ref, T_D, 64,
                     LOG2_E * 128 ** -0.5, qd_ref)


def _inproj(x2, g, w_big, wq_up, wkv_up, seg64, seg128, gains, tab, seq):
    n = x2.shape[0]
    tm = min(512, seq)
    tiles_per_seq = seq // tm
    row = lambda w: pl.BlockSpec((tm, w), lambda i: (i, 0))
    out_widths = (512, 512, 512, 1024, 1024, 512, 512, 256, 256, 512)
    return pl.pallas_call(
        _inproj_body,
        name="inproj",
        grid=(n // tm,),
        in_specs=[row(D_MODEL), _const_spec((1, D_MODEL)), _const_spec(w_big.shape),
                  _const_spec(wq_up.shape), _const_spec(wkv_up.shape), _const_spec(seg64.shape),
                  _const_spec(seg128.shape), _const_spec(gains.shape),
                  pl.BlockSpec((tm, tab.shape[1]), lambda i: (i % tiles_per_seq, 0))],
        out_specs=[row(w) for w in out_widths],
        out_shape=[jax.ShapeDtypeStruct((n, w), BF16) for w in out_widths],
        compiler_params=pltpu.CompilerParams(dimension_semantics=("parallel",),
                                             vmem_limit_bytes=VMEM_LIMIT),
    )(x2, g, w_big, wq_up, wkv_up, seg64, seg128, gains, tab)


def _memprep_body(m_ref, g_ref, w_ref, seg128_ref, gain_ref, tab_ref, k_ref, v_ref):
    mn = _rms_rows(m_ref[...], g_ref[...]).astype(BF16)
    kv = jnp.dot(mn, w_ref[...], preferred_element_type=F32)
    y = kv[:, :512]
    _norm_rope_store(y, _segment_mean_sq(y, seg128_ref, 1.0 / 128), gain_ref[...], tab_ref, 0, 64, 1.0, k_ref)
    v_ref[...] = kv[:, 512:].astype(BF16)


def _memprep(mem2, g, w_kv, seg128, gain, tab_m, mem_len):
    n = mem2.shape[0]
    row = lambda w: pl.BlockSpec((mem_len, w), lambda i: (i, 0))
    return pl.pallas_call(
        _memprep_body,
        name="memprep",
        grid=(n // mem_len,),
        in_specs=[row(D_MODEL), _const_spec((1, D_MODEL)), _const_spec(w_kv.shape),
                  _const_spec(seg128.shape), _const_spec((1, LANES)), _const_spec(tab_m.shape)],
        out_specs=[row(512), row(512)],
        out_shape=[jax.ShapeDtypeStruct((n, 512), BF16)] * 2,
        compiler_params=pltpu.CompilerParams(dimension_semantics=("parallel",),
                                             vmem_limit_bytes=VMEM_LIMIT),
    )(mem2, g, w_kv, seg128, gain, tab_m)


def _attend(q, k, v, sums_on_mxu):
    s = lax.dot_general(q, k, (((1,), (1,)), ((), ())), preferred_element_type=F32)
    e = jnp.exp2(s - jnp.max(s, axis=-1, keepdims=True))
    if not sums_on_mxu:
        l = jnp.sum(e, axis=-1, keepdims=True)
        return jnp.dot(e.astype(BF16), v, preferred_element_type=F32) * (1.0 / l)
    v_ones = jnp.concatenate([v, jnp.ones((v.shape[0], LANES), BF16)], axis=1)
    ov = jnp.dot(e.astype(BF16), v_ones, preferred_element_type=F32)
    return ov[:, :LANES] * (1.0 / ov[:, LANES:LANES + 1])


def _split_halves(q):
    lo = lax.broadcasted_iota(jnp.int32, q.shape, 1) < 64
    zero = jnp.zeros_like(q)
    return jnp.where(lo, q, zero), jnp.where(lo, zero, q)


def _join_halves(o_lo, o_hi):
    lo = lax.broadcasted_iota(jnp.int32, o_lo.shape, 1) < 64
    return jnp.where(lo, o_lo, o_hi)


def _attn_body(*refs, units, finish, n_extra, sums_on_mxu):
    extra = refs[:n_extra]
    q_ref, k_ref, v_ref, o_ref = refs[n_extra:]
    halves = {}
    outs = []
    for q_lane, half, k_lane, v_lane in units:
        q = q_ref[:, q_lane:q_lane + LANES]
        if half is not None:
            if q_lane not in halves:
                halves[q_lane] = _split_halves(q)
            q = halves[q_lane][half]
        outs.append(_attend(q, k_ref[:, k_lane:k_lane + LANES], v_ref[:, v_lane:v_lane + LANES],
                            sums_on_mxu))
    finish(outs, o_ref, *extra)


def _finish_pairs(outs, o_ref):
    for p in range(len(outs) // 2):
        o_ref[:, p * LANES:(p + 1) * LANES] = _join_halves(outs[2 * p], outs[2 * p + 1]).astype(o_ref.dtype)


def _finish_tiles(outs, o_ref):
    for p, o in enumerate(outs):
        o_ref[:, p * LANES:(p + 1) * LANES] = o.astype(o_ref.dtype)


def _finish_diff(outs, o_ref, lam_ref, g_ref, *, lambda_init):
    lp = lam_ref[...]
    lam = (jnp.exp(jnp.sum(lp[0:1] * lp[1:2], keepdims=True))
           - jnp.exp(jnp.sum(lp[2:3] * lp[3:4], keepdims=True)) + lambda_init)
    for p in range(len(outs) // 2):
        o = outs[2 * p] - outs[2 * p + 1] * lam
        o_ref[:, p * LANES:(p + 1) * LANES] = (
            _rms_rows(o, g_ref[...]) * (1.0 - lambda_init)).astype(o_ref.dtype)


def _attention(name, units, finish, q, k, v, *, groups, q_width, k_width, v_width, o_width, tq,
               sums_on_mxu=False, extra=()):
    b, s, _ = q.shape
    kv_len = k.shape[1]
    tq = min(tq, s)
    in_specs = [_const_spec(e.shape) for e in extra] + [
        pl.BlockSpec((None, tq, q_width), lambda bi, gi, qi: (bi, qi, gi)),
        pl.BlockSpec((None, kv_len, k_width), lambda bi, gi, qi: (bi, 0, gi)),
        pl.BlockSpec((None, kv_len, v_width), lambda bi, gi, qi: (bi, 0, gi)),
    ]
    return pl.pallas_call(
        functools.partial(_attn_body, units=units, finish=finish, n_extra=len(extra),
                          sums_on_mxu=sums_on_mxu),
        name=name,
        grid=(b, groups, s // tq),
        in_specs=in_specs,
        out_specs=pl.BlockSpec((None, tq, o_width), lambda bi, gi, qi: (bi, qi, gi)),
        out_shape=jax.ShapeDtypeStruct((b, s, groups * o_width), BF16),
        compiler_params=pltpu.CompilerParams(
            dimension_semantics=("parallel", "parallel", "arbitrary"), vmem_limit_bytes=VMEM_LIMIT),
    )(*extra, q, k, v)


def _route_rows(h2, wr_ref, route_ref):
    h_hi = h2.astype(BF16)
    h_lo = (h2 - h_hi.astype(F32)).astype(BF16)
    logits = (jnp.dot(h_hi, wr_ref[0], preferred_element_type=F32)
              + (jnp.dot(h_hi, wr_ref[1], preferred_element_type=F32)
                 + jnp.dot(h_lo, wr_ref[0], preferred_element_type=F32)))
    l1 = logits.T[:N_EXPERTS, :]
    row = lax.broadcasted_iota(jnp.int32, l1.shape, 0)
    neg = jnp.float32(-jnp.inf)
    m1 = jnp.max(l1, axis=0, keepdims=True)
    i1 = jnp.min(jnp.where(l1 == m1, row, N_EXPERTS), axis=0, keepdims=True)
    l2 = jnp.where(row == i1, neg, l1)
    m2 = jnp.max(l2, axis=0, keepdims=True)
    i2 = jnp.min(jnp.where(l2 == m2, row, N_EXPERTS), axis=0, keepdims=True)
    e2 = jnp.exp(m2 - m1)
    g1 = 1.0 / (1.0 + e2)
    g2 = e2 / (1.0 + e2)
    route_t = jnp.where(row == 0, i1.astype(F32),
                        jnp.where(row == 1, i2.astype(F32),
                                  jnp.where(row == 2, g1, jnp.where(row == 3, g2, 0.0))))
    pad = jnp.zeros((LANES - N_EXPERTS, route_t.shape[1]), F32)
    route_ref[...] = jnp.concatenate([route_t, pad], axis=0).T


def _merge_body(*refs, with_router):
    if with_router:
        (x_ref, oa_ref, ob_ref, oc_ref, od_ref, g_ref, wg_ref, bg_ref, wb_ref, wo_ref, gf_ref, wr_ref,
         xo_ref, h2_ref, route_ref, hprev_ref) = refs

        @pl.when(pl.program_id(0) == 0)
        def _():
            hprev_ref[...] = jnp.zeros(hprev_ref.shape, F32)

        _route_rows(hprev_ref[...], wr_ref, route_ref)
    else:
        (x_ref, oa_ref, ob_ref, oc_ref, od_ref, g_ref, wg_ref, bg_ref, wb_ref, wo_ref, gf_ref,
         xo_ref, h2_ref) = refs
    x = x_ref[...]
    h = _rms_rows(x, g_ref[...]).astype(BF16)
    merged = None
    for b, o_ref in enumerate((oa_ref, ob_ref, oc_ref, od_ref)):
        gate = jax.nn.sigmoid(jnp.dot(h, wg_ref[b], preferred_element_type=F32) + bg_ref[b])
        term = gate * jnp.dot(o_ref[...], wb_ref[b], preferred_element_type=F32)
        merged = term if merged is None else merged + term
    x_new = x + jnp.dot(merged.astype(BF16), wo_ref[...], preferred_element_type=F32)
    xo_ref[...] = x_new
    h2 = _rms_rows(x_new, gf_ref[...])
    h2_ref[...] = h2.astype(h2_ref.dtype)
    if with_router:
        hprev_ref[...] = h2


def _merge(x2, branches, g, wg, bg, wb, wo, gf, w_router=None):
    n = x2.shape[0]
    tm = min(512, n)
    n_tiles = n // tm
    with_router = w_router is not None
    row = lambda w: pl.BlockSpec((tm, w), lambda i: (jnp.minimum(i, n_tiles - 1), 0))
    in_specs = ([row(D_MODEL)] + [row(512)] * 4
                + [_const_spec((1, D_MODEL)), _const_spec(wg.shape), _const_spec(bg.shape),
                   _const_spec(wb.shape), _const_spec(wo.shape), _const_spec((1, D_MODEL))])
    args = [x2, *branches, g, wg, bg, wb, wo, gf]
    out_specs = [row(D_MODEL), row(D_MODEL)]
    out_shape = [jax.ShapeDtypeStruct((n, D_MODEL), F32),
                 jax.ShapeDtypeStruct((n, D_MODEL), F32 if with_router else BF16)]
    scratch_shapes = []
    if with_router:
        in_specs.append(_const_spec(w_router.shape))
        args.append(w_router)
        out_specs.append(pl.BlockSpec((tm, LANES), lambda i: (jnp.maximum(i - 1, 0), 0)))
        out_shape.append(jax.ShapeDtypeStruct((n, LANES), F32))
        scratch_shapes.append(pltpu.VMEM((tm, D_MODEL), F32))
    return pl.pallas_call(
        functools.partial(_merge_body, with_router=with_router),
        name="merge_router" if with_router else "merge",
        grid=(n_tiles + 1 if with_router else n_tiles,),
        in_specs=in_specs, out_specs=out_specs, out_shape=out_shape, scratch_shapes=scratch_shapes,
        compiler_params=pltpu.CompilerParams(
            dimension_semantics=("arbitrary",) if with_router else ("parallel",),
            vmem_limit_bytes=VMEM_LIMIT),
    )(*args)


def _swiglu_chunk(xb, w1, w3, w2):
    a = jnp.dot(xb, w1, preferred_element_type=F32)
    b = jnp.dot(xb, w3, preferred_element_type=F32)
    return jnp.dot((jax.nn.silu(a) * b).astype(BF16), w2, preferred_element_type=F32)


def _ffn_body(x_ref, h_ref, w1_ref, w3_ref, w2_ref, o_ref, acc_ref):
    f = pl.program_id(1)
    y = _swiglu_chunk(h_ref[...], w1_ref[...], w3_ref[...], w2_ref[...])

    @pl.when(f == 0)
    def _():
        acc_ref[...] = x_ref[...] + y

    @pl.when(f > 0)
    def _():
        acc_ref[...] += y

    @pl.when(f == pl.num_programs(1) - 1)
    def _():
        o_ref[...] = acc_ref[...]


def _dense_ffn(x2, h2, w1, w3, w2):
    n = x2.shape[0]
    tm = min(1024, n)
    tf = 512
    return pl.pallas_call(
        _ffn_body,
        name="dense_ffn",
        grid=(n // tm, FFN_DIM // tf),
        in_specs=[pl.BlockSpec((tm, D_MODEL), lambda i, f: (i, 0)),
                  pl.BlockSpec((tm, D_MODEL), lambda i, f: (i, 0)),
                  pl.BlockSpec((D_MODEL, tf), lambda i, f: (0, f)),
                  pl.BlockSpec((D_MODEL, tf), lambda i, f: (0, f)),
                  pl.BlockSpec((tf, D_MODEL), lambda i, f: (f, 0))],
        out_specs=pl.BlockSpec((tm, D_MODEL), lambda i, f: (i, 0)),
        out_shape=jax.ShapeDtypeStruct((n, D_MODEL), F32),
        scratch_shapes=[pltpu.VMEM((tm, D_MODEL), F32)],
        compiler_params=pltpu.CompilerParams(dimension_semantics=("parallel", "arbitrary"),
                                             vmem_limit_bytes=VMEM_LIMIT),
    )(x2, h2, w1, w3, w2)


MOE_TF = 1792


def _moe_body(expert_ref, src_ref, dst_ref, h_hbm, w1_ref, w3_ref, w2_ref, y_hbm,
              xg_ref, xb_ref, acc_ref, gsem, ssem):
    j = pl.program_id(0)
    f = pl.program_id(1)
    n_f = pl.num_programs(1)
    slot = j % 2
    other = 1 - slot
    rows_per_step = MOE_ROWS // (FFN_DIM // MOE_TF)

    def gather_copy(block, r, buf):
        tok = src_ref[block * MOE_ROWS + r]
        return pltpu.make_async_copy(h_hbm.at[pl.ds(tok, 1), :], xg_ref.at[buf, pl.ds(r, 1), :],
                                     gsem.at[buf])

    def scatter_copy(block, r, buf):
        dst = dst_ref[(block + 1) * MOE_ROWS + r]
        return pltpu.make_async_copy(acc_ref.at[buf, pl.ds(r, 1), :], y_hbm.at[pl.ds(dst, 1), :], ssem)

    def wait_rows(copy):
        for _ in range(MOE_ROWS):
            copy.wait()

    @pl.when(jnp.logical_and(j == 0, f == 0))
    def _():
        def start(r, c):
            gather_copy(0, r, 0).start()
            return c
        lax.fori_loop(0, MOE_ROWS, start, 0)
        acc_ref[1] = jnp.zeros((MOE_ROWS, D_MODEL), F32)

    @pl.when(f == 0)
    def _():
        wait_rows(gather_copy(j, 0, slot))
        xb_ref[...] = xg_ref[slot].astype(BF16)

    for step in range(FFN_DIM // MOE_TF):
        @pl.when(f == step)
        def _():
            for r in range(step * rows_per_step, (step + 1) * rows_per_step):
                gather_copy(j + 1, r, other).start()
                scatter_copy(j - 1, r, other).start()

    @pl.when(f == 0)
    def _():
        acc_ref[slot] = jnp.zeros((MOE_ROWS, D_MODEL), F32)

    acc_ref[slot] += _swiglu_chunk(xb_ref[...], w1_ref[...], w3_ref[...], w2_ref[...])

    @pl.when(f == n_f - 1)
    def _():
        wait_rows(scatter_copy(j - 1, 0, other))

    @pl.when(jnp.logical_and(j == pl.num_programs(0) - 1, f == n_f - 1))
    def _():
        def start(r, c):
            scatter_copy(j, r, slot).start()
            return c
        lax.fori_loop(0, MOE_ROWS, start, 0)
        wait_rows(scatter_copy(j, 0, slot))
        wait_rows(gather_copy(j + 1, 0, other))


def _moe_ffn(h2, block_expert, slot_src, slot_dst, w1, w3, w2):
    n = h2.shape[0]
    n_blocks = block_expert.shape[0]
    wspec = lambda shape, imap: pl.BlockSpec(shape, imap)
    grid_spec = pltpu.PrefetchScalarGridSpec(
        num_scalar_prefetch=3,
        grid=(n_blocks, FFN_DIM // MOE_TF),
        in_specs=[pl.BlockSpec(memory_space=pl.ANY),
                  wspec((None, D_MODEL, MOE_TF), lambda j, f, e, s, d: (e[j], 0, f)),
                  wspec((None, D_MODEL, MOE_TF), lambda j, f, e, s, d: (e[j], 0, f)),
                  wspec((None, MOE_TF, D_MODEL), lambda j, f, e, s, d: (e[j], f, 0))],
        out_specs=pl.BlockSpec(memory_space=pl.ANY),
        scratch_shapes=[pltpu.VMEM((2, MOE_ROWS, D_MODEL), F32),
                        pltpu.VMEM((MOE_ROWS, D_MODEL), BF16),
                        pltpu.VMEM((2, MOE_ROWS, D_MODEL), F32),
                        pltpu.SemaphoreType.DMA((2,)),
                        pltpu.SemaphoreType.DMA(())],
    )
    return pl.pallas_call(
        _moe_body,
        name="moe_ffn",
        grid_spec=grid_spec,
        out_shape=jax.ShapeDtypeStruct((TOP_K * n + MOE_ROWS, D_MODEL), F32),
        compiler_params=pltpu.CompilerParams(dimension_semantics=("arbitrary", "arbitrary"),
                                             vmem_limit_bytes=VMEM_LIMIT),
    )(block_expert, slot_src, slot_dst, h2, w1, w3, w2)


def _combine_body(x_ref, y1_ref, y2_ref, route_ref, o_ref):
    g1 = route_ref[:, 2:3]
    g2 = route_ref[:, 3:4]
    o_ref[...] = x_ref[...] + (y1_ref[...] * g1 + y2_ref[...] * g2)


def _combine(x2, y, route):
    n = x2.shape[0]
    tm = min(512, n)
    row = lambda w: pl.BlockSpec((tm, w), lambda i: (i, 0))
    return pl.pallas_call(
        _combine_body,
        name="moe_combine",
        grid=(n // tm,),
        in_specs=[row(D_MODEL), row(D_MODEL), pl.BlockSpec((tm, D_MODEL), lambda i: (i + n // tm, 0)),
                  row(LANES)],
        out_specs=row(D_MODEL),
        out_shape=jax.ShapeDtypeStruct((n, D_MODEL), F32),
        compiler_params=pltpu.CompilerParams(dimension_semantics=("parallel",),
                                             vmem_limit_bytes=VMEM_LIMIT),
    )(x2, y, y, route)


def _route_plan(route, n):
    n_assign = n * TOP_K
    flat_e = route[:, :TOP_K].astype(jnp.int32).reshape(-1)
    onehot = (flat_e[:, None] == jnp.arange(N_EXPERTS, dtype=jnp.int32)[None, :]).astype(jnp.int32)
    ranks = jnp.cumsum(onehot, axis=0) - onehot
    counts = jnp.sum(onehot, axis=0)
    rank = jnp.sum(ranks * onehot, axis=1)
    padded = (counts + MOE_ROWS - 1) // MOE_ROWS * MOE_ROWS
    pad_end = jnp.cumsum(padded)
    pad_start = pad_end - padded
    slot = pad_start[flat_e] + rank
    n_blocks = (n_assign + N_EXPERTS * (MOE_ROWS - 1)) // MOE_ROWS
    pad_ids = n_assign + jnp.arange(MOE_ROWS, dtype=jnp.int32)
    slot_assign = jnp.tile(pad_ids, n_blocks).at[slot].set(jnp.arange(n_assign, dtype=jnp.int32))
    valid = slot_assign < n_assign
    slot_src = jnp.where(valid, slot_assign // TOP_K, 0)
    slot_dst = jnp.where(valid, (slot_assign % TOP_K) * n + slot_assign // TOP_K, slot_assign)
    block_start = jnp.arange(n_blocks, dtype=jnp.int32) * MOE_ROWS
    block_expert = jnp.minimum(jnp.searchsorted(pad_end, block_start, side='right'),
                               N_EXPERTS - 1).astype(jnp.int32)
    return (block_expert, jnp.concatenate([slot_src, jnp.zeros((MOE_ROWS,), jnp.int32)]),
            jnp.concatenate([pad_ids, slot_dst]))


def _inv_freq(dim):
    return 1.0 / (ROPE_THETA ** (jnp.arange(0, dim, 2, dtype=F32) / dim))


def _rope_table(pos_of_lane, freq_of_lane, first_half, active=None):
    ang = pos_of_lane * freq_of_lane[None, :]
    cos, sin = jnp.cos(ang), jnp.sin(ang) * jnp.where(first_half, -1.0, 1.0)[None, :]
    if active is not None:
        cos = jnp.where(active[None, :], cos, 1.0)
        sin = jnp.where(active[None, :], sin, 0.0)
    return jnp.concatenate([cos, sin], axis=1).astype(F32)


def _rope_tables(seq, mem_len):
    lane = np.arange(LANES)
    pos = jnp.arange(seq, dtype=jnp.int32)
    posf = jnp.broadcast_to(pos.astype(F32)[:, None], (seq, LANES))
    tab_a = _rope_table(posf, _inv_freq(64)[lane % 32], (lane % 64) < 32)
    tab_b = _rope_table(posf, _inv_freq(32)[lane % 16], (lane % 32) < 16, (lane >= 64) & (lane < 96))
    rows = (pos // MEM_GRID_W).astype(F32)[:, None]
    cols = (pos % MEM_GRID_W).astype(F32)[:, None]
    pos_c = jnp.where(((lane % 64) < 32)[None, :], rows, cols)
    tab_c = _rope_table(pos_c, _inv_freq(32)[lane % 16], (lane % 32) < 16)
    f_d = _inv_freq(128)[lane % 64]
    tab_d = _rope_table(posf, f_d, lane < 64)
    mposf = jnp.broadcast_to(jnp.arange(mem_len, dtype=F32)[:, None], (mem_len, LANES))
    tab_m = _rope_table(mposf, f_d, lane < 64)
    return jnp.concatenate([tab_a, tab_b, tab_c, tab_d], axis=1), tab_m


def _segment_matrix(width):
    i = np.arange(MXU_DIM)
    return jnp.asarray((i[:, None] // width) == (i[None, :] // width), dtype=BF16)


def _tile_row(g, reps):
    return jnp.tile(g.astype(F32), reps).reshape(1, -1)


def _layer_params(layer, w_in, w_mla_q_up, w_mla_kv_up, w_mem_kv, gains):
    w = w_in[layer]
    z = lambda c: jnp.zeros((D_MODEL, c), F32)
    gk, gv = w[:, 2464:2592], w[:, 2592:2720]
    dup = lambda m: jnp.concatenate([m[:, :64], m[:, :64], m[:, 64:], m[:, 64:]], axis=1)
    w_big = jnp.concatenate([
        w[:, 0:1536], w[:, 1536:1792], w[:, 1792:1920],
        z(64), w[:, 1920:1952], z(32),
        w[:, 1952:2464], dup(gk), dup(gv), w[:, 2720:3232]], axis=1)
    wq = w_mla_q_up[layer].reshape(256, MLA_HEADS, MLA_QK_DIM)
    wq = jnp.pad(wq, ((0, 0), (0, 0), (0, LANES - MLA_QK_DIM))).reshape(256, MLA_HEADS * LANES)
    wkv = w_mla_kv_up[layer].reshape(128, MLA_HEADS, 128)
    wk = jnp.pad(wkv[:, :, :64], ((0, 0), (0, 0), (0, 64))).reshape(128, MLA_HEADS * LANES)
    wv = wkv[:, :, 64:].reshape(128, MLA_HEADS * 64)
    wm = w_mem_kv[layer].reshape(D_MODEL, 4, 256)
    w_mem = jnp.concatenate([wm[:, :, :128].reshape(D_MODEL, 512), wm[:, :, 128:].reshape(D_MODEL, 512)], axis=1)
    pad96 = lambda g: jnp.pad(g.astype(F32), (0, LANES - MLA_QK_DIM)).reshape(1, LANES)
    (dq_g, dk_g, ql_g, kvl_g, mq_g, mk_g, gq_g, gk_g, memq_g) = gains
    gain_rows = jnp.concatenate([
        _tile_row(dq_g, 2), _tile_row(dk_g, 2), ql_g.astype(F32).reshape(2, LANES),
        kvl_g.astype(F32).reshape(1, LANES), pad96(mq_g), pad96(mk_g),
        _tile_row(gq_g, 2), _tile_row(gk_g, 2), memq_g.astype(F32).reshape(1, LANES),
        jnp.zeros((6, LANES), F32)], axis=0)
    return (w_big.astype(BF16), wq.astype(BF16), jnp.concatenate([wk, wv], axis=1).astype(BF16),
            w_mem.astype(BF16), gain_rows)


def kernel(x, mem, norm_attn_g, norm_mem_g, w_in, diff_q_norm_g, diff_k_norm_g, diff_lambda, diff_subln_g,
           mla_q_lat_g, mla_kv_lat_g, w_mla_q_up, w_mla_kv_up, mla_q_norm_g, mla_k_norm_g, gqa_q_norm_g,
           gqa_k_norm_g, w_mem_kv, mem_q_norm_g, mem_k_norm_g, w_branch, w_gate, b_gate, w_out, norm_ffn_g,
           dense_w1, dense_w3, dense_w2, moe_router, moe_w1, moe_w3, moe_w2):
    batch, seq, _ = x.shape
    mem_len = mem.shape[1]
    n = batch * seq
    depth = w_in.shape[0]
    tab, tab_m = _rope_tables(seq, mem_len)
    seg64, seg128 = _segment_matrix(64), _segment_matrix(128)
    x2 = x.reshape(n, D_MODEL)
    mem2 = mem.reshape(batch * mem_len, D_MODEL)
    row = lambda g: g.astype(F32).reshape(1, -1)
    b3 = lambda a, w: a.reshape(batch, -1, w)

    for layer in range(depth):
        lambda_init = 0.8 - 0.6 * math.exp(-0.3 * layer)
        w_big, wq_up, wkv_up, w_mem, gain_rows = _layer_params(
            layer, w_in, w_mla_q_up, w_mla_kv_up, w_mem_kv,
            (diff_q_norm_g[layer], diff_k_norm_g[layer], mla_q_lat_g[layer], mla_kv_lat_g[layer],
             mla_q_norm_g[layer], mla_k_norm_g[layer], gqa_q_norm_g[layer], gqa_k_norm_g[layer],
             mem_q_norm_g[layer]))
        qa, ka, va, qb, kb, vb, qc, kc, vc, qd = _inproj(
            x2, row(norm_attn_g[layer]), w_big, wq_up, wkv_up, seg64, seg128, gain_rows, tab, seq)
        kd, vd = _memprep(mem2, row(norm_mem_g[layer]), w_mem, seg128, row(mem_k_norm_g[layer]), tab_m, mem_len)

        o_a = _attention("attn_diff", [(0, 0, 0, 0), (0, 1, 0, 0), (128, 0, 128, 128), (128, 1, 128, 128)],
                         functools.partial(_finish_diff, lambda_init=lambda_init),
                         b3(qa, 512), b3(ka, 512), b3(va, 512), groups=2, q_width=256,
                         k_width=256, v_width=256, o_width=256, tq=512,
                         extra=(diff_lambda[layer].astype(F32), row(diff_subln_g[layer])))
        o_b = _attention("attn_latent", [(128 * u, None, 128 * u, 128 * (u // 2)) for u in range(4)],
                         _finish_pairs, b3(qb, 1024), b3(kb, 1024), b3(vb, 512), groups=2,
                         q_width=512, k_width=512, v_width=256, o_width=256, tq=512, sums_on_mxu=True)
        o_c = _attention("attn_gqa", [(0, 0, 0, 0), (0, 1, 0, 0), (128, 0, 0, 0), (128, 1, 0, 0)],
                         _finish_pairs, b3(qc, 512), b3(kc, 256), b3(vc, 256), groups=2,
                         q_width=256, k_width=128, v_width=128, o_width=256, tq=512, sums_on_mxu=True)
        o_d = _attention("attn_mem", [(128 * u, None, 128 * u, 128 * u) for u in range(4)],
                         _finish_tiles, b3(qd, 512), b3(kd, 512), b3(vd, 512), groups=1,
                         q_width=512, k_width=512, v_width=512, o_width=512, tq=512)
        branches = [o.reshape(n, 512) for o in (o_a, o_b, o_c, o_d)]

        i = layer // 2
        routed = layer % 2 == 1
        w_router = None
        if routed:
            w_r = jnp.pad(moe_router[i].astype(F32), ((0, 0), (0, LANES - N_EXPERTS)))
            w_r_hi = w_r.astype(BF16)
            w_router = jnp.stack([w_r_hi, (w_r - w_r_hi.astype(F32)).astype(BF16)])
        merged = _merge(x2, branches, row(norm_attn_g[layer]), w_gate[layer].astype(BF16),
                        b_gate[layer].astype(F32).reshape(4, 1, D_MODEL), w_branch[layer].astype(BF16),
                        w_out[layer].astype(BF16), row(norm_ffn_g[layer]), w_router)
        if not routed:
            x_mid, h2 = merged
            x2 = _dense_ffn(x_mid, h2, dense_w1[i].astype(BF16), dense_w3[i].astype(BF16),
                            dense_w2[i].astype(BF16))
        else:
            x_mid, h2, route = merged
            block_expert, slot_src, slot_dst = _route_plan(route, n)
            y = _moe_ffn(h2, block_expert, slot_src, slot_dst, moe_w1[i].astype(BF16),
                         moe_w3[i].astype(BF16), moe_w2[i].astype(BF16))
            x2 = _combine(x_mid, y, route)
    return x2.reshape(batch, seq, D_MODEL)
```

```python
import functools
import math

import numpy as np
import jax
import jax.numpy as jnp
from jax import lax
from jax.experimental import pallas as pl
from jax.experimental.pallas import tpu as pltpu

F32 = jnp.float32
BF16 = jnp.bfloat16

D_MODEL = 1024
MEM_GRID_W = 64
ROPE_THETA = 10000.0
NORM_EPS = 1e-6
LOG2_E = math.log2(math.e)
DIFF_HEADS = 4
MLA_HEADS = 8
MLA_QK_DIM = 96
FFN_DIM = 3584
N_EXPERTS = 8
TOP_K = 2
MOE_ROWS = 512
LANES = 128
MXU_DIM = 256
VMEM_LIMIT = 56 * 1024 * 1024

C_DQ, C_DK, C_DV = 0, 512, 1024
C_MQL, C_MKVL, C_KROPE = 1536, 1792, 1920
C_GQ, C_GK, C_GV, C_MEMQ = 2048, 2560, 2816, 3072
IN_COLS_PADDED = 3584
T_A, T_B, T_C, T_D = 0, 256, 512, 768


def _const_spec(shape):
    return pl.BlockSpec(shape, lambda *_: (0,) * len(shape), pipeline_mode=pl.Buffered(1))


def _rms_rows(x, g):
    ms = jnp.mean(x * x, axis=-1, keepdims=True)
    return x * lax.rsqrt(ms + NORM_EPS) * g


def _segment_mean_sq(y, seg_ref, inv_count):
    sq = (y * y).astype(BF16)
    parts = [jnp.dot(sq[:, c:c + MXU_DIM], seg_ref[...], preferred_element_type=F32)
             for c in range(0, y.shape[1], MXU_DIM)]
    ss = parts[0] if len(parts) == 1 else jnp.concatenate(parts, axis=1)
    return ss * inv_count


def _rope_lanes(y, cos, sin_signed, half):
    lane = lax.broadcasted_iota(jnp.int32, y.shape, 1)
    first = (lane % (2 * half)) < half
    rot = jnp.where(first, pltpu.roll(y, LANES - half, 1), pltpu.roll(y, half, 1))
    return y * cos + rot * sin_signed


def _norm_rope_store(y, ms, gain, tab_ref, t_off, half, scale, out_ref):
    yn = y * lax.rsqrt(ms + NORM_EPS)
    cos = tab_ref[:, t_off:t_off + LANES]
    sin = tab_ref[:, t_off + LANES:t_off + 2 * LANES]
    for c in range(0, y.shape[1], LANES):
        r = _rope_lanes(yn[:, c:c + LANES] * gain, cos, sin, half)
        if scale != 1.0:
            r = r * scale
        out_ref[:, c:c + LANES] = r.astype(out_ref.dtype)


def _inproj_body(x_ref, g_ref, w_ref, wq_ref, wkv_ref, seg64_ref, seg128_ref, gains_ref, tab_ref,
                 qa_ref, ka_ref, va_ref, qb_ref, kb_ref, vb_ref, qc_ref, kc_ref, vc_ref, qd_ref):
    h = _rms_rows(x_ref[...], g_ref[...]).astype(BF16)

    def proj(c0, width):
        return jnp.dot(h, w_ref[:, c0:c0 + width], preferred_element_type=F32)

    def gain(row):
        return gains_ref[row:row + 1, :]

    y = proj(C_DQ, 512)
    _norm_rope_store(y, _segment_mean_sq(y, seg64_ref, 1.0 / 64), gain(0), tab_ref, T_A, 32,
                     LOG2_E * 64 ** -0.5, qa_ref)
    y = proj(C_DK, 512)
    _norm_rope_store(y, _segment_mean_sq(y, seg64_ref, 1.0 / 64), gain(1), tab_ref, T_A, 32, 1.0, ka_ref)
    va_ref[...] = proj(C_DV, 512).astype(BF16)

    g_ql = jnp.concatenate([gain(2), gain(3)], axis=1)
    ql = _rms_rows(proj(C_MQL, 256), g_ql).astype(BF16)
    y = jnp.dot(ql, wq_ref[...], preferred_element_type=F32)
    _norm_rope_store(y, _segment_mean_sq(y, seg128_ref, 1.0 / MLA_QK_DIM), gain(5), tab_ref, T_B, 16,
                     LOG2_E * MLA_QK_DIM ** -0.5, qb_ref)
    kvl = _rms_rows(proj(C_MKVL, 128), gain(4)).astype(BF16)
    kv = jnp.dot(kvl, wkv_ref[...], preferred_element_type=F32)
    k_rope = proj(C_KROPE, 128)
    y = kv[:, :1024] + jnp.concatenate([k_rope] * MLA_HEADS, axis=1)
    _norm_rope_store(y, _segment_mean_sq(y, seg128_ref, 1.0 / MLA_QK_DIM), gain(6), tab_ref, T_B, 16,
                     1.0, kb_ref)
    vb_ref[...] = kv[:, 1024:].astype(BF16)

    y = proj(C_GQ, 512)
    _norm_rope_store(y, _segment_mean_sq(y, seg64_ref, 1.0 / 64), gain(7), tab_ref, T_C, 16,
                     LOG2_E * 64 ** -0.5, qc_ref)
    y = proj(C_GK, 256)
    _norm_rope_store(y, _segment_mean_sq(y, seg64_ref, 1.0 / 64), gain(8), tab_ref, T_C, 16, 1.0, kc_ref)
    vc_ref[...] = proj(C_GV, 256).astype(BF16)

    y = proj(C_MEMQ, 512)
    _norm_rope_store(y, _segment_mean_sq(y, seg128_ref, 1.0 / 128), gain(9), tab_ref, T_D, 64,
                     LOG2_E * 128 ** -0.5, qd_ref)


def _inproj(x2, g, w_big, wq_up, wkv_up, seg64, seg128, gains, tab, seq):
    n = x2.shape[0]
    tm = min(512, seq)
    tiles_per_seq = seq // tm
    row = lambda w: pl.BlockSpec((tm, w), lambda i: (i, 0))
    out_widths = (512, 512, 512, 1024, 1024, 512, 512, 256, 256, 512)
    return pl.pallas_call(
        _inproj_body,
        name="inproj",
        grid=(n // tm,),
        in_specs=[row(D_MODEL), _const_spec((1, D_MODEL)), _const_spec(w_big.shape),
                  _const_spec(wq_up.shape), _const_spec(wkv_up.shape), _const_spec(seg64.shape),
                  _const_spec(seg128.shape), _const_spec(gains.shape),
                  pl.BlockSpec((tm, tab.shape[1]), lambda i: (i % tiles_per_seq, 0))],
        out_specs=[row(w) for w in out_widths],
        out_shape=[jax.ShapeDtypeStruct((n, w), BF16) for w in out_widths],
        compiler_params=pltpu.CompilerParams(dimension_semantics=("parallel",),
                                             vmem_limit_bytes=VMEM_LIMIT),
    )(x2, g, w_big, wq_up, wkv_up, seg64, seg128, gains, tab)


def _memprep_body(m_ref, g_ref, w_ref, seg128_ref, gain_ref, tab_ref, k_ref, v_ref):
    mn = _rms_rows(m_ref[...], g_ref[...]).astype(BF16)
    kv = jnp.dot(mn, w_ref[...], preferred_element_type=F32)
    y = kv[:, :512]
    _norm_rope_store(y, _segment_mean_sq(y, seg128_ref, 1.0 / 128), gain_ref[...], tab_ref, 0, 64, 1.0, k_ref)
    v_ref[...] = kv[:, 512:].astype(BF16)


def _memprep(mem2, g, w_kv, seg128, gain, tab_m, mem_len):
    n = mem2.shape[0]
    row = lambda w: pl.BlockSpec((mem_len, w), lambda i: (i, 0))
    return pl.pallas_call(
        _memprep_body,
        name="memprep",
        grid=(n // mem_len,),
        in_specs=[row(D_MODEL), _const_spec((1, D_MODEL)), _const_spec(w_kv.shape),
                  _const_spec(seg128.shape), _const_spec((1, LANES)), _const_spec(tab_m.shape)],
        out_specs=[row(512), row(512)],
        out_shape=[jax.ShapeDtypeStruct((n, 512), BF16)] * 2,
        compiler_params=pltpu.CompilerParams(dimension_semantics=("parallel",),
                                             vmem_limit_bytes=VMEM_LIMIT),
    )(mem2, g, w_kv, seg128, gain, tab_m)


def _attend(q, k, v, sums_on_mxu):
    s = lax.dot_general(q, k, (((1,), (1,)), ((), ())), preferred_element_type=F32)
    e = jnp.exp2(s - jnp.max(s, axis=-1, keepdims=True))
    if not sums_on_mxu:
        l = jnp.sum(e, axis=-1, keepdims=True)
        return jnp.dot(e.astype(BF16), v, preferred_element_type=F32) * (1.0 / l)
    v_ones = jnp.concatenate([v, jnp.ones((v.shape[0], LANES), BF16)], axis=1)
    ov = jnp.dot(e.astype(BF16), v_ones, preferred_element_type=F32)
    return ov[:, :LANES] * (1.0 / ov[:, LANES:LANES + 1])


def _split_halves(q):
    lo = lax.broadcasted_iota(jnp.int32, q.shape, 1) < 64
    zero = jnp.zeros_like(q)
    return jnp.where(lo, q, zero), jnp.where(lo, zero, q)


def _join_halves(o_lo, o_hi):
    lo = lax.broadcasted_iota(jnp.int32, o_lo.shape, 1) < 64
    return jnp.where(lo, o_lo, o_hi)


def _attn_body(*refs, units, finish, n_extra, sums_on_mxu):
    extra = refs[:n_extra]
    q_ref, k_ref, v_ref, o_ref = refs[n_extra:]
    halves = {}
    outs = []
    for q_lane, half, k_lane, v_lane in units:
        q = q_ref[:, q_lane:q_lane + LANES]
        if half is not None:
            if q_lane not in halves:
                halves[q_lane] = _split_halves(q)
            q = halves[q_lane][half]
        outs.append(_attend(q, k_ref[:, k_lane:k_lane + LANES], v_ref[:, v_lane:v_lane + LANES],
                            sums_on_mxu))
    finish(outs, o_ref, *extra)


def _finish_pairs(outs, o_ref):
    for p in range(len(outs) // 2):
        o_ref[:, p * LANES:(p + 1) * LANES] = _join_halves(outs[2 * p], outs[2 * p + 1]).astype(o_ref.dtype)


def _finish_tiles(outs, o_ref):
    for p, o in enumerate(outs):
        o_ref[:, p * LANES:(p + 1) * LANES] = o.astype(o_ref.dtype)


def _finish_diff(outs, o_ref, lam_ref, g_ref, *, lambda_init):
    lp = lam_ref[...]
    lam = (jnp.exp(jnp.sum(lp[0:1] * lp[1:2], keepdims=True))
           - jnp.exp(jnp.sum(lp[2:3] * lp[3:4], keepdims=True)) + lambda_init)
    for p in range(len(outs) // 2):
        o = outs[2 * p] - outs[2 * p + 1] * lam
        o_ref[:, p * LANES:(p + 1) * LANES] = (
            _rms_rows(o, g_ref[...]) * (1.0 - lambda_init)).astype(o_ref.dtype)


def _attention(name, units, finish, q, k, v, *, groups, q_width, k_width, v_width, o_width, tq,
               sums_on_mxu=False, extra=()):
    b, s, _ = q.shape
    kv_len = k.shape[1]
    tq = min(tq, s)
    in_specs = [_const_spec(e.shape) for e in extra] + [
        pl.BlockSpec((None, tq, q_width), lambda bi, gi, qi: (bi, qi, gi)),
        pl.BlockSpec((None, kv_len, k_width), lambda bi, gi, qi: (bi, 0, gi)),
        pl.BlockSpec((None, kv_len, v_width), lambda bi, gi, qi: (bi, 0, gi)),
    ]
    return pl.pallas_call(
        functools.partial(_attn_body, units=units, finish=finish, n_extra=len(extra),
                          sums_on_mxu=sums_on_mxu),
        name=name,
        grid=(b, groups, s // tq),
        in_specs=in_specs,
        out_specs=pl.BlockSpec((None, tq, o_width), lambda bi, gi, qi: (bi, qi, gi)),
        out_shape=jax.ShapeDtypeStruct((b, s, groups * o_width), BF16),
        compiler_params=pltpu.CompilerParams(
            dimension_semantics=("parallel", "parallel", "arbitrary"), vmem_limit_bytes=VMEM_LIMIT),
    )(*extra, q, k, v)


def _route_rows(h2, wr_ref, route_ref):
    h_hi = h2.astype(BF16)
    h_lo = (h2 - h_hi.astype(F32)).astype(BF16)
    logits = (jnp.dot(h_hi, wr_ref[0], preferred_element_type=F32)
              + (jnp.dot(h_hi, wr_ref[1], preferred_element_type=F32)
                 + jnp.dot(h_lo, wr_ref[0], preferred_element_type=F32)))
    l1 = logits.T[:N_EXPERTS, :]
    row = lax.broadcasted_iota(jnp.int32, l1.shape, 0)
    neg = jnp.float32(-jnp.inf)
    m1 = jnp.max(l1, axis=0, keepdims=True)
    i1 = jnp.min(jnp.where(l1 == m1, row, N_EXPERTS), axis=0, keepdims=True)
    l2 = jnp.where(row == i1, neg, l1)
    m2 = jnp.max(l2, axis=0, keepdims=True)
    i2 = jnp.min(jnp.where(l2 == m2, row, N_EXPERTS), axis=0, keepdims=True)
    e2 = jnp.exp(m2 - m1)
    g1 = 1.0 / (1.0 + e2)
    g2 = e2 / (1.0 + e2)
    route_t = jnp.where(row == 0, i1.astype(F32),
                        jnp.where(row == 1, i2.astype(F32),
                                  jnp.where(row == 2, g1, jnp.where(row == 3, g2, 0.0))))
    pad = jnp.zeros((LANES - N_EXPERTS, route_t.shape[1]), F32)
    route_ref[...] = jnp.concatenate([route_t, pad], axis=0).T


def _merge_body(*refs, with_router):
    if with_router:
        (x_ref, oa_ref, ob_ref, oc_ref, od_ref, g_ref, wg_ref, bg_ref, wb_ref, wo_ref, gf_ref, wr_ref,
         xo_ref, h2_ref, route_ref, hprev_ref) = refs

        @pl.when(pl.program_id(0) == 0)
        def _():
            hprev_ref[...] = jnp.zeros(hprev_ref.shape, F32)

        _route_rows(hprev_ref[...], wr_ref, route_ref)
    else:
        (x_ref, oa_ref, ob_ref, oc_ref, od_ref, g_ref, wg_ref, bg_ref, wb_ref, wo_ref, gf_ref,
         xo_ref, h2_ref) = refs
    x = x_ref[...]
    h = _rms_rows(x, g_ref[...]).astype(BF16)
    merged = None
    for b, o_ref in enumerate((oa_ref, ob_ref, oc_ref, od_ref)):
        gate = jax.nn.sigmoid(jnp.dot(h, wg_ref[b], preferred_element_type=F32) + bg_ref[b])
        term = gate * jnp.dot(o_ref[...], wb_ref[b], preferred_element_type=F32)
        merged = term if merged is None else merged + term
    x_new = x + jnp.dot(merged.astype(BF16), wo_ref[...], preferred_element_type=F32)
    xo_ref[...] = x_new
    h2 = _rms_rows(x_new, gf_ref[...])
    h2_ref[...] = h2.astype(h2_ref.dtype)
    if with_router:
        hprev_ref[...] = h2


def _merge(x2, branches, g, wg, bg, wb, wo, gf, w_router=None):
    n = x2.shape[0]
    tm = min(512, n)
    n_tiles = n // tm
    with_router = w_router is not None
    row = lambda w: pl.BlockSpec((tm, w), lambda i: (jnp.minimum(i, n_tiles - 1), 0))
    in_specs = ([row(D_MODEL)] + [row(512)] * 4
                + [_const_spec((1, D_MODEL)), _const_spec(wg.shape), _const_spec(bg.shape),
                   _const_spec(wb.shape), _const_spec(wo.shape), _const_spec((1, D_MODEL))])
    args = [x2, *branches, g, wg, bg, wb, wo, gf]
    out_specs = [row(D_MODEL), row(D_MODEL)]
    out_shape = [jax.ShapeDtypeStruct((n, D_MODEL), F32),
                 jax.ShapeDtypeStruct((n, D_MODEL), F32 if with_router else BF16)]
    scratch_shapes = []
    if with_router:
        in_specs.append(_const_spec(w_router.shape))
        args.append(w_router)
        out_specs.append(pl.BlockSpec((tm, LANES), lambda i: (jnp.maximum(i - 1, 0), 0)))
        out_shape.append(jax.ShapeDtypeStruct((n, LANES), F32))
        scratch_shapes.append(pltpu.VMEM((tm, D_MODEL), F32))
    return pl.pallas_call(
        functools.partial(_merge_body, with_router=with_router),
        name="merge_router" if with_router else "merge",
        grid=(n_tiles + 1 if with_router else n_tiles,),
        in_specs=in_specs, out_specs=out_specs, out_shape=out_shape, scratch_shapes=scratch_shapes,
        compiler_params=pltpu.CompilerParams(
            dimension_semantics=("arbitrary",) if with_router else ("parallel",),
            vmem_limit_bytes=VMEM_LIMIT),
    )(*args)


def _swiglu_chunk(xb, w1, w3, w2):
    a = jnp.dot(xb, w1, preferred_element_type=F32)
    b = jnp.dot(xb, w3, preferred_element_type=F32)
    return jnp.dot((jax.nn.silu(a) * b).astype(BF16), w2, preferred_element_type=F32)


def _ffn_body(x_ref, h_ref, w1_ref, w3_ref, w2_ref, o_ref):
    @pl.when(pl.program_id(1) == 0)
    def _():
        o_ref[...] = x_ref[...]

    o_ref[...] += _swiglu_chunk(h_ref[...], w1_ref[...], w3_ref[...], w2_ref[...])


def _dense_ffn(x2, h2, w1, w3, w2):
    n = x2.shape[0]
    tm = min(512, n)
    tf = 1792
    return pl.pallas_call(
        _ffn_body,
        name="dense_ffn",
        grid=(n // tm, FFN_DIM // tf),
        in_specs=[pl.BlockSpec((tm, D_MODEL), lambda i, f: (i, 0)),
                  pl.BlockSpec((tm, D_MODEL), lambda i, f: (i, 0)),
                  pl.BlockSpec((D_MODEL, tf), lambda i, f: (0, f)),
                  pl.BlockSpec((D_MODEL, tf), lambda i, f: (0, f)),
                  pl.BlockSpec((tf, D_MODEL), lambda i, f: (f, 0))],
        out_specs=pl.BlockSpec((tm, D_MODEL), lambda i, f: (i, 0)),
        out_shape=jax.ShapeDtypeStruct((n, D_MODEL), F32),
        compiler_params=pltpu.CompilerParams(dimension_semantics=("parallel", "arbitrary"),
                                             vmem_limit_bytes=VMEM_LIMIT),
    )(x2, h2, w1, w3, w2)


MOE_TF = 1792


def _moe_body(expert_ref, src_ref, dst_ref, h_hbm, w1_ref, w3_ref, w2_ref, y_hbm,
              xg_ref, xb_ref, acc_ref, gsem, ssem):
    j = pl.program_id(0)
    f = pl.program_id(1)
    n_f = pl.num_programs(1)
    slot = j % 2
    other = 1 - slot
    rows_per_step = MOE_ROWS // (FFN_DIM // MOE_TF)

    def gather_copy(block, r, buf):
        tok = src_ref[block * MOE_ROWS + r]
        return pltpu.make_async_copy(h_hbm.at[pl.ds(tok, 1), :], xg_ref.at[buf, pl.ds(r, 1), :],
                                     gsem.at[buf])

    def scatter_copy(block, r, buf):
        dst = dst_ref[(block + 1) * MOE_ROWS + r]
        return pltpu.make_async_copy(acc_ref.at[buf, pl.ds(r, 1), :], y_hbm.at[pl.ds(dst, 1), :], ssem)

    def wait_rows(copy):
        for _ in range(MOE_ROWS):
            copy.wait()

    @pl.when(jnp.logical_and(j == 0, f == 0))
    def _():
        def start(r, c):
            gather_copy(0, r, 0).start()
            return c
        lax.fori_loop(0, MOE_ROWS, start, 0)
        acc_ref[1] = jnp.zeros((MOE_ROWS, D_MODEL), F32)

    @pl.when(f == 0)
    def _():
        wait_rows(gather_copy(j, 0, slot))
        xb_ref[...] = xg_ref[slot].astype(BF16)

    for step in range(FFN_DIM // MOE_TF):
        @pl.when(f == step)
        def _():
            for r in range(step * rows_per_step, (step + 1) * rows_per_step):
                gather_copy(j + 1, r, other).start()
                scatter_copy(j - 1, r, other).start()

    @pl.when(f == 0)
    def _():
        acc_ref[slot] = jnp.zeros((MOE_ROWS, D_MODEL), F32)

    acc_ref[slot] += _swiglu_chunk(xb_ref[...], w1_ref[...], w3_ref[...], w2_ref[...])

    @pl.when(f == n_f - 1)
    def _():
        wait_rows(scatter_copy(j - 1, 0, other))

    @pl.when(jnp.logical_and(j == pl.num_programs(0) - 1, f == n_f - 1))
    def _():
        def start(r, c):
            scatter_copy(j, r, slot).start()
            return c
        lax.fori_loop(0, MOE_ROWS, start, 0)
        wait_rows(scatter_copy(j, 0, slot))
        wait_rows(gather_copy(j + 1, 0, other))


def _moe_ffn(h2, block_expert, slot_src, slot_dst, w1, w3, w2):
    n = h2.shape[0]
    n_blocks = block_expert.shape[0]
    wspec = lambda shape, imap: pl.BlockSpec(shape, imap)
    grid_spec = pltpu.PrefetchScalarGridSpec(
        num_scalar_prefetch=3,
        grid=(n_blocks, FFN_DIM // MOE_TF),
        in_specs=[pl.BlockSpec(memory_space=pl.ANY),
                  wspec((None, D_MODEL, MOE_TF), lambda j, f, e, s, d: (e[j], 0, f)),
                  wspec((None, D_MODEL, MOE_TF), lambda j, f, e, s, d: (e[j], 0, f)),
                  wspec((None, MOE_TF, D_MODEL), lambda j, f, e, s, d: (e[j], f, 0))],
        out_specs=pl.BlockSpec(memory_space=pl.ANY),
        scratch_shapes=[pltpu.VMEM((2, MOE_ROWS, D_MODEL), F32),
                        pltpu.VMEM((MOE_ROWS, D_MODEL), BF16),
                        pltpu.VMEM((2, MOE_ROWS, D_MODEL), F32),
                        pltpu.SemaphoreType.DMA((2,)),
                        pltpu.SemaphoreType.DMA(())],
    )
    return pl.pallas_call(
        _moe_body,
        name="moe_ffn",
        grid_spec=grid_spec,
        out_shape=jax.ShapeDtypeStruct((TOP_K * n + MOE_ROWS, D_MODEL), F32),
        compiler_params=pltpu.CompilerParams(dimension_semantics=("arbitrary", "arbitrary"),
                                             vmem_limit_bytes=VMEM_LIMIT),
    )(block_expert, slot_src, slot_dst, h2, w1, w3, w2)


def _combine_body(x_ref, y1_ref, y2_ref, route_ref, o_ref):
    g1 = route_ref[:, 2:3]
    g2 = route_ref[:, 3:4]
    o_ref[...] = x_ref[...] + (y1_ref[...] * g1 + y2_ref[...] * g2)


def _combine(x2, y, route):
    n = x2.shape[0]
    tm = min(512, n)
    row = lambda w: pl.BlockSpec((tm, w), lambda i: (i, 0))
    return pl.pallas_call(
        _combine_body,
        name="moe_combine",
        grid=(n // tm,),
        in_specs=[row(D_MODEL), row(D_MODEL), pl.BlockSpec((tm, D_MODEL), lambda i: (i + n // tm, 0)),
                  row(LANES)],
        out_specs=row(D_MODEL),
        out_shape=jax.ShapeDtypeStruct((n, D_MODEL), F32),
        compiler_params=pltpu.CompilerParams(dimension_semantics=("parallel",),
                                             vmem_limit_bytes=VMEM_LIMIT),
    )(x2, y, y, route)


def _route_plan(route, n):
    n_assign = n * TOP_K
    flat_e = route[:, :TOP_K].astype(jnp.int32).reshape(-1)
    onehot = (flat_e[:, None] == jnp.arange(N_EXPERTS, dtype=jnp.int32)[None, :]).astype(jnp.int32)
    ranks = jnp.cumsum(onehot, axis=0) - onehot
    counts = jnp.sum(onehot, axis=0)
    rank = jnp.sum(ranks * onehot, axis=1)
    padded = (counts + MOE_ROWS - 1) // MOE_ROWS * MOE_ROWS
    pad_end = jnp.cumsum(padded)
    pad_start = pad_end - padded
    slot = pad_start[flat_e] + rank
    n_blocks = (n_assign + N_EXPERTS * (MOE_ROWS - 1)) // MOE_ROWS
    pad_ids = n_assign + jnp.arange(MOE_ROWS, dtype=jnp.int32)
    slot_assign = jnp.tile(pad_ids, n_blocks).at[slot].set(jnp.arange(n_assign, dtype=jnp.int32))
    valid = slot_assign < n_assign
    slot_src = jnp.where(valid, slot_assign // TOP_K, 0)
    slot_dst = jnp.where(valid, (slot_assign % TOP_K) * n + slot_assign // TOP_K, slot_assign)
    block_start = jnp.arange(n_blocks, dtype=jnp.int32) * MOE_ROWS
    block_expert = jnp.minimum(jnp.searchsorted(pad_end, block_start, side='right'),
                               N_EXPERTS - 1).astype(jnp.int32)
    return (block_expert, jnp.concatenate([slot_src, jnp.zeros((MOE_ROWS,), jnp.int32)]),
            jnp.concatenate([pad_ids, slot_dst]))


def _inv_freq(dim):
    return 1.0 / (ROPE_THETA ** (jnp.arange(0, dim, 2, dtype=F32) / dim))


def _rope_table(pos_of_lane, freq_of_lane, first_half, active=None):
    ang = pos_of_lane * freq_of_lane[None, :]
    cos, sin = jnp.cos(ang), jnp.sin(ang) * jnp.where(first_half, -1.0, 1.0)[None, :]
    if active is not None:
        cos = jnp.where(active[None, :], cos, 1.0)
        sin = jnp.where(active[None, :], sin, 0.0)
    return jnp.concatenate([cos, sin], axis=1).astype(F32)


def _rope_tables(seq, mem_len):
    lane = np.arange(LANES)
    pos = jnp.arange(seq, dtype=jnp.int32)
    posf = jnp.broadcast_to(pos.astype(F32)[:, None], (seq, LANES))
    tab_a = _rope_table(posf, _inv_freq(64)[lane % 32], (lane % 64) < 32)
    tab_b = _rope_table(posf, _inv_freq(32)[lane % 16], (lane % 32) < 16, (lane >= 64) & (lane < 96))
    rows = (pos // MEM_GRID_W).astype(F32)[:, None]
    cols = (pos % MEM_GRID_W).astype(F32)[:, None]
    pos_c = jnp.where(((lane % 64) < 32)[None, :], rows, cols)
    tab_c = _rope_table(pos_c, _inv_freq(32)[lane % 16], (lane % 32) < 16)
    f_d = _inv_freq(128)[lane % 64]
    tab_d = _rope_table(posf, f_d, lane < 64)
    mposf = jnp.broadcast_to(jnp.arange(mem_len, dtype=F32)[:, None], (mem_len, LANES))
    tab_m = _rope_table(mposf, f_d, lane < 64)
    return jnp.concatenate([tab_a, tab_b, tab_c, tab_d], axis=1), tab_m


def _segment_matrix(width):
    i = np.arange(MXU_DIM)
    return jnp.asarray((i[:, None] // width) == (i[None, :] // width), dtype=BF16)


def _tile_row(g, reps):
    return jnp.tile(g.astype(F32), reps).reshape(1, -1)


def _layer_params(layer, w_in, w_mla_q_up, w_mla_kv_up, w_mem_kv, gains):
    w = w_in[layer]
    z = lambda c: jnp.zeros((D_MODEL, c), F32)
    gk, gv = w[:, 2464:2592], w[:, 2592:2720]
    dup = lambda m: jnp.concatenate([m[:, :64], m[:, :64], m[:, 64:], m[:, 64:]], axis=1)
    w_big = jnp.concatenate([
        w[:, 0:1536], w[:, 1536:1792], w[:, 1792:1920],
        z(64), w[:, 1920:1952], z(32),
        w[:, 1952:2464], dup(gk), dup(gv), w[:, 2720:3232]], axis=1)
    wq = w_mla_q_up[layer].reshape(256, MLA_HEADS, MLA_QK_DIM)
    wq = jnp.pad(wq, ((0, 0), (0, 0), (0, LANES - MLA_QK_DIM))).reshape(256, MLA_HEADS * LANES)
    wkv = w_mla_kv_up[layer].reshape(128, MLA_HEADS, 128)
    wk = jnp.pad(wkv[:, :, :64], ((0, 0), (0, 0), (0, 64))).reshape(128, MLA_HEADS * LANES)
    wv = wkv[:, :, 64:].reshape(128, MLA_HEADS * 64)
    wm = w_mem_kv[layer].reshape(D_MODEL, 4, 256)
    w_mem = jnp.concatenate([wm[:, :, :128].reshape(D_MODEL, 512), wm[:, :, 128:].reshape(D_MODEL, 512)], axis=1)
    pad96 = lambda g: jnp.pad(g.astype(F32), (0, LANES - MLA_QK_DIM)).reshape(1, LANES)
    (dq_g, dk_g, ql_g, kvl_g, mq_g, mk_g, gq_g, gk_g, memq_g) = gains
    gain_rows = jnp.concatenate([
        _tile_row(dq_g, 2), _tile_row(dk_g, 2), ql_g.astype(F32).reshape(2, LANES),
        kvl_g.astype(F32).reshape(1, LANES), pad96(mq_g), pad96(mk_g),
        _tile_row(gq_g, 2), _tile_row(gk_g, 2), memq_g.astype(F32).reshape(1, LANES),
        jnp.zeros((6, LANES), F32)], axis=0)
    return (w_big.astype(BF16), wq.astype(BF16), jnp.concatenate([wk, wv], axis=1).astype(BF16),
            w_mem.astype(BF16), gain_rows)


def kernel(x, mem, norm_attn_g, norm_mem_g, w_in, diff_q_norm_g, diff_k_norm_g, diff_lambda, diff_subln_g,
           mla_q_lat_g, mla_kv_lat_g, w_mla_q_up, w_mla_kv_up, mla_q_norm_g, mla_k_norm_g, gqa_q_norm_g,
           gqa_k_norm_g, w_mem_kv, mem_q_norm_g, mem_k_norm_g, w_branch, w_gate, b_gate, w_out, norm_ffn_g,
           dense_w1, dense_w3, dense_w2, moe_router, moe_w1, moe_w3, moe_w2):
    batch, seq, _ = x.shape
    mem_len = mem.shape[1]
    n = batch * seq
    depth = w_in.shape[0]
    tab, tab_m = _rope_tables(seq, mem_len)
    seg64, seg128 = _segment_matrix(64), _segment_matrix(128)
    x2 = x.reshape(n, D_MODEL)
    mem2 = mem.reshape(batch * mem_len, D_MODEL)
    row = lambda g: g.astype(F32).reshape(1, -1)
    b3 = lambda a, w: a.reshape(batch, -1, w)

    for layer in range(depth):
        lambda_init = 0.8 - 0.6 * math.exp(-0.3 * layer)
        w_big, wq_up, wkv_up, w_mem, gain_rows = _layer_params(
            layer, w_in, w_mla_q_up, w_mla_kv_up, w_mem_kv,
            (diff_q_norm_g[layer], diff_k_norm_g[layer], mla_q_lat_g[layer], mla_kv_lat_g[layer],
             mla_q_norm_g[layer], mla_k_norm_g[layer], gqa_q_norm_g[layer], gqa_k_norm_g[layer],
             mem_q_norm_g[layer]))
        qa, ka, va, qb, kb, vb, qc, kc, vc, qd = _inproj(
            x2, row(norm_attn_g[layer]), w_big, wq_up, wkv_up, seg64, seg128, gain_rows, tab, seq)
        kd, vd = _memprep(mem2, row(norm_mem_g[layer]), w_mem, seg128, row(mem_k_norm_g[layer]), tab_m, mem_len)

        o_a = _attention("attn_diff", [(0, 0, 0, 0), (0, 1, 0, 0), (128, 0, 128, 128), (128, 1, 128, 128)],
                         functools.partial(_finish_diff, lambda_init=lambda_init),
                         b3(qa, 512), b3(ka, 512), b3(va, 512), groups=2, q_width=256,
                         k_width=256, v_width=256, o_width=256, tq=512,
                         extra=(diff_lambda[layer].astype(F32), row(diff_subln_g[layer])))
        o_b = _attention("attn_latent", [(128 * u, None, 128 * u, 128 * (u // 2)) for u in range(4)],
                         _finish_pairs, b3(qb, 1024), b3(kb, 1024), b3(vb, 512), groups=2,
                         q_width=512, k_width=512, v_width=256, o_width=256, tq=512, sums_on_mxu=True)
        o_c = _attention("attn_gqa", [(0, 0, 0, 0), (0, 1, 0, 0), (128, 0, 0, 0), (128, 1, 0, 0)],
                         _finish_pairs, b3(qc, 512), b3(kc, 256), b3(vc, 256), groups=2,
                         q_width=256, k_width=128, v_width=128, o_width=256, tq=512, sums_on_mxu=True)
        o_d = _attention("attn_mem", [(128 * u, None, 128 * u, 128 * u) for u in range(4)],
                         _finish_tiles, b3(qd, 512), b3(kd, 512), b3(vd, 512), groups=1,
                         q_width=512, k_width=512, v_width=512, o_width=512, tq=512)
        branches = [o.reshape(n, 512) for o in (o_a, o_b, o_c, o_d)]

        i = layer // 2
        routed = layer % 2 == 1
        w_router = None
        if routed:
            w_r = jnp.pad(moe_router[i].astype(F32), ((0, 0), (0, LANES - N_EXPERTS)))
            w_r_hi = w_r.astype(BF16)
            w_router = jnp.stack([w_r_hi, (w_r - w_r_hi.astype(F32)).astype(BF16)])
        merged = _merge(x2, branches, row(norm_attn_g[layer]), w_gate[layer].astype(BF16),
                        b_gate[layer].astype(F32).reshape(4, 1, D_MODEL), w_branch[layer].astype(BF16),
                        w_out[layer].astype(BF16), row(norm_ffn_g[layer]), w_router)
        if not routed:
            x_mid, h2 = merged
            x2 = _dense_ffn(x_mid, h2, dense_w1[i].astype(BF16), dense_w3[i].astype(BF16),
                            dense_w2[i].astype(BF16))
        else:
            x_mid, h2, route = merged
            block_expert, slot_src, slot_dst = _route_plan(route, n)
            y = _moe_ffn(h2, block_expert, slot_src, slot_dst, moe_w1[i].astype(BF16),
                         moe_w3[i].astype(BF16), moe_w2[i].astype(BF16))
            x2 = _combine(x_mid, y, route)
    return x2.reshape(batch, seq, D_MODEL)
```

```python
import functools
import math

import numpy as np
import jax
import jax.numpy as jnp
from jax import lax
from jax.experimental import pallas as pl
from jax.experimental.pallas import tpu as pltpu

F32 = jnp.float32
BF16 = jnp.bfloat16

D_MODEL = 1024
MEM_GRID_W = 64
ROPE_THETA = 10000.0
NORM_EPS = 1e-6
LOG2_E = math.log2(math.e)
DIFF_HEADS = 4
MLA_HEADS = 8
MLA_QK_DIM = 96
FFN_DIM = 3584
N_EXPERTS = 8
TOP_K = 2
MOE_ROWS = 512
LANES = 128
MXU_DIM = 256
VMEM_LIMIT = 56 * 1024 * 1024

C_DQ, C_DK, C_DV = 0, 512, 1024
C_MQL, C_MKVL, C_KROPE = 1536, 1792, 1920
C_GQ, C_GK, C_GV, C_MEMQ = 2048, 2560, 2816, 3072
IN_COLS_PADDED = 3584
T_A, T_B, T_C, T_D = 0, 256, 512, 768


def _const_spec(shape):
    return pl.BlockSpec(shape, lambda *_: (0,) * len(shape), pipeline_mode=pl.Buffered(1))


def _rms_rows(x, g):
    ms = jnp.mean(x * x, axis=-1, keepdims=True)
    return x * lax.rsqrt(ms + NORM_EPS) * g


def _segment_mean_sq(y, seg_ref, inv_count):
    sq = (y * y).astype(BF16)
    parts = [jnp.dot(sq[:, c:c + MXU_DIM], seg_ref[...], preferred_element_type=F32)
             for c in range(0, y.shape[1], MXU_DIM)]
    ss = parts[0] if len(parts) == 1 else jnp.concatenate(parts, axis=1)
    return ss * inv_count


def _rope_lanes(y, cos, sin_signed, half):
    lane = lax.broadcasted_iota(jnp.int32, y.shape, 1)
    first = (lane % (2 * half)) < half
    rot = jnp.where(first, pltpu.roll(y, LANES - half, 1), pltpu.roll(y, half, 1))
    return y * cos + rot * sin_signed


def _norm_rope_store(y, ms, gain, tab_ref, t_off, half, scale, out_ref):
    yn = y * lax.rsqrt(ms + NORM_EPS)
    cos = tab_ref[:, t_off:t_off + LANES]
    sin = tab_ref[:, t_off + LANES:t_off + 2 * LANES]
    for c in range(0, y.shape[1], LANES):
        r = _rope_lanes(yn[:, c:c + LANES] * gain, cos, sin, half)
        if scale != 1.0:
            r = r * scale
        out_ref[:, c:c + LANES] = r.astype(out_ref.dtype)


def _inproj_body(x_ref, g_ref, w_ref, wq_ref, wkv_ref, seg64_ref, seg128_ref, gains_ref, tab_ref,
                 qa_ref, ka_ref, va_ref, qb_ref, kb_ref, vb_ref, qc_ref, kc_ref, vc_ref, qd_ref):
    h = _rms_rows(x_ref[...], g_ref[...]).astype(BF16)

    def proj(c0, width):
        return jnp.dot(h, w_ref[:, c0:c0 + width], preferred_element_type=F32)

    def gain(row):
        return gains_ref[row:row + 1, :]

    y = proj(C_DQ, 512)
    _norm_rope_store(y, _segment_mean_sq(y, seg64_ref, 1.0 / 64), gain(0), tab_ref, T_A, 32,
                     LOG2_E * 64 ** -0.5, qa_ref)
    y = proj(C_DK, 512)
    _norm_rope_store(y, _segment_mean_sq(y, seg64_ref, 1.0 / 64), gain(1), tab_ref, T_A, 32, 1.0, ka_ref)
    va_ref[...] = proj(C_DV, 512).astype(BF16)

    g_ql = jnp.concatenate([gain(2), gain(3)], axis=1)
    ql = _rms_rows(proj(C_MQL, 256), g_ql).astype(BF16)
    y = jnp.dot(ql, wq_ref[...], preferred_element_type=F32)
    _norm_rope_store(y, _segment_mean_sq(y, seg128_ref, 1.0 / MLA_QK_DIM), gain(5), tab_ref, T_B, 16,
                     LOG2_E * MLA_QK_DIM ** -0.5, qb_ref)
    kvl = _rms_rows(proj(C_MKVL, 128), gain(4)).astype(BF16)
    kv = jnp.dot(kvl, wkv_ref[...], preferred_element_type=F32)
    k_rope = proj(C_KROPE, 128)
    y = kv[:, :1024] + jnp.concatenate([k_rope] * MLA_HEADS, axis=1)
    _norm_rope_store(y, _segment_mean_sq(y, seg128_ref, 1.0 / MLA_QK_DIM), gain(6), tab_ref, T_B, 16,
                     1.0, kb_ref)
    vb_ref[...] = kv[:, 1024:].astype(BF16)

    y = proj(C_GQ, 512)
    _norm_rope_store(y, _segment_mean_sq(y, seg64_ref, 1.0 / 64), gain(7), tab_ref, T_C, 16,
                     LOG2_E * 64 ** -0.5, qc_ref)
    y = proj(C_GK, 256)
    _norm_rope_store(y, _segment_mean_sq(y, seg64_ref, 1.0 / 64), gain(8), tab_ref, T_C, 16, 1.0, kc_ref)
    vc_ref[...] = proj(C_GV, 256).astype(BF16)

    y = proj(C_MEMQ, 512)
    _norm_rope_store(y, _segment_mean_sq(y, seg128_ref, 1.0 / 128), gain(9), tab_ref, T_D, 64,
                     LOG2_E * 128 ** -0.5, qd_ref)


def _inproj(x2, g, w_big, wq_up, wkv_up, seg64, seg128, gains, tab, seq):
    n = x2.shape[0]
    tm = min(512, seq)
    tiles_per_seq = seq // tm
    row = lambda w: pl.BlockSpec((tm, w), lambda i: (i, 0))
    out_widths = (512, 512, 512, 1024, 1024, 512, 512, 256, 256, 512)
    return pl.pallas_call(
        _inproj_body,
        name="inproj",
        grid=(n // tm,),
        in_specs=[row(D_MODEL), _const_spec((1, D_MODEL)), _const_spec(w_big.shape),
                  _const_spec(wq_up.shape), _const_spec(wkv_up.shape), _const_spec(seg64.shape),
                  _const_spec(seg128.shape), _const_spec(gains.shape),
                  pl.BlockSpec((tm, tab.shape[1]), lambda i: (i % tiles_per_seq, 0))],
        out_specs=[row(w) for w in out_widths],
        out_shape=[jax.ShapeDtypeStruct((n, w), BF16) for w in out_widths],
        compiler_params=pltpu.CompilerParams(dimension_semantics=("parallel",),
                                             vmem_limit_bytes=VMEM_LIMIT),
    )(x2, g, w_big, wq_up, wkv_up, seg64, seg128, gains, tab)


def _memprep_body(m_ref, g_ref, w_ref, seg128_ref, gain_ref, tab_ref, k_ref, v_ref):
    mn = _rms_rows(m_ref[...], g_ref[...]).astype(BF16)
    kv = jnp.dot(mn, w_ref[...], preferred_element_type=F32)
    y = kv[:, :512]
    _norm_rope_store(y, _segment_mean_sq(y, seg128_ref, 1.0 / 128), gain_ref[...], tab_ref, 0, 64, 1.0, k_ref)
    v_ref[...] = kv[:, 512:].astype(BF16)


def _memprep(mem2, g, w_kv, seg128, gain, tab_m, mem_len):
    n = mem2.shape[0]
    row = lambda w: pl.BlockSpec((mem_len, w), lambda i: (i, 0))
    return pl.pallas_call(
        _memprep_body,
        name="memprep",
        grid=(n // mem_len,),
        in_specs=[row(D_MODEL), _const_spec((1, D_MODEL)), _const_spec(w_kv.shape),
                  _const_spec(seg128.shape), _const_spec((1, LANES)), _const_spec(tab_m.shape)],
        out_specs=[row(512), row(512)],
        out_shape=[jax.ShapeDtypeStruct((n, 512), BF16)] * 2,
        compiler_params=pltpu.CompilerParams(dimension_semantics=("parallel",),
                                             vmem_limit_bytes=VMEM_LIMIT),
    )(mem2, g, w_kv, seg128, gain, tab_m)


def _attend(q, k, v, sums_on_mxu):
    s = lax.dot_general(q, k, (((1,), (1,)), ((), ())), preferred_element_type=F32)
    e = jnp.exp2(s - jnp.max(s, axis=-1, keepdims=True))
    if not sums_on_mxu:
        l = jnp.sum(e, axis=-1, keepdims=True)
        return jnp.dot(e.astype(BF16), v, preferred_element_type=F32) * (1.0 / l)
    v_ones = jnp.concatenate([v, jnp.ones((v.shape[0], LANES), BF16)], axis=1)
    ov = jnp.dot(e.astype(BF16), v_ones, preferred_element_type=F32)
    return ov[:, :LANES] * (1.0 / ov[:, LANES:LANES + 1])


def _split_halves(q):
    lo = lax.broadcasted_iota(jnp.int32, q.shape, 1) < 64
    zero = jnp.zeros_like(q)
    return jnp.where(lo, q, zero), jnp.where(lo, zero, q)


def _join_halves(o_lo, o_hi):
    lo = lax.broadcasted_iota(jnp.int32, o_lo.shape, 1) < 64
    return jnp.where(lo, o_lo, o_hi)


def _attn_body(*refs, units, finish, n_extra, sums_on_mxu):
    extra = refs[:n_extra]
    q_ref, k_ref, v_ref, o_ref = refs[n_extra:]
    halves = {}
    outs = []
    for q_lane, half, k_lane, v_lane in units:
        q = q_ref[:, q_lane:q_lane + LANES]
        if half is not None:
            if q_lane not in halves:
                halves[q_lane] = _split_halves(q)
            q = halves[q_lane][half]
        outs.append(_attend(q, k_ref[:, k_lane:k_lane + LANES], v_ref[:, v_lane:v_lane + LANES],
                            sums_on_mxu))
    finish(outs, o_ref, *extra)


def _finish_pairs(outs, o_ref):
    for p in range(len(outs) // 2):
        o_ref[:, p * LANES:(p + 1) * LANES] = _join_halves(outs[2 * p], outs[2 * p + 1]).astype(o_ref.dtype)


def _finish_tiles(outs, o_ref):
    for p, o in enumerate(outs):
        o_ref[:, p * LANES:(p + 1) * LANES] = o.astype(o_ref.dtype)


def _finish_diff(outs, o_ref, lam_ref, g_ref, *, lambda_init):
    lp = lam_ref[...]
    lam = (jnp.exp(jnp.sum(lp[0:1] * lp[1:2], keepdims=True))
           - jnp.exp(jnp.sum(lp[2:3] * lp[3:4], keepdims=True)) + lambda_init)
    for p in range(len(outs) // 2):
        o = outs[2 * p] - outs[2 * p + 1] * lam
        o_ref[:, p * LANES:(p + 1) * LANES] = (
            _rms_rows(o, g_ref[...]) * (1.0 - lambda_init)).astype(o_ref.dtype)


def _attention(name, units, finish, q, k, v, *, groups, q_width, k_width, v_width, o_width, tq,
               sums_on_mxu=False, extra=()):
    b, s, _ = q.shape
    kv_len = k.shape[1]
    tq = min(tq, s)
    in_specs = [_const_spec(e.shape) for e in extra] + [
        pl.BlockSpec((None, tq, q_width), lambda bi, gi, qi: (bi, qi, gi)),
        pl.BlockSpec((None, kv_len, k_width), lambda bi, gi, qi: (bi, 0, gi)),
        pl.BlockSpec((None, kv_len, v_width), lambda bi, gi, qi: (bi, 0, gi)),
    ]
    return pl.pallas_call(
        functools.partial(_attn_body, units=units, finish=finish, n_extra=len(extra),
                          sums_on_mxu=sums_on_mxu),
        name=name,
        grid=(b, groups, s // tq),
        in_specs=in_specs,
        out_specs=pl.BlockSpec((None, tq, o_width), lambda bi, gi, qi: (bi, qi, gi)),
        out_shape=jax.ShapeDtypeStruct((b, s, groups * o_width), BF16),
        compiler_params=pltpu.CompilerParams(
            dimension_semantics=("parallel", "parallel", "arbitrary"), vmem_limit_bytes=VMEM_LIMIT),
    )(*extra, q, k, v)


def _route_rows(h2, wr_ref, route_ref):
    h_hi = h2.astype(BF16)
    h_lo = (h2 - h_hi.astype(F32)).astype(BF16)
    logits = (jnp.dot(h_hi, wr_ref[0], preferred_element_type=F32)
              + (jnp.dot(h_hi, wr_ref[1], preferred_element_type=F32)
                 + jnp.dot(h_lo, wr_ref[0], preferred_element_type=F32)))
    l1 = logits.T[:N_EXPERTS, :]
    row = lax.broadcasted_iota(jnp.int32, l1.shape, 0)
    neg = jnp.float32(-jnp.inf)
    m1 = jnp.max(l1, axis=0, keepdims=True)
    i1 = jnp.min(jnp.where(l1 == m1, row, N_EXPERTS), axis=0, keepdims=True)
    l2 = jnp.where(row == i1, neg, l1)
    m2 = jnp.max(l2, axis=0, keepdims=True)
    i2 = jnp.min(jnp.where(l2 == m2, row, N_EXPERTS), axis=0, keepdims=True)
    e2 = jnp.exp(m2 - m1)
    g1 = 1.0 / (1.0 + e2)
    g2 = e2 / (1.0 + e2)
    route_t = jnp.where(row == 0, i1.astype(F32),
                        jnp.where(row == 1, i2.astype(F32),
                                  jnp.where(row == 2, g1, jnp.where(row == 3, g2, 0.0))))
    pad = jnp.zeros((LANES - N_EXPERTS, route_t.shape[1]), F32)
    route_ref[...] = jnp.concatenate([route_t, pad], axis=0).T


def _merge_body(*refs, with_router):
    if with_router:
        (x_ref, oa_ref, ob_ref, oc_ref, od_ref, g_ref, wg_ref, bg_ref, wb_ref, wo_ref, gf_ref, wr_ref,
         xo_ref, h2_ref, route_ref, hprev_ref) = refs

        @pl.when(pl.program_id(0) == 0)
        def _():
            hprev_ref[...] = jnp.zeros(hprev_ref.shape, F32)

        _route_rows(hprev_ref[...], wr_ref, route_ref)
    else:
        (x_ref, oa_ref, ob_ref, oc_ref, od_ref, g_ref, wg_ref, bg_ref, wb_ref, wo_ref, gf_ref,
         xo_ref, h2_ref) = refs
    x = x_ref[...]
    h = _rms_rows(x, g_ref[...]).astype(BF16)
    merged = None
    for b, o_ref in enumerate((oa_ref, ob_ref, oc_ref, od_ref)):
        gate = jax.nn.sigmoid(jnp.dot(h, wg_ref[b], preferred_element_type=F32) + bg_ref[b])
        term = gate * jnp.dot(o_ref[...], wb_ref[b], preferred_element_type=F32)
        merged = term if merged is None else merged + term
    x_new = x + jnp.dot(merged.astype(BF16), wo_ref[...], preferred_element_type=F32)
    xo_ref[...] = x_new
    h2 = _rms_rows(x_new, gf_ref[...])
    h2_ref[...] = h2.astype(h2_ref.dtype)
    if with_router:
        hprev_ref[...] = h2


def _merge(x2, branches, g, wg, bg, wb, wo, gf, w_router=None):
    n = x2.shape[0]
    tm = min(512, n)
    n_tiles = n // tm
    with_router = w_router is not None
    row = lambda w: pl.BlockSpec((tm, w), lambda i: (jnp.minimum(i, n_tiles - 1), 0))
    in_specs = ([row(D_MODEL)] + [row(512)] * 4
                + [_const_spec((1, D_MODEL)), _const_spec(wg.shape), _const_spec(bg.shape),
                   _const_spec(wb.shape), _const_spec(wo.shape), _const_spec((1, D_MODEL))])
    args = [x2, *branches, g, wg, bg, wb, wo, gf]
    out_specs = [row(D_MODEL), row(D_MODEL)]
    out_shape = [jax.ShapeDtypeStruct((n, D_MODEL), F32),
                 jax.ShapeDtypeStruct((n, D_MODEL), F32 if with_router else BF16)]
    scratch_shapes = []
    if with_router:
        in_specs.append(_const_spec(w_router.shape))
        args.append(w_router)
        out_specs.append(pl.BlockSpec((tm, LANES), lambda i: (jnp.maximum(i - 1, 0), 0)))
        out_shape.append(jax.ShapeDtypeStruct((n, LANES), F32))
        scratch_shapes.append(pltpu.VMEM((tm, D_MODEL), F32))
    return pl.pallas_call(
        functools.partial(_merge_body, with_router=with_router),
        name="merge_router" if with_router else "merge",
        grid=(n_tiles + 1 if with_router else n_tiles,),
        in_specs=in_specs, out_specs=out_specs, out_shape=out_shape, scratch_shapes=scratch_shapes,
        compiler_params=pltpu.CompilerParams(
            dimension_semantics=("arbitrary",) if with_router else ("parallel",),
            vmem_limit_bytes=VMEM_LIMIT),
    )(*args)


def _swiglu_chunk(xb, w1, w3, w2):
    a = jnp.dot(xb, w1, preferred_element_type=F32)
    b = jnp.dot(xb, w3, preferred_element_type=F32)
    return jnp.dot((jax.nn.silu(a) * b).astype(BF16), w2, preferred_element_type=F32)


def _ffn_body(x_ref, h_ref, w1_ref, w3_ref, w2_ref, o_ref):
    @pl.when(pl.program_id(1) == 0)
    def _():
        o_ref[...] = x_ref[...]

    o_ref[...] += _swiglu_chunk(h_ref[...], w1_ref[...], w3_ref[...], w2_ref[...])


def _dense_ffn(x2, h2, w1, w3, w2):
    n = x2.shape[0]
    tm = min(512, n)
    tf = 1792
    return pl.pallas_call(
        _ffn_body,
        name="dense_ffn",
        grid=(n // tm, FFN_DIM // tf),
        in_specs=[pl.BlockSpec((tm, D_MODEL), lambda i, f: (i, 0)),
                  pl.BlockSpec((tm, D_MODEL), lambda i, f: (i, 0)),
                  pl.BlockSpec((D_MODEL, tf), lambda i, f: (0, f)),
                  pl.BlockSpec((D_MODEL, tf), lambda i, f: (0, f)),
                  pl.BlockSpec((tf, D_MODEL), lambda i, f: (f, 0))],
        out_specs=pl.BlockSpec((tm, D_MODEL), lambda i, f: (i, 0)),
        out_shape=jax.ShapeDtypeStruct((n, D_MODEL), F32),
        compiler_params=pltpu.CompilerParams(dimension_semantics=("parallel", "arbitrary"),
                                             vmem_limit_bytes=VMEM_LIMIT),
    )(x2, h2, w1, w3, w2)


MOE_TF = 1792


def _moe_body(expert_ref, src_ref, dst_ref, h_hbm, w1_ref, w3_ref, w2_ref, y_hbm,
              xg_ref, xb_ref, acc_ref, gsem, ssem):
    j = pl.program_id(0)
    f = pl.program_id(1)
    n_f = pl.num_programs(1)
    slot = j % 2
    other = 1 - slot
    rows_per_step = MOE_ROWS // (FFN_DIM // MOE_TF)

    def gather_copy(block, r, buf):
        tok = src_ref[block * MOE_ROWS + r]
        return pltpu.make_async_copy(h_hbm.at[pl.ds(tok, 1), :], xg_ref.at[buf, pl.ds(r, 1), :],
                                     gsem.at[buf])

    def scatter_copy(block, r, buf):
        dst = dst_ref[(block + 1) * MOE_ROWS + r]
        return pltpu.make_async_copy(acc_ref.at[buf, pl.ds(r, 1), :], y_hbm.at[pl.ds(dst, 1), :], ssem)

    def wait_rows(copy):
        for _ in range(MOE_ROWS):
            copy.wait()

    @pl.when(jnp.logical_and(j == 0, f == 0))
    def _():
        def start(r, c):
            gather_copy(0, r, 0).start()
            return c
        lax.fori_loop(0, MOE_ROWS, start, 0)
        acc_ref[1] = jnp.zeros((MOE_ROWS, D_MODEL), F32)

    @pl.when(f == 0)
    def _():
        wait_rows(gather_copy(j, 0, slot))
        xb_ref[...] = xg_ref[slot].astype(BF16)

    for step in range(FFN_DIM // MOE_TF):
        @pl.when(f == step)
        def _():
            for r in range(step * rows_per_step, (step + 1) * rows_per_step):
                gather_copy(j + 1, r, other).start()
                scatter_copy(j - 1, r, other).start()

    @pl.when(f == 0)
    def _():
        acc_ref[slot] = jnp.zeros((MOE_ROWS, D_MODEL), F32)

    acc_ref[slot] += _swiglu_chunk(xb_ref[...], w1_ref[...], w3_ref[...], w2_ref[...])

    @pl.when(f == n_f - 1)
    def _():
        wait_rows(scatter_copy(j - 1, 0, other))

    @pl.when(jnp.logical_and(j == pl.num_programs(0) - 1, f == n_f - 1))
    def _():
        def start(r, c):
            scatter_copy(j, r, slot).start()
            return c
        lax.fori_loop(0, MOE_ROWS, start, 0)
        wait_rows(scatter_copy(j, 0, slot))
        wait_rows(gather_copy(j + 1, 0, other))


def _moe_ffn(h2, block_expert, slot_src, slot_dst, w1, w3, w2):
    n = h2.shape[0]
    n_blocks = block_expert.shape[0]
    wspec = lambda shape, imap: pl.BlockSpec(shape, imap)
    grid_spec = pltpu.PrefetchScalarGridSpec(
        num_scalar_prefetch=3,
        grid=(n_blocks, FFN_DIM // MOE_TF),
        in_specs=[pl.BlockSpec(memory_space=pl.ANY),
                  wspec((None, D_MODEL, MOE_TF), lambda j, f, e, s, d: (e[j], 0, f)),
                  wspec((None, D_MODEL, MOE_TF), lambda j, f, e, s, d: (e[j], 0, f)),
                  wspec((None, MOE_TF, D_MODEL), lambda j, f, e, s, d: (e[j], f, 0))],
        out_specs=pl.BlockSpec(memory_space=pl.ANY),
        scratch_shapes=[pltpu.VMEM((2, MOE_ROWS, D_MODEL), F32),
                        pltpu.VMEM((MOE_ROWS, D_MODEL), BF16),
                        pltpu.VMEM((2, MOE_ROWS, D_MODEL), F32),
                        pltpu.SemaphoreType.DMA((2,)),
                        pltpu.SemaphoreType.DMA(())],
    )
    return pl.pallas_call(
        _moe_body,
        name="moe_ffn",
        grid_spec=grid_spec,
        out_shape=jax.ShapeDtypeStruct((TOP_K * n + MOE_ROWS, D_MODEL), F32),
        compiler_params=pltpu.CompilerParams(dimension_semantics=("arbitrary", "arbitrary"),
                                             vmem_limit_bytes=VMEM_LIMIT),
    )(block_expert, slot_src, slot_dst, h2, w1, w3, w2)


def _combine_body(x_ref, y1_ref, y2_ref, route_ref, o_ref):
    g1 = route_ref[:, 2:3]
    g2 = route_ref[:, 3:4]
    o_ref[...] = x_ref[...] + (y1_ref[...] * g1 + y2_ref[...] * g2)


def _combine(x2, y, route):
    n = x2.shape[0]
    tm = min(512, n)
    row = lambda w: pl.BlockSpec((tm, w), lambda i: (i, 0))
    return pl.pallas_call(
        _combine_body,
        name="moe_combine",
        grid=(n // tm,),
        in_specs=[row(D_MODEL), row(D_MODEL), pl.BlockSpec((tm, D_MODEL), lambda i: (i + n // tm, 0)),
                  row(LANES)],
        out_specs=row(D_MODEL),
        out_shape=jax.ShapeDtypeStruct((n, D_MODEL), F32),
        compiler_params=pltpu.CompilerParams(dimension_semantics=("parallel",),
                                             vmem_limit_bytes=VMEM_LIMIT),
    )(x2, y, y, route)


def _invert_slots_body(slot_ref, init_hbm, out_ref, sem):
    copy = pltpu.make_async_copy(init_hbm, out_ref, sem)
    copy.start()
    copy.wait()

    def place(a, c):
        out_ref[slot_ref[a]] = a
        return c
    lax.fori_loop(0, slot_ref.shape[0], place, 0, unroll=8)


def _invert_slots(slot, init):
    smem = pl.BlockSpec(memory_space=pltpu.SMEM)
    return pl.pallas_call(
        _invert_slots_body,
        name="invert_slots",
        in_specs=[smem, pl.BlockSpec(memory_space=pl.ANY)],
        out_specs=smem,
        out_shape=jax.ShapeDtypeStruct(init.shape, jnp.int32),
        scratch_shapes=[pltpu.SemaphoreType.DMA(())],
    )(slot, init)


def _route_plan(route, n):
    n_assign = n * TOP_K
    flat_e = route[:, :TOP_K].astype(jnp.int32).reshape(-1)
    onehot = (flat_e[:, None] == jnp.arange(N_EXPERTS, dtype=jnp.int32)[None, :]).astype(jnp.int32)
    ranks = jnp.cumsum(onehot, axis=0) - onehot
    counts = jnp.sum(onehot, axis=0)
    rank = jnp.sum(ranks * onehot, axis=1)
    padded = (counts + MOE_ROWS - 1) // MOE_ROWS * MOE_ROWS
    pad_end = jnp.cumsum(padded)
    pad_start = pad_end - padded
    slot = pad_start[flat_e] + rank
    n_blocks = (n_assign + N_EXPERTS * (MOE_ROWS - 1)) // MOE_ROWS
    pad_ids = n_assign + jnp.arange(MOE_ROWS, dtype=jnp.int32)
    slot_assign = _invert_slots(slot, jnp.tile(pad_ids, n_blocks))
    valid = slot_assign < n_assign
    slot_src = jnp.where(valid, slot_assign // TOP_K, 0)
    slot_dst = jnp.where(valid, (slot_assign % TOP_K) * n + slot_assign // TOP_K, slot_assign)
    block_start = jnp.arange(n_blocks, dtype=jnp.int32) * MOE_ROWS
    block_expert = jnp.minimum(jnp.searchsorted(pad_end, block_start, side='right'),
                               N_EXPERTS - 1).astype(jnp.int32)
    return (block_expert, jnp.concatenate([slot_src, jnp.zeros((MOE_ROWS,), jnp.int32)]),
            jnp.concatenate([pad_ids, slot_dst]))


def _inv_freq(dim):
    return 1.0 / (ROPE_THETA ** (jnp.arange(0, dim, 2, dtype=F32) / dim))


def _rope_table(pos_of_lane, freq_of_lane, first_half, active=None):
    ang = pos_of_lane * freq_of_lane[None, :]
    cos, sin = jnp.cos(ang), jnp.sin(ang) * jnp.where(first_half, -1.0, 1.0)[None, :]
    if active is not None:
        cos = jnp.where(active[None, :], cos, 1.0)
        sin = jnp.where(active[None, :], sin, 0.0)
    return jnp.concatenate([cos, sin], axis=1).astype(F32)


def _rope_tables(seq, mem_len):
    lane = np.arange(LANES)
    pos = jnp.arange(seq, dtype=jnp.int32)
    posf = jnp.broadcast_to(pos.astype(F32)[:, None], (seq, LANES))
    tab_a = _rope_table(posf, _inv_freq(64)[lane % 32], (lane % 64) < 32)
    tab_b = _rope_table(posf, _inv_freq(32)[lane % 16], (lane % 32) < 16, (lane >= 64) & (lane < 96))
    rows = (pos // MEM_GRID_W).astype(F32)[:, None]
    cols = (pos % MEM_GRID_W).astype(F32)[:, None]
    pos_c = jnp.where(((lane % 64) < 32)[None, :], rows, cols)
    tab_c = _rope_table(pos_c, _inv_freq(32)[lane % 16], (lane % 32) < 16)
    f_d = _inv_freq(128)[lane % 64]
    tab_d = _rope_table(posf, f_d, lane < 64)
    mposf = jnp.broadcast_to(jnp.arange(mem_len, dtype=F32)[:, None], (mem_len, LANES))
    tab_m = _rope_table(mposf, f_d, lane < 64)
    return jnp.concatenate([tab_a, tab_b, tab_c, tab_d], axis=1), tab_m


def _segment_matrix(width):
    i = np.arange(MXU_DIM)
    return jnp.asarray((i[:, None] // width) == (i[None, :] // width), dtype=BF16)


def _tile_row(g, reps):
    return jnp.tile(g.astype(F32), reps).reshape(1, -1)


def _layer_params(layer, w_in, w_mla_q_up, w_mla_kv_up, w_mem_kv, gains):
    w = w_in[layer]
    z = lambda c: jnp.zeros((D_MODEL, c), F32)
    gk, gv = w[:, 2464:2592], w[:, 2592:2720]
    dup = lambda m: jnp.concatenate([m[:, :64], m[:, :64], m[:, 64:], m[:, 64:]], axis=1)
    w_big = jnp.concatenate([
        w[:, 0:1536], w[:, 1536:1792], w[:, 1792:1920],
        z(64), w[:, 1920:1952], z(32),
        w[:, 1952:2464], dup(gk), dup(gv), w[:, 2720:3232]], axis=1)
    wq = w_mla_q_up[layer].reshape(256, MLA_HEADS, MLA_QK_DIM)
    wq = jnp.pad(wq, ((0, 0), (0, 0), (0, LANES - MLA_QK_DIM))).reshape(256, MLA_HEADS * LANES)
    wkv = w_mla_kv_up[layer].reshape(128, MLA_HEADS, 128)
    wk = jnp.pad(wkv[:, :, :64], ((0, 0), (0, 0), (0, 64))).reshape(128, MLA_HEADS * LANES)
    wv = wkv[:, :, 64:].reshape(128, MLA_HEADS * 64)
    wm = w_mem_kv[layer].reshape(D_MODEL, 4, 256)
    w_mem = jnp.concatenate([wm[:, :, :128].reshape(D_MODEL, 512), wm[:, :, 128:].reshape(D_MODEL, 512)], axis=1)
    pad96 = lambda g: jnp.pad(g.astype(F32), (0, LANES - MLA_QK_DIM)).reshape(1, LANES)
    (dq_g, dk_g, ql_g, kvl_g, mq_g, mk_g, gq_g, gk_g, memq_g) = gains
    gain_rows = jnp.concatenate([
        _tile_row(dq_g, 2), _tile_row(dk_g, 2), ql_g.astype(F32).reshape(2, LANES),
        kvl_g.astype(F32).reshape(1, LANES), pad96(mq_g), pad96(mk_g),
        _tile_row(gq_g, 2), _tile_row(gk_g, 2), memq_g.astype(F32).reshape(1, LANES),
        jnp.zeros((6, LANES), F32)], axis=0)
    return (w_big.astype(BF16), wq.astype(BF16), jnp.concatenate([wk, wv], axis=1).astype(BF16),
            w_mem.astype(BF16), gain_rows)


def kernel(x, mem, norm_attn_g, norm_mem_g, w_in, diff_q_norm_g, diff_k_norm_g, diff_lambda, diff_subln_g,
           mla_q_lat_g, mla_kv_lat_g, w_mla_q_up, w_mla_kv_up, mla_q_norm_g, mla_k_norm_g, gqa_q_norm_g,
           gqa_k_norm_g, w_mem_kv, mem_q_norm_g, mem_k_norm_g, w_branch, w_gate, b_gate, w_out, norm_ffn_g,
           dense_w1, dense_w3, dense_w2, moe_router, moe_w1, moe_w3, moe_w2):
    batch, seq, _ = x.shape
    mem_len = mem.shape[1]
    n = batch * seq
    depth = w_in.shape[0]
    tab, tab_m = _rope_tables(seq, mem_len)
    seg64, seg128 = _segment_matrix(64), _segment_matrix(128)
    x2 = x.reshape(n, D_MODEL)
    mem2 = mem.reshape(batch * mem_len, D_MODEL)
    row = lambda g: g.astype(F32).reshape(1, -1)
    b3 = lambda a, w: a.reshape(batch, -1, w)

    for layer in range(depth):
        lambda_init = 0.8 - 0.6 * math.exp(-0.3 * layer)
        w_big, wq_up, wkv_up, w_mem, gain_rows = _layer_params(
            layer, w_in, w_mla_q_up, w_mla_kv_up, w_mem_kv,
            (diff_q_norm_g[layer], diff_k_norm_g[layer], mla_q_lat_g[layer], mla_kv_lat_g[layer],
             mla_q_norm_g[layer], mla_k_norm_g[layer], gqa_q_norm_g[layer], gqa_k_norm_g[layer],
             mem_q_norm_g[layer]))
        qa, ka, va, qb, kb, vb, qc, kc, vc, qd = _inproj(
            x2, row(norm_attn_g[layer]), w_big, wq_up, wkv_up, seg64, seg128, gain_rows, tab, seq)
        kd, vd = _memprep(mem2, row(norm_mem_g[layer]), w_mem, seg128, row(mem_k_norm_g[layer]), tab_m, mem_len)

        o_a = _attention("attn_diff", [(0, 0, 0, 0), (0, 1, 0, 0), (128, 0, 128, 128), (128, 1, 128, 128)],
                         functools.partial(_finish_diff, lambda_init=lambda_init),
                         b3(qa, 512), b3(ka, 512), b3(va, 512), groups=2, q_width=256,
                         k_width=256, v_width=256, o_width=256, tq=512,
                         extra=(diff_lambda[layer].astype(F32), row(diff_subln_g[layer])))
        o_b = _attention("attn_latent", [(128 * u, None, 128 * u, 128 * (u // 2)) for u in range(4)],
                         _finish_pairs, b3(qb, 1024), b3(kb, 1024), b3(vb, 512), groups=2,
                         q_width=512, k_width=512, v_width=256, o_width=256, tq=512, sums_on_mxu=True)
        o_c = _attention("attn_gqa", [(0, 0, 0, 0), (0, 1, 0, 0), (128, 0, 0, 0), (128, 1, 0, 0)],
                         _finish_pairs, b3(qc, 512), b3(kc, 256), b3(vc, 256), groups=2,
                         q_width=256, k_width=128, v_width=128, o_width=256, tq=512, sums_on_mxu=True)
        o_d = _attention("attn_mem", [(128 * u, None, 128 * u, 128 * u) for u in range(4)],
                         _finish_tiles, b3(qd, 512), b3(kd, 512), b3(vd, 512), groups=1,
                         q_width=512, k_width=512, v_width=512, o_width=512, tq=512)
        branches = [o.reshape(n, 512) for o in (o_a, o_b, o_c, o_d)]

        i = layer // 2
        routed = layer % 2 == 1
        w_router = None
        if routed:
            w_r = jnp.pad(moe_router[i].astype(F32), ((0, 0), (0, LANES - N_EXPERTS)))
            w_r_hi = w_r.astype(BF16)
            w_router = jnp.stack([w_r_hi, (w_r - w_r_hi.astype(F32)).astype(BF16)])
        merged = _merge(x2, branches, row(norm_attn_g[layer]), w_gate[layer].astype(BF16),
                        b_gate[layer].astype(F32).reshape(4, 1, D_MODEL), w_branch[layer].astype(BF16),
                        w_out[layer].astype(BF16), row(norm_ffn_g[layer]), w_router)
        if not routed:
            x_mid, h2 = merged
            x2 = _dense_ffn(x_mid, h2, dense_w1[i].astype(BF16), dense_w3[i].astype(BF16),
                            dense_w2[i].astype(BF16))
        else:
            x_mid, h2, route = merged
            block_expert, slot_src, slot_dst = _route_plan(route, n)
            y = _moe_ffn(h2, block_expert, slot_src, slot_dst, moe_w1[i].astype(BF16),
                         moe_w3[i].astype(BF16), moe_w2[i].astype(BF16))
            x2 = _combine(x_mid, y, route)
    return x2.reshape(batch, seq, D_MODEL)
```

```python
import functools
import math

import numpy as np
import jax
import jax.numpy as jnp
from jax import lax
from jax.experimental import pallas as pl
from jax.experimental.pallas import tpu as pltpu

F32 = jnp.float32
BF16 = jnp.bfloat16

D_MODEL = 1024
MEM_GRID_W = 64
ROPE_THETA = 10000.0
NORM_EPS = 1e-6
LOG2_E = math.log2(math.e)
DIFF_HEADS = 4
MLA_HEADS = 8
MLA_QK_DIM = 96
FFN_DIM = 3584
N_EXPERTS = 8
TOP_K = 2
MOE_ROWS = 512
LANES = 128
MXU_DIM = 256
VMEM_LIMIT = 56 * 1024 * 1024

C_DQ, C_DK, C_DV = 0, 512, 1024
C_MQL, C_MKVL, C_KROPE = 1536, 1792, 1920
C_GQ, C_GK, C_GV, C_MEMQ = 2048, 2560, 2816, 3072
IN_COLS_PADDED = 3584
T_A, T_B, T_C, T_D = 0, 256, 512, 768


def _const_spec(shape):
    return pl.BlockSpec(shape, lambda *_: (0,) * len(shape), pipeline_mode=pl.Buffered(1))


def _rms_rows(x, g):
    ms = jnp.mean(x * x, axis=-1, keepdims=True)
    return x * lax.rsqrt(ms + NORM_EPS) * g


def _segment_mean_sq(y, seg_ref, inv_count):
    sq = (y * y).astype(BF16)
    parts = [jnp.dot(sq[:, c:c + MXU_DIM], seg_ref[...], preferred_element_type=F32)
             for c in range(0, y.shape[1], MXU_DIM)]
    ss = parts[0] if len(parts) == 1 else jnp.concatenate(parts, axis=1)
    return ss * inv_count


def _rope_lanes(y, cos, sin_signed, half):
    lane = lax.broadcasted_iota(jnp.int32, y.shape, 1)
    first = (lane % (2 * half)) < half
    rot = jnp.where(first, pltpu.roll(y, LANES - half, 1), pltpu.roll(y, half, 1))
    return y * cos + rot * sin_signed


def _norm_rope_store(y, ms, gain, tab_ref, t_off, half, scale, out_ref):
    yn = y * lax.rsqrt(ms + NORM_EPS)
    cos = tab_ref[:, t_off:t_off + LANES]
    sin = tab_ref[:, t_off + LANES:t_off + 2 * LANES]
    for c in range(0, y.shape[1], LANES):
        r = _rope_lanes(yn[:, c:c + LANES] * gain, cos, sin, half)
        if scale != 1.0:
            r = r * scale
        out_ref[:, c:c + LANES] = r.astype(out_ref.dtype)


def _inproj_body(x_ref, g_ref, w_ref, wq_ref, wkv_ref, seg64_ref, seg128_ref, gains_ref, tab_ref,
                 qa_ref, ka_ref, va_ref, qb_ref, kb_ref, vb_ref, qc_ref, kc_ref, vc_ref, qd_ref):
    h = _rms_rows(x_ref[...], g_ref[...]).astype(BF16)

    def proj(c0, width):
        return jnp.dot(h, w_ref[:, c0:c0 + width], preferred_element_type=F32)

    def gain(row):
        return gains_ref[row:row + 1, :]

    y = proj(C_DQ, 512)
    _norm_rope_store(y, _segment_mean_sq(y, seg64_ref, 1.0 / 64), gain(0), tab_ref, T_A, 32,
                     LOG2_E * 64 ** -0.5, qa_ref)
    y = proj(C_DK, 512)
    _norm_rope_store(y, _segment_mean_sq(y, seg64_ref, 1.0 / 64), gain(1), tab_ref, T_A, 32, 1.0, ka_ref)
    va_ref[...] = proj(C_DV, 512).astype(BF16)

    g_ql = jnp.concatenate([gain(2), gain(3)], axis=1)
    ql = _rms_rows(proj(C_MQL, 256), g_ql).astype(BF16)
    y = jnp.dot(ql, wq_ref[...], preferred_element_type=F32)
    _norm_rope_store(y, _segment_mean_sq(y, seg128_ref, 1.0 / MLA_QK_DIM), gain(5), tab_ref, T_B, 16,
                     LOG2_E * MLA_QK_DIM ** -0.5, qb_ref)
    kvl = _rms_rows(proj(C_MKVL, 128), gain(4)).astype(BF16)
    kv = jnp.dot(kvl, wkv_ref[...], preferred_element_type=F32)
    k_rope = proj(C_KROPE, 128)
    y = kv[:, :1024] + jnp.concatenate([k_rope] * MLA_HEADS, axis=1)
    _norm_rope_store(y, _segment_mean_sq(y, seg128_ref, 1.0 / MLA_QK_DIM), gain(6), tab_ref, T_B, 16,
                     1.0, kb_ref)
    vb_ref[...] = kv[:, 1024:].astype(BF16)

    y = proj(C_GQ, 512)
    _norm_rope_store(y, _segment_mean_sq(y, seg64_ref, 1.0 / 64), gain(7), tab_ref, T_C, 16,
                     LOG2_E * 64 ** -0.5, qc_ref)
    y = proj(C_GK, 256)
    _norm_rope_store(y, _segment_mean_sq(y, seg64_ref, 1.0 / 64), gain(8), tab_ref, T_C, 16, 1.0, kc_ref)
    vc_ref[...] = proj(C_GV, 256).astype(BF16)

    y = proj(C_MEMQ, 512)
    _norm_rope_store(y, _segment_mean_sq(y, seg128_ref, 1.0 / 128), gain(9), tab_ref, T_D, 64,
                     LOG2_E * 128 ** -0.5, qd_ref)


def _inproj(x2, g, w_big, wq_up, wkv_up, seg64, seg128, gains, tab, seq):
    n = x2.shape[0]
    tm = min(512, seq)
    tiles_per_seq = seq // tm
    row = lambda w: pl.BlockSpec((tm, w), lambda i: (i, 0))
    out_widths = (512, 512, 512, 1024, 1024, 512, 512, 256, 256, 512)
    return pl.pallas_call(
        _inproj_body,
        name="inproj",
        grid=(n // tm,),
        in_specs=[row(D_MODEL), _const_spec((1, D_MODEL)), _const_spec(w_big.shape),
                  _const_spec(wq_up.shape), _const_spec(wkv_up.shape), _const_spec(seg64.shape),
                  _const_spec(seg128.shape), _const_spec(gains.shape),
                  pl.BlockSpec((tm, tab.shape[1]), lambda i: (i % tiles_per_seq, 0))],
        out_specs=[row(w) for w in out_widths],
        out_shape=[jax.ShapeDtypeStruct((n, w), BF16) for w in out_widths],
        compiler_params=pltpu.CompilerParams(dimension_semantics=("parallel",),
                                             vmem_limit_bytes=VMEM_LIMIT),
    )(x2, g, w_big, wq_up, wkv_up, seg64, seg128, gains, tab)


def _memprep_body(m_ref, g_ref, w_ref, seg128_ref, gain_ref, tab_ref, k_ref, v_ref):
    mn = _rms_rows(m_ref[...], g_ref[...]).astype(BF16)
    kv = jnp.dot(mn, w_ref[...], preferred_element_type=F32)
    y = kv[:, :512]
    _norm_rope_store(y, _segment_mean_sq(y, seg128_ref, 1.0 / 128), gain_ref[...], tab_ref, 0, 64, 1.0, k_ref)
    v_ref[...] = kv[:, 512:].astype(BF16)


def _memprep(mem2, g, w_kv, seg128, gain, tab_m, mem_len):
    n = mem2.shape[0]
    row = lambda w: pl.BlockSpec((mem_len, w), lambda i: (i, 0))
    return pl.pallas_call(
        _memprep_body,
        name="memprep",
        grid=(n // mem_len,),
        in_specs=[row(D_MODEL), _const_spec((1, D_MODEL)), _const_spec(w_kv.shape),
                  _const_spec(seg128.shape), _const_spec((1, LANES)), _const_spec(tab_m.shape)],
        out_specs=[row(512), row(512)],
        out_shape=[jax.ShapeDtypeStruct((n, 512), BF16)] * 2,
        compiler_params=pltpu.CompilerParams(dimension_semantics=("parallel",),
                                             vmem_limit_bytes=VMEM_LIMIT),
    )(mem2, g, w_kv, seg128, gain, tab_m)


SAFE_SHIFT_LIMIT = 60.0


def _attend(q, k, v, sums_on_mxu, shift=None):
    s = lax.dot_general(q, k, (((1,), (1,)), ((), ())), preferred_element_type=F32)
    e = jnp.exp2(s - (jnp.max(s, axis=-1, keepdims=True) if shift is None else shift))
    if not sums_on_mxu:
        l = jnp.sum(e, axis=-1, keepdims=True)
        return jnp.dot(e.astype(BF16), v, preferred_element_type=F32) * (1.0 / l)
    v_ones = jnp.concatenate([v, jnp.ones((v.shape[0], LANES), BF16)], axis=1)
    ov = jnp.dot(e.astype(BF16), v_ones, preferred_element_type=F32)
    return ov[:, :LANES] * (1.0 / ov[:, LANES:LANES + 1])


def _split_halves(q):
    lo = lax.broadcasted_iota(jnp.int32, q.shape, 1) < 64
    zero = jnp.zeros_like(q)
    return jnp.where(lo, q, zero), jnp.where(lo, zero, q)


def _join_halves(o_lo, o_hi):
    lo = lax.broadcasted_iota(jnp.int32, o_lo.shape, 1) < 64
    return jnp.where(lo, o_lo, o_hi)


def _attn_body(shift_ref, *refs, units, finish, n_extra, sums_on_mxu):
    extra = refs[:n_extra]
    q_ref, k_ref, v_ref, o_ref = refs[n_extra:]
    shift = shift_ref[0]

    def run(shift):
        halves = {}
        outs = []
        for q_lane, half, k_lane, v_lane in units:
            q = q_ref[:, q_lane:q_lane + LANES]
            if half is not None:
                if q_lane not in halves:
                    halves[q_lane] = _split_halves(q)
                q = halves[q_lane][half]
            outs.append(_attend(q, k_ref[:, k_lane:k_lane + LANES], v_ref[:, v_lane:v_lane + LANES],
                                sums_on_mxu, shift))
        finish(outs, o_ref, *extra)

    @pl.when(shift <= SAFE_SHIFT_LIMIT)
    def _():
        run(shift)

    @pl.when(jnp.logical_not(shift <= SAFE_SHIFT_LIMIT))
    def _():
        run(None)


def _finish_pairs(outs, o_ref):
    for p in range(len(outs) // 2):
        o_ref[:, p * LANES:(p + 1) * LANES] = _join_halves(outs[2 * p], outs[2 * p + 1]).astype(o_ref.dtype)


def _finish_tiles(outs, o_ref):
    for p, o in enumerate(outs):
        o_ref[:, p * LANES:(p + 1) * LANES] = o.astype(o_ref.dtype)


def _finish_diff(outs, o_ref, lam_ref, g_ref, *, lambda_init):
    lp = lam_ref[...]
    lam = (jnp.exp(jnp.sum(lp[0:1] * lp[1:2], keepdims=True))
           - jnp.exp(jnp.sum(lp[2:3] * lp[3:4], keepdims=True)) + lambda_init)
    for p in range(len(outs) // 2):
        o = outs[2 * p] - outs[2 * p + 1] * lam
        o_ref[:, p * LANES:(p + 1) * LANES] = (
            _rms_rows(o, g_ref[...]) * (1.0 - lambda_init)).astype(o_ref.dtype)


def _attention(name, units, finish, shift, q, k, v, *, groups, q_width, k_width, v_width, o_width, tq,
               sums_on_mxu=False, extra=()):
    b, s, _ = q.shape
    kv_len = k.shape[1]
    tq = min(tq, s)
    grid_spec = pltpu.PrefetchScalarGridSpec(
        num_scalar_prefetch=1,
        grid=(b, groups, s // tq),
        in_specs=[_const_spec(e.shape) for e in extra] + [
            pl.BlockSpec((None, tq, q_width), lambda bi, gi, qi, _: (bi, qi, gi)),
            pl.BlockSpec((None, kv_len, k_width), lambda bi, gi, qi, _: (bi, 0, gi)),
            pl.BlockSpec((None, kv_len, v_width), lambda bi, gi, qi, _: (bi, 0, gi)),
        ],
        out_specs=pl.BlockSpec((None, tq, o_width), lambda bi, gi, qi, _: (bi, qi, gi)),
    )
    return pl.pallas_call(
        functools.partial(_attn_body, units=units, finish=finish, n_extra=len(extra),
                          sums_on_mxu=sums_on_mxu),
        name=name,
        grid_spec=grid_spec,
        out_shape=jax.ShapeDtypeStruct((b, s, groups * o_width), BF16),
        compiler_params=pltpu.CompilerParams(
            dimension_semantics=("parallel", "parallel", "arbitrary"), vmem_limit_bytes=VMEM_LIMIT),
    )(shift, *extra, q, k, v)


def _route_rows(h2, wr_ref, route_ref):
    h_hi = h2.astype(BF16)
    h_lo = (h2 - h_hi.astype(F32)).astype(BF16)
    logits = (jnp.dot(h_hi, wr_ref[0], preferred_element_type=F32)
              + (jnp.dot(h_hi, wr_ref[1], preferred_element_type=F32)
                 + jnp.dot(h_lo, wr_ref[0], preferred_element_type=F32)))
    l1 = logits.T[:N_EXPERTS, :]
    row = lax.broadcasted_iota(jnp.int32, l1.shape, 0)
    neg = jnp.float32(-jnp.inf)
    m1 = jnp.max(l1, axis=0, keepdims=True)
    i1 = jnp.min(jnp.where(l1 == m1, row, N_EXPERTS), axis=0, keepdims=True)
    l2 = jnp.where(row == i1, neg, l1)
    m2 = jnp.max(l2, axis=0, keepdims=True)
    i2 = jnp.min(jnp.where(l2 == m2, row, N_EXPERTS), axis=0, keepdims=True)
    e2 = jnp.exp(m2 - m1)
    g1 = 1.0 / (1.0 + e2)
    g2 = e2 / (1.0 + e2)
    route_t = jnp.where(row == 0, i1.astype(F32),
                        jnp.where(row == 1, i2.astype(F32),
                                  jnp.where(row == 2, g1, jnp.where(row == 3, g2, 0.0))))
    pad = jnp.zeros((LANES - N_EXPERTS, route_t.shape[1]), F32)
    route_ref[...] = jnp.concatenate([route_t, pad], axis=0).T


def _merge_body(*refs, with_router):
    if with_router:
        (x_ref, oa_ref, ob_ref, oc_ref, od_ref, g_ref, wg_ref, bg_ref, wb_ref, wo_ref, gf_ref, wr_ref,
         xo_ref, h2_ref, route_ref, hprev_ref) = refs

        @pl.when(pl.program_id(0) == 0)
        def _():
            hprev_ref[...] = jnp.zeros(hprev_ref.shape, F32)

        _route_rows(hprev_ref[...], wr_ref, route_ref)
    else:
        (x_ref, oa_ref, ob_ref, oc_ref, od_ref, g_ref, wg_ref, bg_ref, wb_ref, wo_ref, gf_ref,
         xo_ref, h2_ref) = refs
    x = x_ref[...]
    h = _rms_rows(x, g_ref[...]).astype(BF16)
    merged = None
    for b, o_ref in enumerate((oa_ref, ob_ref, oc_ref, od_ref)):
        gate = jax.nn.sigmoid(jnp.dot(h, wg_ref[b], preferred_element_type=F32) + bg_ref[b])
        term = gate * jnp.dot(o_ref[...], wb_ref[b], preferred_element_type=F32)
        merged = term if merged is None else merged + term
    x_new = x + jnp.dot(merged.astype(BF16), wo_ref[...], preferred_element_type=F32)
    xo_ref[...] = x_new
    h2 = _rms_rows(x_new, gf_ref[...])
    if with_router:
        _tile_rows_store(h2_ref, h2)
        hprev_ref[...] = h2
    else:
        h2_ref[...] = h2.astype(h2_ref.dtype)


def _merge(x2, branches, g, wg, bg, wb, wo, gf, w_router=None):
    n = x2.shape[0]
    tm = min(512, n)
    n_tiles = n // tm
    with_router = w_router is not None
    row = lambda w: pl.BlockSpec((tm, w), lambda i: (jnp.minimum(i, n_tiles - 1), 0))
    in_specs = ([row(D_MODEL)] + [row(512)] * 4
                + [_const_spec((1, D_MODEL)), _const_spec(wg.shape), _const_spec(bg.shape),
                   _const_spec(wb.shape), _const_spec(wo.shape), _const_spec((1, D_MODEL))])
    args = [x2, *branches, g, wg, bg, wb, wo, gf]
    out_specs = [row(D_MODEL), row(D_MODEL)]
    out_shape = [jax.ShapeDtypeStruct((n, D_MODEL), F32), jax.ShapeDtypeStruct((n, D_MODEL), BF16)]
    scratch_shapes = []
    if with_router:
        out_specs[1] = pl.BlockSpec((tm * ROW_TILE, LANES), lambda i: (jnp.minimum(i, n_tiles - 1), 0))
        out_shape[1] = jax.ShapeDtypeStruct((n * ROW_TILE, LANES), F32)
        in_specs.append(_const_spec(w_router.shape))
        args.append(w_router)
        out_specs.append(pl.BlockSpec((tm, LANES), lambda i: (jnp.maximum(i - 1, 0), 0)))
        out_shape.append(jax.ShapeDtypeStruct((n, LANES), F32))
        scratch_shapes.append(pltpu.VMEM((tm, D_MODEL), F32))
    return pl.pallas_call(
        functools.partial(_merge_body, with_router=with_router),
        name="merge_router" if with_router else "merge",
        grid=(n_tiles + 1 if with_router else n_tiles,),
        in_specs=in_specs, out_specs=out_specs, out_shape=out_shape, scratch_shapes=scratch_shapes,
        compiler_params=pltpu.CompilerParams(
            dimension_semantics=("arbitrary",) if with_router else ("parallel",),
            vmem_limit_bytes=VMEM_LIMIT),
    )(*args)


def _swiglu_chunk(xb, w1, w3, w2):
    a = jnp.dot(xb, w1, preferred_element_type=F32)
    b = jnp.dot(xb, w3, preferred_element_type=F32)
    return jnp.dot((jax.nn.silu(a) * b).astype(BF16), w2, preferred_element_type=F32)


def _ffn_body(x_ref, h_ref, w1_ref, w3_ref, w2_ref, o_ref):
    @pl.when(pl.program_id(1) == 0)
    def _():
        o_ref[...] = x_ref[...]

    o_ref[...] += _swiglu_chunk(h_ref[...], w1_ref[...], w3_ref[...], w2_ref[...])


def _dense_ffn(x2, h2, w1, w3, w2):
    n = x2.shape[0]
    tm = min(512, n)
    tf = 1792
    return pl.pallas_call(
        _ffn_body,
        name="dense_ffn",
        grid=(n // tm, FFN_DIM // tf),
        in_specs=[pl.BlockSpec((tm, D_MODEL), lambda i, f: (i, 0)),
                  pl.BlockSpec((tm, D_MODEL), lambda i, f: (i, 0)),
                  pl.BlockSpec((D_MODEL, tf), lambda i, f: (0, f)),
                  pl.BlockSpec((D_MODEL, tf), lambda i, f: (0, f)),
                  pl.BlockSpec((tf, D_MODEL), lambda i, f: (f, 0))],
        out_specs=pl.BlockSpec((tm, D_MODEL), lambda i, f: (i, 0)),
        out_shape=jax.ShapeDtypeStruct((n, D_MODEL), F32),
        compiler_params=pltpu.CompilerParams(dimension_semantics=("parallel", "arbitrary"),
                                             vmem_limit_bytes=VMEM_LIMIT),
    )(x2, h2, w1, w3, w2)


MOE_TF = 1792


ROW_TILE = D_MODEL // LANES


def _tile_rows_load(ref, rows):
    return jnp.concatenate([ref[pl.ds(s, rows, stride=ROW_TILE), :] for s in range(ROW_TILE)], axis=1)


def _tile_rows_store(ref, x):
    for s in range(ROW_TILE):
        ref[pl.ds(s, x.shape[0], stride=ROW_TILE), :] = x[:, s * LANES:(s + 1) * LANES]


def _moe_body(expert_ref, src_ref, dst_ref, h_hbm, w1_ref, w3_ref, w2_ref, y_hbm,
              xg_ref, xb_ref, acc_ref, stage_ref, gsem, ssem):
    j = pl.program_id(0)
    f = pl.program_id(1)
    n_f = pl.num_programs(1)
    slot = j % 2
    other = 1 - slot
    rows_per_step = MOE_ROWS // (FFN_DIM // MOE_TF)

    def tile_of(r):
        first = r * ROW_TILE
        return pl.ds(first if isinstance(r, int) else pl.multiple_of(first, ROW_TILE), ROW_TILE)

    def gather_copy(block, r, buf):
        src = pl.multiple_of(src_ref[block * MOE_ROWS + r], ROW_TILE)
        return pltpu.make_async_copy(h_hbm.at[pl.ds(src, ROW_TILE), :], xg_ref.at[buf, tile_of(r), :],
                                     gsem.at[buf])

    def scatter_copy(block, r, buf):
        dst = pl.multiple_of(dst_ref[(block + 1) * MOE_ROWS + r], ROW_TILE)
        return pltpu.make_async_copy(stage_ref.at[buf, tile_of(r), :], y_hbm.at[pl.ds(dst, ROW_TILE), :],
                                     ssem)

    def wait_rows(copy):
        for _ in range(MOE_ROWS):
            copy.wait()

    @pl.when(jnp.logical_and(j == 0, f == 0))
    def _():
        def start(r, c):
            gather_copy(0, r, 0).start()
            return c
        lax.fori_loop(0, MOE_ROWS, start, 0)
        stage_ref[1] = jnp.zeros(stage_ref.shape[1:], F32)

    @pl.when(f == 0)
    def _():
        wait_rows(gather_copy(j, 0, slot))
        xb_ref[...] = _tile_rows_load(xg_ref.at[slot], MOE_ROWS).astype(BF16)

    for step in range(FFN_DIM // MOE_TF):
        @pl.when(f == step)
        def _():
            for r in range(step * rows_per_step, (step + 1) * rows_per_step):
                gather_copy(j + 1, r, other).start()
                scatter_copy(j - 1, r, other).start()

    @pl.when(f == 0)
    def _():
        acc_ref[...] = jnp.zeros((MOE_ROWS, D_MODEL), F32)

    acc_ref[...] += _swiglu_chunk(xb_ref[...], w1_ref[...], w3_ref[...], w2_ref[...])

    @pl.when(f == n_f - 1)
    def _():
        wait_rows(scatter_copy(j - 1, 0, other))
        _tile_rows_store(stage_ref.at[slot], acc_ref[...])

    @pl.when(jnp.logical_and(j == pl.num_programs(0) - 1, f == n_f - 1))
    def _():
        def start(r, c):
            scatter_copy(j, r, slot).start()
            return c
        lax.fori_loop(0, MOE_ROWS, start, 0)
        wait_rows(scatter_copy(j, 0, slot))
        wait_rows(gather_copy(j + 1, 0, other))


def _moe_ffn(h2_tiles, block_expert, slot_src, slot_dst, w1, w3, w2):
    n = h2_tiles.shape[0] // ROW_TILE
    n_blocks = block_expert.shape[0]
    wspec = lambda shape, imap: pl.BlockSpec(shape, imap)
    grid_spec = pltpu.PrefetchScalarGridSpec(
        num_scalar_prefetch=3,
        grid=(n_blocks, FFN_DIM // MOE_TF),
        in_specs=[pl.BlockSpec(memory_space=pl.ANY),
                  wspec((None, D_MODEL, MOE_TF), lambda j, f, e, s, d: (e[j], 0, f)),
                  wspec((None, D_MODEL, MOE_TF), lambda j, f, e, s, d: (e[j], 0, f)),
                  wspec((None, MOE_TF, D_MODEL), lambda j, f, e, s, d: (e[j], f, 0))],
        out_specs=pl.BlockSpec(memory_space=pl.ANY),
        scratch_shapes=[pltpu.VMEM((2, MOE_ROWS * ROW_TILE, LANES), F32),
                        pltpu.VMEM((MOE_ROWS, D_MODEL), BF16),
                        pltpu.VMEM((MOE_ROWS, D_MODEL), F32),
                        pltpu.VMEM((2, MOE_ROWS * ROW_TILE, LANES), F32),
                        pltpu.SemaphoreType.DMA((2,)),
                        pltpu.SemaphoreType.DMA(())],
    )
    return pl.pallas_call(
        _moe_body,
        name="moe_ffn",
        grid_spec=grid_spec,
        out_shape=jax.ShapeDtypeStruct(((TOP_K * n + MOE_ROWS) * ROW_TILE, LANES), F32),
        compiler_params=pltpu.CompilerParams(dimension_semantics=("arbitrary", "arbitrary"),
                                             vmem_limit_bytes=VMEM_LIMIT),
    )(block_expert, slot_src, slot_dst, h2_tiles, w1, w3, w2)


def _combine_body(x_ref, y1_ref, y2_ref, route_ref, o_ref):
    g1 = route_ref[:, 2:3]
    g2 = route_ref[:, 3:4]
    rows = x_ref.shape[0]
    o_ref[...] = x_ref[...] + (_tile_rows_load(y1_ref, rows) * g1 + _tile_rows_load(y2_ref, rows) * g2)


def _combine(x2, y, route):
    n = x2.shape[0]
    tm = min(512, n)
    row = lambda w: pl.BlockSpec((tm, w), lambda i: (i, 0))
    return pl.pallas_call(
        _combine_body,
        name="moe_combine",
        grid=(n // tm,),
        in_specs=[row(D_MODEL), pl.BlockSpec((tm * ROW_TILE, LANES), lambda i: (i, 0)),
                  pl.BlockSpec((tm * ROW_TILE, LANES), lambda i: (i + n // tm, 0)),
                  row(LANES)],
        out_specs=row(D_MODEL),
        out_shape=jax.ShapeDtypeStruct((n, D_MODEL), F32),
        compiler_params=pltpu.CompilerParams(dimension_semantics=("parallel",),
                                             vmem_limit_bytes=VMEM_LIMIT),
    )(x2, y, y, route)


def _invert_slots_body(slot_ref, init_hbm, out_ref, sem):
    copy = pltpu.make_async_copy(init_hbm, out_ref, sem)
    copy.start()
    copy.wait()

    def place(a, c):
        out_ref[slot_ref[a]] = a
        return c
    lax.fori_loop(0, slot_ref.shape[0], place, 0, unroll=8)


def _invert_slots(slot, init):
    smem = pl.BlockSpec(memory_space=pltpu.SMEM)
    return pl.pallas_call(
        _invert_slots_body,
        name="invert_slots",
        in_specs=[smem, pl.BlockSpec(memory_space=pl.ANY)],
        out_specs=smem,
        out_shape=jax.ShapeDtypeStruct(init.shape, jnp.int32),
        scratch_shapes=[pltpu.SemaphoreType.DMA(())],
    )(slot, init)


def _route_plan(route, n):
    n_assign = n * TOP_K
    flat_e = route[:, :TOP_K].astype(jnp.int32).reshape(-1)
    onehot = (flat_e[:, None] == jnp.arange(N_EXPERTS, dtype=jnp.int32)[None, :]).astype(jnp.int32)
    ranks = jnp.cumsum(onehot, axis=0) - onehot
    counts = jnp.sum(onehot, axis=0)
    rank = jnp.sum(ranks * onehot, axis=1)
    padded = (counts + MOE_ROWS - 1) // MOE_ROWS * MOE_ROWS
    pad_end = jnp.cumsum(padded)
    pad_start = pad_end - padded
    slot = pad_start[flat_e] + rank
    n_blocks = (n_assign + N_EXPERTS * (MOE_ROWS - 1)) // MOE_ROWS
    pad_ids = n_assign + jnp.arange(MOE_ROWS, dtype=jnp.int32)
    slot_assign = _invert_slots(slot, jnp.tile(pad_ids, n_blocks))
    valid = slot_assign < n_assign
    slot_src = jnp.where(valid, slot_assign // TOP_K, 0)
    slot_dst = jnp.where(valid, (slot_assign % TOP_K) * n + slot_assign // TOP_K, slot_assign)
    block_start = jnp.arange(n_blocks, dtype=jnp.int32) * MOE_ROWS
    block_expert = jnp.minimum(jnp.searchsorted(pad_end, block_start, side='right'),
                               N_EXPERTS - 1).astype(jnp.int32)
    return (block_expert, jnp.concatenate([slot_src, jnp.zeros((MOE_ROWS,), jnp.int32)]) * ROW_TILE,
            jnp.concatenate([pad_ids, slot_dst]) * ROW_TILE)


def _inv_freq(dim):
    return 1.0 / (ROPE_THETA ** (jnp.arange(0, dim, 2, dtype=F32) / dim))


def _rope_table(pos_of_lane, freq_of_lane, first_half, active=None):
    ang = pos_of_lane * freq_of_lane[None, :]
    cos, sin = jnp.cos(ang), jnp.sin(ang) * jnp.where(first_half, -1.0, 1.0)[None, :]
    if active is not None:
        cos = jnp.where(active[None, :], cos, 1.0)
        sin = jnp.where(active[None, :], sin, 0.0)
    return jnp.concatenate([cos, sin], axis=1).astype(F32)


def _rope_tables(seq, mem_len):
    lane = np.arange(LANES)
    pos = jnp.arange(seq, dtype=jnp.int32)
    posf = jnp.broadcast_to(pos.astype(F32)[:, None], (seq, LANES))
    tab_a = _rope_table(posf, _inv_freq(64)[lane % 32], (lane % 64) < 32)
    tab_b = _rope_table(posf, _inv_freq(32)[lane % 16], (lane % 32) < 16, (lane >= 64) & (lane < 96))
    rows = (pos // MEM_GRID_W).astype(F32)[:, None]
    cols = (pos % MEM_GRID_W).astype(F32)[:, None]
    pos_c = jnp.where(((lane % 64) < 32)[None, :], rows, cols)
    tab_c = _rope_table(pos_c, _inv_freq(32)[lane % 16], (lane % 32) < 16)
    f_d = _inv_freq(128)[lane % 64]
    tab_d = _rope_table(posf, f_d, lane < 64)
    mposf = jnp.broadcast_to(jnp.arange(mem_len, dtype=F32)[:, None], (mem_len, LANES))
    tab_m = _rope_table(mposf, f_d, lane < 64)
    return jnp.concatenate([tab_a, tab_b, tab_c, tab_d], axis=1), tab_m


def _score_bound(q_gain, k_gain, dim):
    gq = jnp.max(jnp.abs(q_gain.astype(F32)))
    gk = jnp.max(jnp.abs(k_gain.astype(F32)))
    return (1.02 * LOG2_E * dim ** 0.5 * gq * gk).reshape(1)


def _segment_matrix(width):
    i = np.arange(MXU_DIM)
    return jnp.asarray((i[:, None] // width) == (i[None, :] // width), dtype=BF16)


def _tile_row(g, reps):
    return jnp.tile(g.astype(F32), reps).reshape(1, -1)


def _layer_params(layer, w_in, w_mla_q_up, w_mla_kv_up, w_mem_kv, gains):
    w = w_in[layer]
    z = lambda c: jnp.zeros((D_MODEL, c), F32)
    gk, gv = w[:, 2464:2592], w[:, 2592:2720]
    dup = lambda m: jnp.concatenate([m[:, :64], m[:, :64], m[:, 64:], m[:, 64:]], axis=1)
    w_big = jnp.concatenate([
        w[:, 0:1536], w[:, 1536:1792], w[:, 1792:1920],
        z(64), w[:, 1920:1952], z(32),
        w[:, 1952:2464], dup(gk), dup(gv), w[:, 2720:3232]], axis=1)
    wq = w_mla_q_up[layer].reshape(256, MLA_HEADS, MLA_QK_DIM)
    wq = jnp.pad(wq, ((0, 0), (0, 0), (0, LANES - MLA_QK_DIM))).reshape(256, MLA_HEADS * LANES)
    wkv = w_mla_kv_up[layer].reshape(128, MLA_HEADS, 128)
    wk = jnp.pad(wkv[:, :, :64], ((0, 0), (0, 0), (0, 64))).reshape(128, MLA_HEADS * LANES)
    wv = wkv[:, :, 64:].reshape(128, MLA_HEADS * 64)
    wm = w_mem_kv[layer].reshape(D_MODEL, 4, 256)
    w_mem = jnp.concatenate([wm[:, :, :128].reshape(D_MODEL, 512), wm[:, :, 128:].reshape(D_MODEL, 512)], axis=1)
    pad96 = lambda g: jnp.pad(g.astype(F32), (0, LANES - MLA_QK_DIM)).reshape(1, LANES)
    (dq_g, dk_g, ql_g, kvl_g, mq_g, mk_g, gq_g, gk_g, memq_g) = gains
    gain_rows = jnp.concatenate([
        _tile_row(dq_g, 2), _tile_row(dk_g, 2), ql_g.astype(F32).reshape(2, LANES),
        kvl_g.astype(F32).reshape(1, LANES), pad96(mq_g), pad96(mk_g),
        _tile_row(gq_g, 2), _tile_row(gk_g, 2), memq_g.astype(F32).reshape(1, LANES),
        jnp.zeros((6, LANES), F32)], axis=0)
    return (w_big.astype(BF16), wq.astype(BF16), jnp.concatenate([wk, wv], axis=1).astype(BF16),
            w_mem.astype(BF16), gain_rows)


def kernel(x, mem, norm_attn_g, norm_mem_g, w_in, diff_q_norm_g, diff_k_norm_g, diff_lambda, diff_subln_g,
           mla_q_lat_g, mla_kv_lat_g, w_mla_q_up, w_mla_kv_up, mla_q_norm_g, mla_k_norm_g, gqa_q_norm_g,
           gqa_k_norm_g, w_mem_kv, mem_q_norm_g, mem_k_norm_g, w_branch, w_gate, b_gate, w_out, norm_ffn_g,
           dense_w1, dense_w3, dense_w2, moe_router, moe_w1, moe_w3, moe_w2):
    batch, seq, _ = x.shape
    mem_len = mem.shape[1]
    n = batch * seq
    depth = w_in.shape[0]
    tab, tab_m = _rope_tables(seq, mem_len)
    seg64, seg128 = _segment_matrix(64), _segment_matrix(128)
    x2 = x.reshape(n, D_MODEL)
    mem2 = mem.reshape(batch * mem_len, D_MODEL)
    row = lambda g: g.astype(F32).reshape(1, -1)
    b3 = lambda a, w: a.reshape(batch, -1, w)

    for layer in range(depth):
        lambda_init = 0.8 - 0.6 * math.exp(-0.3 * layer)
        w_big, wq_up, wkv_up, w_mem, gain_rows = _layer_params(
            layer, w_in, w_mla_q_up, w_mla_kv_up, w_mem_kv,
            (diff_q_norm_g[layer], diff_k_norm_g[layer], mla_q_lat_g[layer], mla_kv_lat_g[layer],
             mla_q_norm_g[layer], mla_k_norm_g[layer], gqa_q_norm_g[layer], gqa_k_norm_g[layer],
             mem_q_norm_g[layer]))
        qa, ka, va, qb, kb, vb, qc, kc, vc, qd = _inproj(
            x2, row(norm_attn_g[layer]), w_big, wq_up, wkv_up, seg64, seg128, gain_rows, tab, seq)
        kd, vd = _memprep(mem2, row(norm_mem_g[layer]), w_mem, seg128, row(mem_k_norm_g[layer]), tab_m, mem_len)

        o_a = _attention("attn_diff", [(0, 0, 0, 0), (0, 1, 0, 0), (128, 0, 128, 128), (128, 1, 128, 128)],
                         functools.partial(_finish_diff, lambda_init=lambda_init),
                         _score_bound(diff_q_norm_g[layer], diff_k_norm_g[layer], 64),
                         b3(qa, 512), b3(ka, 512), b3(va, 512), groups=2, q_width=256,
                         k_width=256, v_width=256, o_width=256, tq=512,
                         extra=(diff_lambda[layer].astype(F32), row(diff_subln_g[layer])))
        o_b = _attention("attn_latent", [(128 * u, None, 128 * u, 128 * (u // 2)) for u in range(4)],
                         _finish_pairs, _score_bound(mla_q_norm_g[layer], mla_k_norm_g[layer], MLA_QK_DIM),
                         b3(qb, 1024), b3(kb, 1024), b3(vb, 512), groups=2,
                         q_width=512, k_width=512, v_width=256, o_width=256, tq=512, sums_on_mxu=True)
        o_c = _attention("attn_gqa", [(0, 0, 0, 0), (0, 1, 0, 0), (128, 0, 0, 0), (128, 1, 0, 0)],
                         _finish_pairs, _score_bound(gqa_q_norm_g[layer], gqa_k_norm_g[layer], 64),
                         b3(qc, 512), b3(kc, 256), b3(vc, 256), groups=2,
                         q_width=256, k_width=128, v_width=128, o_width=256, tq=512, sums_on_mxu=True)
        o_d = _attention("attn_mem", [(128 * u, None, 128 * u, 128 * u) for u in range(4)],
                         _finish_tiles, _score_bound(mem_q_norm_g[layer], mem_k_norm_g[layer], 128),
                         b3(qd, 512), b3(kd, 512), b3(vd, 512), groups=1,
                         q_width=512, k_width=512, v_width=512, o_width=512, tq=512)
        branches = [o.reshape(n, 512) for o in (o_a, o_b, o_c, o_d)]

        i = layer // 2
        routed = layer % 2 == 1
        w_router = None
        if routed:
            w_r = jnp.pad(moe_router[i].astype(F32), ((0, 0), (0, LANES - N_EXPERTS)))
            w_r_hi = w_r.astype(BF16)
            w_router = jnp.stack([w_r_hi, (w_r - w_r_hi.astype(F32)).astype(BF16)])
        merged = _merge(x2, branches, row(norm_attn_g[layer]), w_gate[layer].astype(BF16),
                        b_gate[layer].astype(F32).reshape(4, 1, D_MODEL), w_branch[layer].astype(BF16),
                        w_out[layer].astype(BF16), row(norm_ffn_g[layer]), w_router)
        if not routed:
            x_mid, h2 = merged
            x2 = _dense_ffn(x_mid, h2, dense_w1[i].astype(BF16), dense_w3[i].astype(BF16),
                            dense_w2[i].astype(BF16))
        else:
            x_mid, h2, route = merged
            block_expert, slot_src, slot_dst = _route_plan(route, n)
            y = _moe_ffn(h2, block_expert, slot_src, slot_dst, moe_w1[i].astype(BF16),
                         moe_w3[i].astype(BF16), moe_w2[i].astype(BF16))
            x2 = _combine(x_mid, y, route)
    return x2.reshape(batch, seq, D_MODEL)
```

```python
import functools
import math

import numpy as np
import jax
import jax.numpy as jnp
from jax import lax
from jax.experimental import pallas as pl
from jax.experimental.pallas import tpu as pltpu

F32 = jnp.float32
BF16 = jnp.bfloat16

D_MODEL = 1024
MEM_GRID_W = 64
ROPE_THETA = 10000.0
NORM_EPS = 1e-6
LOG2_E = math.log2(math.e)
DIFF_HEADS = 4
MLA_HEADS = 8
MLA_QK_DIM = 96
FFN_DIM = 3584
N_EXPERTS = 8
TOP_K = 2
MOE_ROWS = 512
LANES = 128
MXU_DIM = 256
VMEM_LIMIT = 56 * 1024 * 1024

C_DQ, C_DK, C_DV = 0, 512, 1024
C_MQL, C_MKVL, C_KROPE = 1536, 1792, 1920
C_GQ, C_GK, C_GV, C_MEMQ = 2048, 2560, 2816, 3072
IN_COLS_PADDED = 3584
T_A, T_B, T_C, T_D = 0, 256, 512, 768


def _const_spec(shape):
    return pl.BlockSpec(shape, lambda *_: (0,) * len(shape), pipeline_mode=pl.Buffered(1))


def _rms_rows(x, g):
    ms = jnp.mean(x * x, axis=-1, keepdims=True)
    return x * lax.rsqrt(ms + NORM_EPS) * g


def _segment_mean_sq(y, seg_ref, inv_count):
    sq = (y * y).astype(BF16)
    parts = [jnp.dot(sq[:, c:c + MXU_DIM], seg_ref[...], preferred_element_type=F32)
             for c in range(0, y.shape[1], MXU_DIM)]
    ss = parts[0] if len(parts) == 1 else jnp.concatenate(parts, axis=1)
    return ss * inv_count


def _rope_lanes(y, cos, sin_signed, half):
    lane = lax.broadcasted_iota(jnp.int32, y.shape, 1)
    first = (lane % (2 * half)) < half
    rot = jnp.where(first, pltpu.roll(y, LANES - half, 1), pltpu.roll(y, half, 1))
    return y * cos + rot * sin_signed


def _norm_rope_store(y, ms, gain, tab_ref, t_off, half, scale, out_ref):
    yn = y * lax.rsqrt(ms + NORM_EPS)
    cos = tab_ref[:, t_off:t_off + LANES]
    sin = tab_ref[:, t_off + LANES:t_off + 2 * LANES]
    for c in range(0, y.shape[1], LANES):
        r = _rope_lanes(yn[:, c:c + LANES] * gain, cos, sin, half)
        if scale != 1.0:
            r = r * scale
        out_ref[:, c:c + LANES] = r.astype(out_ref.dtype)


def _inproj_body(x_ref, g_ref, w_ref, wq_ref, wkv_ref, seg64_ref, seg128_ref, gains_ref, tab_ref,
                 qa_ref, ka_ref, va_ref, qb_ref, kb_ref, vb_ref, qc_ref, kc_ref, vc_ref, qd_ref):
    h = _rms_rows(x_ref[...], g_ref[...]).astype(BF16)

    def proj(c0, width):
        return jnp.dot(h, w_ref[:, c0:c0 + width], preferred_element_type=F32)

    def gain(row):
        return gains_ref[row:row + 1, :]

    y = proj(C_DQ, 512)
    _norm_rope_store(y, _segment_mean_sq(y, seg64_ref, 1.0 / 64), gain(0), tab_ref, T_A, 32,
                     LOG2_E * 64 ** -0.5, qa_ref)
    y = proj(C_DK, 512)
    _norm_rope_store(y, _segment_mean_sq(y, seg64_ref, 1.0 / 64), gain(1), tab_ref, T_A, 32, 1.0, ka_ref)
    va_ref[...] = proj(C_DV, 512).astype(BF16)

    g_ql = jnp.concatenate([gain(2), gain(3)], axis=1)
    ql = _rms_rows(proj(C_MQL, 256), g_ql).astype(BF16)
    y = jnp.dot(ql, wq_ref[...], preferred_element_type=F32)
    _norm_rope_store(y, _segment_mean_sq(y, seg128_ref, 1.0 / MLA_QK_DIM), gain(5), tab_ref, T_B, 16,
                     LOG2_E * MLA_QK_DIM ** -0.5, qb_ref)
    kvl = _rms_rows(proj(C_MKVL, 128), gain(4)).astype(BF16)
    kv = jnp.dot(kvl, wkv_ref[...], preferred_element_type=F32)
    k_rope = proj(C_KROPE, 128)
    y = kv[:, :1024] + jnp.concatenate([k_rope] * MLA_HEADS, axis=1)
    _norm_rope_store(y, _segment_mean_sq(y, seg128_ref, 1.0 / MLA_QK_DIM), gain(6), tab_ref, T_B, 16,
                     1.0, kb_ref)
    vb_ref[...] = kv[:, 1024:].astype(BF16)

    y = proj(C_GQ, 512)
    _norm_rope_store(y, _segment_mean_sq(y, seg64_ref, 1.0 / 64), gain(7), tab_ref, T_C, 16,
                     LOG2_E * 64 ** -0.5, qc_ref)
    y = proj(C_GK, 256)
    _norm_rope_store(y, _segment_mean_sq(y, seg64_ref, 1.0 / 64), gain(8), tab_ref, T_C, 16, 1.0, kc_ref)
    vc_ref[...] = proj(C_GV, 256).astype(BF16)

    y = proj(C_MEMQ, 512)
    _norm_rope_store(y, _segment_mean_sq(y, seg128_ref, 1.0 / 128), gain(9), tab_ref, T_D, 64,
                     LOG2_E * 128 ** -0.5, qd_ref)


def _inproj(x2, g, w_big, wq_up, wkv_up, seg64, seg128, gains, tab, seq):
    n = x2.shape[0]
    tm = min(512, seq)
    tiles_per_seq = seq // tm
    row = lambda w: pl.BlockSpec((tm, w), lambda i: (i, 0))
    out_widths = (512, 512, 512, 1024, 1024, 512, 512, 256, 256, 512)
    return pl.pallas_call(
        _inproj_body,
        name="inproj",
        grid=(n // tm,),
        in_specs=[row(D_MODEL), _const_spec((1, D_MODEL)), _const_spec(w_big.shape),
                  _const_spec(wq_up.shape), _const_spec(wkv_up.shape), _const_spec(seg64.shape),
                  _const_spec(seg128.shape), _const_spec(gains.shape),
                  pl.BlockSpec((tm, tab.shape[1]), lambda i: (i % tiles_per_seq, 0))],
        out_specs=[row(w) for w in out_widths],
        out_shape=[jax.ShapeDtypeStruct((n, w), BF16) for w in out_widths],
        compiler_params=pltpu.CompilerParams(dimension_semantics=("parallel",),
                                             vmem_limit_bytes=VMEM_LIMIT),
    )(x2, g, w_big, wq_up, wkv_up, seg64, seg128, gains, tab)


def _memprep_body(m_ref, g_ref, w_ref, seg128_ref, gain_ref, tab_ref, k_ref, v_ref):
    mn = _rms_rows(m_ref[...], g_ref[...]).astype(BF16)
    kv = jnp.dot(mn, w_ref[...], preferred_element_type=F32)
    y = kv[:, :512]
    _norm_rope_store(y, _segment_mean_sq(y, seg128_ref, 1.0 / 128), gain_ref[...], tab_ref, 0, 64, 1.0, k_ref)
    v_ref[...] = kv[:, 512:].astype(BF16)


def _memprep(mem2, g, w_kv, seg128, gain, tab_m, mem_len):
    n = mem2.shape[0]
    row = lambda w: pl.BlockSpec((mem_len, w), lambda i: (i, 0))
    return pl.pallas_call(
        _memprep_body,
        name="memprep",
        grid=(n // mem_len,),
        in_specs=[row(D_MODEL), _const_spec((1, D_MODEL)), _const_spec(w_kv.shape),
                  _const_spec(seg128.shape), _const_spec((1, LANES)), _const_spec(tab_m.shape)],
        out_specs=[row(512), row(512)],
        out_shape=[jax.ShapeDtypeStruct((n, 512), BF16)] * 2,
        compiler_params=pltpu.CompilerParams(dimension_semantics=("parallel",),
                                             vmem_limit_bytes=VMEM_LIMIT),
    )(mem2, g, w_kv, seg128, gain, tab_m)


SAFE_SHIFT_LIMIT = 60.0


def _attend(q, k, v, sums_on_mxu, shift=None):
    s = lax.dot_general(q, k, (((1,), (1,)), ((), ())), preferred_element_type=F32)
    e = jnp.exp2(s - (jnp.max(s, axis=-1, keepdims=True) if shift is None else shift))
    if not sums_on_mxu:
        l = jnp.sum(e, axis=-1, keepdims=True)
        return jnp.dot(e.astype(BF16), v, preferred_element_type=F32) * (1.0 / l)
    v_ones = jnp.concatenate([v, jnp.ones((v.shape[0], LANES), BF16)], axis=1)
    ov = jnp.dot(e.astype(BF16), v_ones, preferred_element_type=F32)
    return ov[:, :LANES] * (1.0 / ov[:, LANES:LANES + 1])


def _split_halves(q):
    lo = lax.broadcasted_iota(jnp.int32, q.shape, 1) < 64
    zero = jnp.zeros_like(q)
    return jnp.where(lo, q, zero), jnp.where(lo, zero, q)


def _join_halves(o_lo, o_hi):
    lo = lax.broadcasted_iota(jnp.int32, o_lo.shape, 1) < 64
    return jnp.where(lo, o_lo, o_hi)


def _attn_body(shift_ref, *refs, units, finish, n_extra, sums_on_mxu):
    extra = refs[:n_extra]
    q_ref, k_ref, v_ref, o_ref = refs[n_extra:]
    shift = shift_ref[0]

    def run(shift):
        halves = {}
        outs = []
        for q_lane, half, k_lane, v_lane in units:
            q = q_ref[:, q_lane:q_lane + LANES]
            if half is not None:
                if q_lane not in halves:
                    halves[q_lane] = _split_halves(q)
                q = halves[q_lane][half]
            outs.append(_attend(q, k_ref[:, k_lane:k_lane + LANES], v_ref[:, v_lane:v_lane + LANES],
                                sums_on_mxu, shift))
        finish(outs, o_ref, *extra)

    @pl.when(shift <= SAFE_SHIFT_LIMIT)
    def _():
        run(shift)

    @pl.when(jnp.logical_not(shift <= SAFE_SHIFT_LIMIT))
    def _():
        run(None)


def _finish_pairs(outs, o_ref):
    for p in range(len(outs) // 2):
        o_ref[:, p * LANES:(p + 1) * LANES] = _join_halves(outs[2 * p], outs[2 * p + 1]).astype(o_ref.dtype)


def _finish_tiles(outs, o_ref):
    for p, o in enumerate(outs):
        o_ref[:, p * LANES:(p + 1) * LANES] = o.astype(o_ref.dtype)


def _finish_diff(outs, o_ref, lam_ref, g_ref, *, lambda_init):
    lp = lam_ref[...]
    lam = (jnp.exp(jnp.sum(lp[0:1] * lp[1:2], keepdims=True))
           - jnp.exp(jnp.sum(lp[2:3] * lp[3:4], keepdims=True)) + lambda_init)
    for p in range(len(outs) // 2):
        o = outs[2 * p] - outs[2 * p + 1] * lam
        o_ref[:, p * LANES:(p + 1) * LANES] = (
            _rms_rows(o, g_ref[...]) * (1.0 - lambda_init)).astype(o_ref.dtype)


def _attention(name, units, finish, shift, q, k, v, *, groups, q_width, k_width, v_width, o_width, tq,
               sums_on_mxu=False, extra=()):
    b, s, _ = q.shape
    kv_len = k.shape[1]
    tq = min(tq, s)
    grid_spec = pltpu.PrefetchScalarGridSpec(
        num_scalar_prefetch=1,
        grid=(b, groups, s // tq),
        in_specs=[_const_spec(e.shape) for e in extra] + [
            pl.BlockSpec((None, tq, q_width), lambda bi, gi, qi, _: (bi, qi, gi)),
            pl.BlockSpec((None, kv_len, k_width), lambda bi, gi, qi, _: (bi, 0, gi)),
            pl.BlockSpec((None, kv_len, v_width), lambda bi, gi, qi, _: (bi, 0, gi)),
        ],
        out_specs=pl.BlockSpec((None, tq, o_width), lambda bi, gi, qi, _: (bi, qi, gi)),
    )
    return pl.pallas_call(
        functools.partial(_attn_body, units=units, finish=finish, n_extra=len(extra),
                          sums_on_mxu=sums_on_mxu),
        name=name,
        grid_spec=grid_spec,
        out_shape=jax.ShapeDtypeStruct((b, s, groups * o_width), BF16),
        compiler_params=pltpu.CompilerParams(
            dimension_semantics=("parallel", "parallel", "arbitrary"), vmem_limit_bytes=VMEM_LIMIT),
    )(shift, *extra, q, k, v)


def _route_rows(h2, wr_ref, route_ref):
    h_hi = h2.astype(BF16)
    h_lo = (h2 - h_hi.astype(F32)).astype(BF16)
    logits = (jnp.dot(h_hi, wr_ref[0], preferred_element_type=F32)
              + (jnp.dot(h_hi, wr_ref[1], preferred_element_type=F32)
                 + jnp.dot(h_lo, wr_ref[0], preferred_element_type=F32)))
    l1 = logits.T[:N_EXPERTS, :]
    row = lax.broadcasted_iota(jnp.int32, l1.shape, 0)
    neg = jnp.float32(-jnp.inf)
    m1 = jnp.max(l1, axis=0, keepdims=True)
    i1 = jnp.min(jnp.where(l1 == m1, row, N_EXPERTS), axis=0, keepdims=True)
    l2 = jnp.where(row == i1, neg, l1)
    m2 = jnp.max(l2, axis=0, keepdims=True)
    i2 = jnp.min(jnp.where(l2 == m2, row, N_EXPERTS), axis=0, keepdims=True)
    e2 = jnp.exp(m2 - m1)
    g1 = 1.0 / (1.0 + e2)
    g2 = e2 / (1.0 + e2)
    route_t = jnp.where(row == 0, i1.astype(F32),
                        jnp.where(row == 1, i2.astype(F32),
                                  jnp.where(row == 2, g1, jnp.where(row == 3, g2, 0.0))))
    pad = jnp.zeros((LANES - N_EXPERTS, route_t.shape[1]), F32)
    route_ref[...] = jnp.concatenate([route_t, pad], axis=0).T


def _merge_body(*refs, with_router):
    if with_router:
        (x_ref, oa_ref, ob_ref, oc_ref, od_ref, g_ref, wg_ref, bg_ref, wb_ref, wo_ref, gf_ref, wr_ref,
         xo_ref, h2_ref, route_ref, hprev_ref) = refs

        @pl.when(pl.program_id(0) == 0)
        def _():
            hprev_ref[...] = jnp.zeros(hprev_ref.shape, F32)

        _route_rows(hprev_ref[...], wr_ref, route_ref)
    else:
        (x_ref, oa_ref, ob_ref, oc_ref, od_ref, g_ref, wg_ref, bg_ref, wb_ref, wo_ref, gf_ref,
         xo_ref, h2_ref) = refs
    x = x_ref[...]
    h = _rms_rows(x, g_ref[...]).astype(BF16)
    merged = None
    for b, o_ref in enumerate((oa_ref, ob_ref, oc_ref, od_ref)):
        gate = jax.nn.sigmoid(jnp.dot(h, wg_ref[b], preferred_element_type=F32) + bg_ref[b])
        term = gate * jnp.dot(o_ref[...], wb_ref[b], preferred_element_type=F32)
        merged = term if merged is None else merged + term
    x_new = x + jnp.dot(merged.astype(BF16), wo_ref[...], preferred_element_type=F32)
    xo_ref[...] = x_new
    h2 = _rms_rows(x_new, gf_ref[...])
    if with_router:
        _tile_rows_store(h2_ref, h2)
        hprev_ref[...] = h2
    else:
        h2_ref[...] = h2.astype(h2_ref.dtype)


def _merge(x2, branches, g, wg, bg, wb, wo, gf, w_router=None):
    n = x2.shape[0]
    tm = min(512, n)
    n_tiles = n // tm
    with_router = w_router is not None
    row = lambda w: pl.BlockSpec((tm, w), lambda i: (jnp.minimum(i, n_tiles - 1), 0))
    in_specs = ([row(D_MODEL)] + [row(512)] * 4
                + [_const_spec((1, D_MODEL)), _const_spec(wg.shape), _const_spec(bg.shape),
                   _const_spec(wb.shape), _const_spec(wo.shape), _const_spec((1, D_MODEL))])
    args = [x2, *branches, g, wg, bg, wb, wo, gf]
    out_specs = [row(D_MODEL), row(D_MODEL)]
    out_shape = [jax.ShapeDtypeStruct((n, D_MODEL), F32), jax.ShapeDtypeStruct((n, D_MODEL), BF16)]
    scratch_shapes = []
    if with_router:
        out_specs[1] = pl.BlockSpec((tm * ROW_TILE, LANES), lambda i: (jnp.minimum(i, n_tiles - 1), 0))
        out_shape[1] = jax.ShapeDtypeStruct((n * ROW_TILE, LANES), F32)
        in_specs.append(_const_spec(w_router.shape))
        args.append(w_router)
        out_specs.append(pl.BlockSpec((tm, LANES), lambda i: (jnp.maximum(i - 1, 0), 0)))
        out_shape.append(jax.ShapeDtypeStruct((n, LANES), F32))
        scratch_shapes.append(pltpu.VMEM((tm, D_MODEL), F32))
    return pl.pallas_call(
        functools.partial(_merge_body, with_router=with_router),
        name="merge_router" if with_router else "merge",
        grid=(n_tiles + 1 if with_router else n_tiles,),
        in_specs=in_specs, out_specs=out_specs, out_shape=out_shape, scratch_shapes=scratch_shapes,
        compiler_params=pltpu.CompilerParams(
            dimension_semantics=("arbitrary",) if with_router else ("parallel",),
            vmem_limit_bytes=VMEM_LIMIT),
    )(*args)


def _swiglu_chunk(xb, w1, w3, w2):
    a = jnp.dot(xb, w1, preferred_element_type=F32)
    b = jnp.dot(xb, w3, preferred_element_type=F32)
    return jnp.dot((jax.nn.silu(a) * b).astype(BF16), w2, preferred_element_type=F32)


def _ffn_body(x_ref, h_ref, w1_ref, w3_ref, w2_ref, o_ref):
    @pl.when(pl.program_id(1) == 0)
    def _():
        o_ref[...] = x_ref[...]

    o_ref[...] += _swiglu_chunk(h_ref[...], w1_ref[...], w3_ref[...], w2_ref[...])


def _dense_ffn(x2, h2, w1, w3, w2):
    n = x2.shape[0]
    tm = min(512, n)
    tf = 1792
    return pl.pallas_call(
        _ffn_body,
        name="dense_ffn",
        grid=(n // tm, FFN_DIM // tf),
        in_specs=[pl.BlockSpec((tm, D_MODEL), lambda i, f: (i, 0)),
                  pl.BlockSpec((tm, D_MODEL), lambda i, f: (i, 0)),
                  pl.BlockSpec((D_MODEL, tf), lambda i, f: (0, f)),
                  pl.BlockSpec((D_MODEL, tf), lambda i, f: (0, f)),
                  pl.BlockSpec((tf, D_MODEL), lambda i, f: (f, 0))],
        out_specs=pl.BlockSpec((tm, D_MODEL), lambda i, f: (i, 0)),
        out_shape=jax.ShapeDtypeStruct((n, D_MODEL), F32),
        compiler_params=pltpu.CompilerParams(dimension_semantics=("parallel", "arbitrary"),
                                             vmem_limit_bytes=VMEM_LIMIT),
    )(x2, h2, w1, w3, w2)


MOE_TF = 1792


ROW_TILE = D_MODEL // LANES


def _tile_rows_load(ref, rows):
    return jnp.concatenate([ref[pl.ds(s, rows, stride=ROW_TILE), :] for s in range(ROW_TILE)], axis=1)


def _tile_rows_store(ref, x):
    for s in range(ROW_TILE):
        ref[pl.ds(s, x.shape[0], stride=ROW_TILE), :] = x[:, s * LANES:(s + 1) * LANES]


def _moe_body(expert_ref, used_ref, src_ref, dst_ref, h_hbm, w1_ref, w3_ref, w2_ref, y_hbm,
              xg_ref, xb_ref, acc_ref, stage_ref, gsem, ssem):
    j = pl.program_id(0)
    f = pl.program_id(1)
    n_f = pl.num_programs(1)
    slot = j % 2
    other = 1 - slot
    rows_per_step = MOE_ROWS // (FFN_DIM // MOE_TF)

    def tile_of(r):
        first = r * ROW_TILE
        return pl.ds(first if isinstance(r, int) else pl.multiple_of(first, ROW_TILE), ROW_TILE)

    def gather_copy(block, r, buf):
        src = pl.multiple_of(src_ref[block * MOE_ROWS + r], ROW_TILE)
        return pltpu.make_async_copy(h_hbm.at[pl.ds(src, ROW_TILE), :], xg_ref.at[buf, tile_of(r), :],
                                     gsem.at[buf])

    def scatter_copy(block, r, buf):
        dst = pl.multiple_of(dst_ref[(block + 1) * MOE_ROWS + r], ROW_TILE)
        return pltpu.make_async_copy(stage_ref.at[buf, tile_of(r), :], y_hbm.at[pl.ds(dst, ROW_TILE), :],
                                     ssem)

    def wait_rows(copy):
        for _ in range(MOE_ROWS):
            copy.wait()

    @pl.when(jnp.logical_and(j == 0, f == 0))
    def _():
        def start(r, c):
            gather_copy(0, r, 0).start()
            return c
        lax.fori_loop(0, MOE_ROWS, start, 0)
        stage_ref[1] = jnp.zeros(stage_ref.shape[1:], F32)

    @pl.when(f == 0)
    def _():
        wait_rows(gather_copy(j, 0, slot))
        xb_ref[...] = _tile_rows_load(xg_ref.at[slot], MOE_ROWS).astype(BF16)

    for step in range(FFN_DIM // MOE_TF):
        @pl.when(f == step)
        def _():
            for r in range(step * rows_per_step, (step + 1) * rows_per_step):
                gather_copy(j + 1, r, other).start()
                scatter_copy(j - 1, r, other).start()

    @pl.when(f == 0)
    def _():
        acc_ref[...] = jnp.zeros((MOE_ROWS, D_MODEL), F32)

    @pl.when(used_ref[j] > 0)
    def _():
        acc_ref[...] += _swiglu_chunk(xb_ref[...], w1_ref[...], w3_ref[...], w2_ref[...])

    @pl.when(f == n_f - 1)
    def _():
        wait_rows(scatter_copy(j - 1, 0, other))
        _tile_rows_store(stage_ref.at[slot], acc_ref[...])

    @pl.when(jnp.logical_and(j == pl.num_programs(0) - 1, f == n_f - 1))
    def _():
        def start(r, c):
            scatter_copy(j, r, slot).start()
            return c
        lax.fori_loop(0, MOE_ROWS, start, 0)
        wait_rows(scatter_copy(j, 0, slot))
        wait_rows(gather_copy(j + 1, 0, other))


def _moe_ffn(h2_tiles, block_expert, block_used, slot_src, slot_dst, w1, w3, w2):
    n = h2_tiles.shape[0] // ROW_TILE
    n_blocks = block_expert.shape[0]
    wspec = lambda shape, imap: pl.BlockSpec(shape, imap)
    grid_spec = pltpu.PrefetchScalarGridSpec(
        num_scalar_prefetch=4,
        grid=(n_blocks, FFN_DIM // MOE_TF),
        in_specs=[pl.BlockSpec(memory_space=pl.ANY),
                  wspec((None, D_MODEL, MOE_TF), lambda j, f, e, u, s, d: (e[j], 0, f)),
                  wspec((None, D_MODEL, MOE_TF), lambda j, f, e, u, s, d: (e[j], 0, f)),
                  wspec((None, MOE_TF, D_MODEL), lambda j, f, e, u, s, d: (e[j], f, 0))],
        out_specs=pl.BlockSpec(memory_space=pl.ANY),
        scratch_shapes=[pltpu.VMEM((2, MOE_ROWS * ROW_TILE, LANES), F32),
                        pltpu.VMEM((MOE_ROWS, D_MODEL), BF16),
                        pltpu.VMEM((MOE_ROWS, D_MODEL), F32),
                        pltpu.VMEM((2, MOE_ROWS * ROW_TILE, LANES), F32),
                        pltpu.SemaphoreType.DMA((2,)),
                        pltpu.SemaphoreType.DMA(())],
    )
    return pl.pallas_call(
        _moe_body,
        name="moe_ffn",
        grid_spec=grid_spec,
        out_shape=jax.ShapeDtypeStruct(((TOP_K * n + MOE_ROWS) * ROW_TILE, LANES), F32),
        compiler_params=pltpu.CompilerParams(dimension_semantics=("arbitrary", "arbitrary"),
                                             vmem_limit_bytes=VMEM_LIMIT),
    )(block_expert, block_used, slot_src, slot_dst, h2_tiles, w1, w3, w2)


def _combine_body(x_ref, y1_ref, y2_ref, route_ref, o_ref):
    g1 = route_ref[:, 2:3]
    g2 = route_ref[:, 3:4]
    rows = x_ref.shape[0]
    o_ref[...] = x_ref[...] + (_tile_rows_load(y1_ref, rows) * g1 + _tile_rows_load(y2_ref, rows) * g2)


def _combine(x2, y, route):
    n = x2.shape[0]
    tm = min(512, n)
    row = lambda w: pl.BlockSpec((tm, w), lambda i: (i, 0))
    return pl.pallas_call(
        _combine_body,
        name="moe_combine",
        grid=(n // tm,),
        in_specs=[row(D_MODEL), pl.BlockSpec((tm * ROW_TILE, LANES), lambda i: (i, 0)),
                  pl.BlockSpec((tm * ROW_TILE, LANES), lambda i: (i + n // tm, 0)),
                  row(LANES)],
        out_specs=row(D_MODEL),
        out_shape=jax.ShapeDtypeStruct((n, D_MODEL), F32),
        compiler_params=pltpu.CompilerParams(dimension_semantics=("parallel",),
                                             vmem_limit_bytes=VMEM_LIMIT),
    )(x2, y, y, route)


def _invert_slots_body(slot_ref, init_hbm, out_ref, sem):
    copy = pltpu.make_async_copy(init_hbm, out_ref, sem)
    copy.start()
    copy.wait()

    def place(a, c):
        out_ref[slot_ref[a]] = a
        return c
    lax.fori_loop(0, slot_ref.shape[0], place, 0, unroll=8)


def _invert_slots(slot, init):
    smem = pl.BlockSpec(memory_space=pltpu.SMEM)
    return pl.pallas_call(
        _invert_slots_body,
        name="invert_slots",
        in_specs=[smem, pl.BlockSpec(memory_space=pl.ANY)],
        out_specs=smem,
        out_shape=jax.ShapeDtypeStruct(init.shape, jnp.int32),
        scratch_shapes=[pltpu.SemaphoreType.DMA(())],
    )(slot, init)


def _route_plan(route, n):
    n_assign = n * TOP_K
    flat_e = route[:, :TOP_K].astype(jnp.int32).reshape(-1)
    onehot = (flat_e[:, None] == jnp.arange(N_EXPERTS, dtype=jnp.int32)[None, :]).astype(jnp.int32)
    ranks = jnp.cumsum(onehot, axis=0) - onehot
    counts = jnp.sum(onehot, axis=0)
    rank = jnp.sum(ranks * onehot, axis=1)
    padded = (counts + MOE_ROWS - 1) // MOE_ROWS * MOE_ROWS
    pad_end = jnp.cumsum(padded)
    pad_start = pad_end - padded
    slot = pad_start[flat_e] + rank
    n_blocks = (n_assign + N_EXPERTS * (MOE_ROWS - 1)) // MOE_ROWS
    pad_ids = n_assign + jnp.arange(MOE_ROWS, dtype=jnp.int32)
    slot_assign = _invert_slots(slot, jnp.tile(pad_ids, n_blocks))
    valid = slot_assign < n_assign
    slot_src = jnp.where(valid, slot_assign // TOP_K, 0)
    slot_dst = jnp.where(valid, (slot_assign % TOP_K) * n + slot_assign // TOP_K, slot_assign)
    block_start = jnp.arange(n_blocks, dtype=jnp.int32) * MOE_ROWS
    block_expert = jnp.minimum(jnp.searchsorted(pad_end, block_start, side='right'),
                               N_EXPERTS - 1).astype(jnp.int32)
    block_used = (block_start < pad_end[-1]).astype(jnp.int32)
    return (block_expert, block_used,
            jnp.concatenate([slot_src, jnp.zeros((MOE_ROWS,), jnp.int32)]) * ROW_TILE,
            jnp.concatenate([pad_ids, slot_dst]) * ROW_TILE)


def _inv_freq(dim):
    return 1.0 / (ROPE_THETA ** (jnp.arange(0, dim, 2, dtype=F32) / dim))


def _rope_table(pos_of_lane, freq_of_lane, first_half, active=None):
    ang = pos_of_lane * freq_of_lane[None, :]
    cos, sin = jnp.cos(ang), jnp.sin(ang) * jnp.where(first_half, -1.0, 1.0)[None, :]
    if active is not None:
        cos = jnp.where(active[None, :], cos, 1.0)
        sin = jnp.where(active[None, :], sin, 0.0)
    return jnp.concatenate([cos, sin], axis=1).astype(F32)


def _rope_tables(seq, mem_len):
    lane = np.arange(LANES)
    pos = jnp.arange(seq, dtype=jnp.int32)
    posf = jnp.broadcast_to(pos.astype(F32)[:, None], (seq, LANES))
    tab_a = _rope_table(posf, _inv_freq(64)[lane % 32], (lane % 64) < 32)
    tab_b = _rope_table(posf, _inv_freq(32)[lane % 16], (lane % 32) < 16, (lane >= 64) & (lane < 96))
    rows = (pos // MEM_GRID_W).astype(F32)[:, None]
    cols = (pos % MEM_GRID_W).astype(F32)[:, None]
    pos_c = jnp.where(((lane % 64) < 32)[None, :], rows, cols)
    tab_c = _rope_table(pos_c, _inv_freq(32)[lane % 16], (lane % 32) < 16)
    f_d = _inv_freq(128)[lane % 64]
    tab_d = _rope_table(posf, f_d, lane < 64)
    mposf = jnp.broadcast_to(jnp.arange(mem_len, dtype=F32)[:, None], (mem_len, LANES))
    tab_m = _rope_table(mposf, f_d, lane < 64)
    return jnp.concatenate([tab_a, tab_b, tab_c, tab_d], axis=1), tab_m


def _score_bound(q_gain, k_gain, dim):
    gq = jnp.max(jnp.abs(q_gain.astype(F32)))
    gk = jnp.max(jnp.abs(k_gain.astype(F32)))
    return (1.02 * LOG2_E * dim ** 0.5 * gq * gk).reshape(1)


def _segment_matrix(width):
    i = np.arange(MXU_DIM)
    return jnp.asarray((i[:, None] // width) == (i[None, :] // width), dtype=BF16)


def _tile_row(g, reps):
    return jnp.tile(g.astype(F32), reps).reshape(1, -1)


def _layer_params(layer, w_in, w_mla_q_up, w_mla_kv_up, w_mem_kv, gains):
    w = w_in[layer]
    z = lambda c: jnp.zeros((D_MODEL, c), F32)
    gk, gv = w[:, 2464:2592], w[:, 2592:2720]
    dup = lambda m: jnp.concatenate([m[:, :64], m[:, :64], m[:, 64:], m[:, 64:]], axis=1)
    w_big = jnp.concatenate([
        w[:, 0:1536], w[:, 1536:1792], w[:, 1792:1920],
        z(64), w[:, 1920:1952], z(32),
        w[:, 1952:2464], dup(gk), dup(gv), w[:, 2720:3232]], axis=1)
    wq = w_mla_q_up[layer].reshape(256, MLA_HEADS, MLA_QK_DIM)
    wq = jnp.pad(wq, ((0, 0), (0, 0), (0, LANES - MLA_QK_DIM))).reshape(256, MLA_HEADS * LANES)
    wkv = w_mla_kv_up[layer].reshape(128, MLA_HEADS, 128)
    wk = jnp.pad(wkv[:, :, :64], ((0, 0), (0, 0), (0, 64))).reshape(128, MLA_HEADS * LANES)
    wv = wkv[:, :, 64:].reshape(128, MLA_HEADS * 64)
    wm = w_mem_kv[layer].reshape(D_MODEL, 4, 256)
    w_mem = jnp.concatenate([wm[:, :, :128].reshape(D_MODEL, 512), wm[:, :, 128:].reshape(D_MODEL, 512)], axis=1)
    pad96 = lambda g: jnp.pad(g.astype(F32), (0, LANES - MLA_QK_DIM)).reshape(1, LANES)
    (dq_g, dk_g, ql_g, kvl_g, mq_g, mk_g, gq_g, gk_g, memq_g) = gains
    gain_rows = jnp.concatenate([
        _tile_row(dq_g, 2), _tile_row(dk_g, 2), ql_g.astype(F32).reshape(2, LANES),
        kvl_g.astype(F32).reshape(1, LANES), pad96(mq_g), pad96(mk_g),
        _tile_row(gq_g, 2), _tile_row(gk_g, 2), memq_g.astype(F32).reshape(1, LANES),
        jnp.zeros((6, LANES), F32)], axis=0)
    return (w_big.astype(BF16), wq.astype(BF16), jnp.concatenate([wk, wv], axis=1).astype(BF16),
            w_mem.astype(BF16), gain_rows)


def kernel(x, mem, norm_attn_g, norm_mem_g, w_in, diff_q_norm_g, diff_k_norm_g, diff_lambda, diff_subln_g,
           mla_q_lat_g, mla_kv_lat_g, w_mla_q_up, w_mla_kv_up, mla_q_norm_g, mla_k_norm_g, gqa_q_norm_g,
           gqa_k_norm_g, w_mem_kv, mem_q_norm_g, mem_k_norm_g, w_branch, w_gate, b_gate, w_out, norm_ffn_g,
           dense_w1, dense_w3, dense_w2, moe_router, moe_w1, moe_w3, moe_w2):
    batch, seq, _ = x.shape
    mem_len = mem.shape[1]
    n = batch * seq
    depth = w_in.shape[0]
    tab, tab_m = _rope_tables(seq, mem_len)
    seg64, seg128 = _segment_matrix(64), _segment_matrix(128)
    x2 = x.reshape(n, D_MODEL)
    mem2 = mem.reshape(batch * mem_len, D_MODEL)
    row = lambda g: g.astype(F32).reshape(1, -1)
    b3 = lambda a, w: a.reshape(batch, -1, w)

    for layer in range(depth):
        lambda_init = 0.8 - 0.6 * math.exp(-0.3 * layer)
        w_big, wq_up, wkv_up, w_mem, gain_rows = _layer_params(
            layer, w_in, w_mla_q_up, w_mla_kv_up, w_mem_kv,
            (diff_q_norm_g[layer], diff_k_norm_g[layer], mla_q_lat_g[layer], mla_kv_lat_g[layer],
             mla_q_norm_g[layer], mla_k_norm_g[layer], gqa_q_norm_g[layer], gqa_k_norm_g[layer],
             mem_q_norm_g[layer]))
        qa, ka, va, qb, kb, vb, qc, kc, vc, qd = _inproj(
            x2, row(norm_attn_g[layer]), w_big, wq_up, wkv_up, seg64, seg128, gain_rows, tab, seq)
        kd, vd = _memprep(mem2, row(norm_mem_g[layer]), w_mem, seg128, row(mem_k_norm_g[layer]), tab_m, mem_len)

        o_a = _attention("attn_diff", [(0, 0, 0, 0), (0, 1, 0, 0), (128, 0, 128, 128), (128, 1, 128, 128)],
                         functools.partial(_finish_diff, lambda_init=lambda_init),
                         _score_bound(diff_q_norm_g[layer], diff_k_norm_g[layer], 64),
                         b3(qa, 512), b3(ka, 512), b3(va, 512), groups=2, q_width=256,
                         k_width=256, v_width=256, o_width=256, tq=1024,
                         extra=(diff_lambda[layer].astype(F32), row(diff_subln_g[layer])))
        o_b = _attention("attn_latent", [(128 * u, None, 128 * u, 128 * (u // 2)) for u in range(4)],
                         _finish_pairs, _score_bound(mla_q_norm_g[layer], mla_k_norm_g[layer], MLA_QK_DIM),
                         b3(qb, 1024), b3(kb, 1024), b3(vb, 512), groups=2,
                         q_width=512, k_width=512, v_width=256, o_width=256, tq=1024, sums_on_mxu=True)
        o_c = _attention("attn_gqa", [(0, 0, 0, 0), (0, 1, 0, 0), (128, 0, 0, 0), (128, 1, 0, 0)],
                         _finish_pairs, _score_bound(gqa_q_norm_g[layer], gqa_k_norm_g[layer], 64),
                         b3(qc, 512), b3(kc, 256), b3(vc, 256), groups=2,
                         q_width=256, k_width=128, v_width=128, o_width=256, tq=1024, sums_on_mxu=True)
        o_d = _attention("attn_mem", [(128 * u, None, 128 * u, 128 * u) for u in range(4)],
                         _finish_tiles, _score_bound(mem_q_norm_g[layer], mem_k_norm_g[layer], 128),
                         b3(qd, 512), b3(kd, 512), b3(vd, 512), groups=1,
                         q_width=512, k_width=512, v_width=512, o_width=512, tq=512)
        branches = [o.reshape(n, 512) for o in (o_a, o_b, o_c, o_d)]

        i = layer // 2
        routed = layer % 2 == 1
        w_router = None
        if routed:
            w_r = jnp.pad(moe_router[i].astype(F32), ((0, 0), (0, LANES - N_EXPERTS)))
            w_r_hi = w_r.astype(BF16)
            w_router = jnp.stack([w_r_hi, (w_r - w_r_hi.astype(F32)).astype(BF16)])
        merged = _merge(x2, branches, row(norm_attn_g[layer]), w_gate[layer].astype(BF16),
                        b_gate[layer].astype(F32).reshape(4, 1, D_MODEL), w_branch[layer].astype(BF16),
                        w_out[layer].astype(BF16), row(norm_ffn_g[layer]), w_router)
        if not routed:
            x_mid, h2 = merged
            x2 = _dense_ffn(x_mid, h2, dense_w1[i].astype(BF16), dense_w3[i].astype(BF16),
                            dense_w2[i].astype(BF16))
        else:
            x_mid, h2, route = merged
            block_expert, block_used, slot_src, slot_dst = _route_plan(route, n)
            y = _moe_ffn(h2, block_expert, block_used, slot_src, slot_dst, moe_w1[i].astype(BF16),
                         moe_w3[i].astype(BF16), moe_w2[i].astype(BF16))
            x2 = _combine(x_mid, y, route)
    return x2.reshape(batch, seq, D_MODEL)
```

```python
import functools
import math

import numpy as np
import jax
import jax.numpy as jnp
from jax import lax
from jax.experimental import pallas as pl
from jax.experimental.pallas import tpu as pltpu

F32 = jnp.float32
BF16 = jnp.bfloat16

D_MODEL = 1024
MEM_GRID_W = 64
ROPE_THETA = 10000.0
NORM_EPS = 1e-6
LOG2_E = math.log2(math.e)
DIFF_HEADS = 4
MLA_HEADS = 8
MLA_QK_DIM = 96
FFN_DIM = 3584
N_EXPERTS = 8
TOP_K = 2
MOE_ROWS = 512
LANES = 128
MXU_DIM = 256
VMEM_LIMIT = 56 * 1024 * 1024

C_DQ, C_DK, C_DV = 0, 512, 1024
C_MQL, C_MKVL, C_KROPE = 1536, 1792, 1920
C_GQ, C_GK, C_GV, C_MEMQ = 2048, 2560, 2816, 3072
IN_COLS_PADDED = 3584
T_A, T_B, T_C, T_D = 0, 256, 512, 768


def _const_spec(shape):
    return pl.BlockSpec(shape, lambda *_: (0,) * len(shape), pipeline_mode=pl.Buffered(1))


def _rms_rows(x, g):
    ms = jnp.mean(x * x, axis=-1, keepdims=True)
    return x * lax.rsqrt(ms + NORM_EPS) * g


def _segment_mean_sq(y, seg_ref, inv_count):
    sq = (y * y).astype(BF16)
    parts = [jnp.dot(sq[:, c:c + MXU_DIM], seg_ref[...], preferred_element_type=F32)
             for c in range(0, y.shape[1], MXU_DIM)]
    ss = parts[0] if len(parts) == 1 else jnp.concatenate(parts, axis=1)
    return ss * inv_count


def _rope_lanes(y, cos, sin_signed, half):
    lane = lax.broadcasted_iota(jnp.int32, y.shape, 1)
    first = (lane % (2 * half)) < half
    rot = jnp.where(first, pltpu.roll(y, LANES - half, 1), pltpu.roll(y, half, 1))
    return y * cos + rot * sin_signed


def _norm_rope_store(y, ms, gain, tab_ref, t_off, half, scale, out_ref):
    yn = y * lax.rsqrt(ms + NORM_EPS)
    cos = tab_ref[:, t_off:t_off + LANES]
    sin = tab_ref[:, t_off + LANES:t_off + 2 * LANES]
    for c in range(0, y.shape[1], LANES):
        r = _rope_lanes(yn[:, c:c + LANES] * gain, cos, sin, half)
        if scale != 1.0:
            r = r * scale
        out_ref[:, c:c + LANES] = r.astype(out_ref.dtype)


def _inproj_body(x_ref, g_ref, w_ref, wq_ref, wkv_ref, seg64_ref, seg128_ref, gains_ref, tab_ref,
                 qa_ref, ka_ref, va_ref, qb_ref, kb_ref, vb_ref, qc_ref, kc_ref, vc_ref, qd_ref):
    h = _rms_rows(x_ref[...], g_ref[...]).astype(BF16)

    def proj(c0, width):
        return jnp.dot(h, w_ref[:, c0:c0 + width], preferred_element_type=F32)

    def gain(row):
        return gains_ref[row:row + 1, :]

    y = proj(C_DQ, 512)
    _norm_rope_store(y, _segment_mean_sq(y, seg64_ref, 1.0 / 64), gain(0), tab_ref, T_A, 32,
                     LOG2_E * 64 ** -0.5, qa_ref)
    y = proj(C_DK, 512)
    _norm_rope_store(y, _segment_mean_sq(y, seg64_ref, 1.0 / 64), gain(1), tab_ref, T_A, 32, 1.0, ka_ref)
    va_ref[...] = proj(C_DV, 512).astype(BF16)

    g_ql = jnp.concatenate([gain(2), gain(3)], axis=1)
    ql = _rms_rows(proj(C_MQL, 256), g_ql).astype(BF16)
    y = jnp.dot(ql, wq_ref[...], preferred_element_type=F32)
    _norm_rope_store(y, _segment_mean_sq(y, seg128_ref, 1.0 / MLA_QK_DIM), gain(5), tab_ref, T_B, 16,
                     LOG2_E * MLA_QK_DIM ** -0.5, qb_ref)
    kvl = _rms_rows(proj(C_MKVL, 128), gain(4)).astype(BF16)
    kv = jnp.dot(kvl, wkv_ref[...], preferred_element_type=F32)
    k_rope = proj(C_KROPE, 128)
    y = kv[:, :1024] + jnp.concatenate([k_rope] * MLA_HEADS, axis=1)
    _norm_rope_store(y, _segment_mean_sq(y, seg128_ref, 1.0 / MLA_QK_DIM), gain(6), tab_ref, T_B, 16,
                     1.0, kb_ref)
    vb_ref[...] = kv[:, 1024:].astype(BF16)

    y = proj(C_GQ, 512)
    _norm_rope_store(y, _segment_mean_sq(y, seg64_ref, 1.0 / 64), gain(7), tab_ref, T_C, 16,
                     LOG2_E * 64 ** -0.5, qc_ref)
    y = proj(C_GK, 256)
    _norm_rope_store(y, _segment_mean_sq(y, seg64_ref, 1.0 / 64), gain(8), tab_ref, T_C, 16, 1.0, kc_ref)
    vc_ref[...] = proj(C_GV, 256).astype(BF16)

    y = proj(C_MEMQ, 512)
    _norm_rope_store(y, _segment_mean_sq(y, seg128_ref, 1.0 / 128), gain(9), tab_ref, T_D, 64,
                     LOG2_E * 128 ** -0.5, qd_ref)


def _inproj(x2, g, w_big, wq_up, wkv_up, seg64, seg128, gains, tab, seq):
    n = x2.shape[0]
    tm = min(1024, seq)
    tiles_per_seq = seq // tm
    row = lambda w: pl.BlockSpec((tm, w), lambda i: (i, 0))
    out_widths = (512, 512, 512, 1024, 1024, 512, 512, 256, 256, 512)
    return pl.pallas_call(
        _inproj_body,
        name="inproj",
        grid=(n // tm,),
        in_specs=[row(D_MODEL), _const_spec((1, D_MODEL)), _const_spec(w_big.shape),
                  _const_spec(wq_up.shape), _const_spec(wkv_up.shape), _const_spec(seg64.shape),
                  _const_spec(seg128.shape), _const_spec(gains.shape),
                  pl.BlockSpec((tm, tab.shape[1]), lambda i: (i % tiles_per_seq, 0))],
        out_specs=[row(w) for w in out_widths],
        out_shape=[jax.ShapeDtypeStruct((n, w), BF16) for w in out_widths],
        compiler_params=pltpu.CompilerParams(dimension_semantics=("parallel",),
                                             vmem_limit_bytes=VMEM_LIMIT),
    )(x2, g, w_big, wq_up, wkv_up, seg64, seg128, gains, tab)


def _memprep_body(m_ref, g_ref, w_ref, seg128_ref, gain_ref, tab_ref, k_ref, v_ref):
    mn = _rms_rows(m_ref[...], g_ref[...]).astype(BF16)
    kv = jnp.dot(mn, w_ref[...], preferred_element_type=F32)
    y = kv[:, :512]
    _norm_rope_store(y, _segment_mean_sq(y, seg128_ref, 1.0 / 128), gain_ref[...], tab_ref, 0, 64, 1.0, k_ref)
    v_ref[...] = kv[:, 512:].astype(BF16)


def _memprep(mem2, g, w_kv, seg128, gain, tab_m, mem_len):
    n = mem2.shape[0]
    row = lambda w: pl.BlockSpec((mem_len, w), lambda i: (i, 0))
    return pl.pallas_call(
        _memprep_body,
        name="memprep",
        grid=(n // mem_len,),
        in_specs=[row(D_MODEL), _const_spec((1, D_MODEL)), _const_spec(w_kv.shape),
                  _const_spec(seg128.shape), _const_spec((1, LANES)), _const_spec(tab_m.shape)],
        out_specs=[row(512), row(512)],
        out_shape=[jax.ShapeDtypeStruct((n, 512), BF16)] * 2,
        compiler_params=pltpu.CompilerParams(dimension_semantics=("parallel",),
                                             vmem_limit_bytes=VMEM_LIMIT),
    )(mem2, g, w_kv, seg128, gain, tab_m)


SAFE_SHIFT_LIMIT = 60.0


def _attend(q, k, v, sums_on_mxu, shift=None):
    s = lax.dot_general(q, k, (((1,), (1,)), ((), ())), preferred_element_type=F32)
    e = jnp.exp2(s - (jnp.max(s, axis=-1, keepdims=True) if shift is None else shift))
    if not sums_on_mxu:
        l = jnp.sum(e, axis=-1, keepdims=True)
        return jnp.dot(e.astype(BF16), v, preferred_element_type=F32) * (1.0 / l)
    v_ones = jnp.concatenate([v, jnp.ones((v.shape[0], LANES), BF16)], axis=1)
    ov = jnp.dot(e.astype(BF16), v_ones, preferred_element_type=F32)
    return ov[:, :LANES] * (1.0 / ov[:, LANES:LANES + 1])


def _split_halves(q):
    lo = lax.broadcasted_iota(jnp.int32, q.shape, 1) < 64
    zero = jnp.zeros_like(q)
    return jnp.where(lo, q, zero), jnp.where(lo, zero, q)


def _join_halves(o_lo, o_hi):
    lo = lax.broadcasted_iota(jnp.int32, o_lo.shape, 1) < 64
    return jnp.where(lo, o_lo, o_hi)


def _attn_body(shift_ref, *refs, units, finish, n_extra, sums_on_mxu):
    extra = refs[:n_extra]
    q_ref, k_ref, v_ref, o_ref = refs[n_extra:]
    shift = shift_ref[0]

    def run(shift):
        halves = {}
        outs = []
        for q_lane, half, k_lane, v_lane in units:
            q = q_ref[:, q_lane:q_lane + LANES]
            if half is not None:
                if q_lane not in halves:
                    halves[q_lane] = _split_halves(q)
                q = halves[q_lane][half]
            outs.append(_attend(q, k_ref[:, k_lane:k_lane + LANES], v_ref[:, v_lane:v_lane + LANES],
                                sums_on_mxu, shift))
        finish(outs, o_ref, *extra)

    @pl.when(shift <= SAFE_SHIFT_LIMIT)
    def _():
        run(shift)

    @pl.when(jnp.logical_not(shift <= SAFE_SHIFT_LIMIT))
    def _():
        run(None)


def _finish_pairs(outs, o_ref):
    for p in range(len(outs) // 2):
        o_ref[:, p * LANES:(p + 1) * LANES] = _join_halves(outs[2 * p], outs[2 * p + 1]).astype(o_ref.dtype)


def _finish_tiles(outs, o_ref):
    for p, o in enumerate(outs):
        o_ref[:, p * LANES:(p + 1) * LANES] = o.astype(o_ref.dtype)


def _finish_diff(outs, o_ref, lam_ref, g_ref, *, lambda_init):
    lp = lam_ref[...]
    lam = (jnp.exp(jnp.sum(lp[0:1] * lp[1:2], keepdims=True))
           - jnp.exp(jnp.sum(lp[2:3] * lp[3:4], keepdims=True)) + lambda_init)
    for p in range(len(outs) // 2):
        o = outs[2 * p] - outs[2 * p + 1] * lam
        o_ref[:, p * LANES:(p + 1) * LANES] = (
            _rms_rows(o, g_ref[...]) * (1.0 - lambda_init)).astype(o_ref.dtype)


def _attention(name, units, finish, shift, q, k, v, *, groups, q_width, k_width, v_width, o_width, tq,
               sums_on_mxu=False, extra=()):
    b, s, _ = q.shape
    kv_len = k.shape[1]
    tq = min(tq, s)
    grid_spec = pltpu.PrefetchScalarGridSpec(
        num_scalar_prefetch=1,
        grid=(b, groups, s // tq),
        in_specs=[_const_spec(e.shape) for e in extra] + [
            pl.BlockSpec((None, tq, q_width), lambda bi, gi, qi, _: (bi, qi, gi)),
            pl.BlockSpec((None, kv_len, k_width), lambda bi, gi, qi, _: (bi, 0, gi)),
            pl.BlockSpec((None, kv_len, v_width), lambda bi, gi, qi, _: (bi, 0, gi)),
        ],
        out_specs=pl.BlockSpec((None, tq, o_width), lambda bi, gi, qi, _: (bi, qi, gi)),
    )
    return pl.pallas_call(
        functools.partial(_attn_body, units=units, finish=finish, n_extra=len(extra),
                          sums_on_mxu=sums_on_mxu),
        name=name,
        grid_spec=grid_spec,
        out_shape=jax.ShapeDtypeStruct((b, s, groups * o_width), BF16),
        compiler_params=pltpu.CompilerParams(
            dimension_semantics=("parallel", "parallel", "arbitrary"), vmem_limit_bytes=VMEM_LIMIT),
    )(shift, *extra, q, k, v)


def _route_rows(h2, wr_ref, route_ref):
    h_hi = h2.astype(BF16)
    h_lo = (h2 - h_hi.astype(F32)).astype(BF16)
    logits = (jnp.dot(h_hi, wr_ref[0], preferred_element_type=F32)
              + (jnp.dot(h_hi, wr_ref[1], preferred_element_type=F32)
                 + jnp.dot(h_lo, wr_ref[0], preferred_element_type=F32)))
    l1 = logits.T[:N_EXPERTS, :]
    row = lax.broadcasted_iota(jnp.int32, l1.shape, 0)
    neg = jnp.float32(-jnp.inf)
    m1 = jnp.max(l1, axis=0, keepdims=True)
    i1 = jnp.min(jnp.where(l1 == m1, row, N_EXPERTS), axis=0, keepdims=True)
    l2 = jnp.where(row == i1, neg, l1)
    m2 = jnp.max(l2, axis=0, keepdims=True)
    i2 = jnp.min(jnp.where(l2 == m2, row, N_EXPERTS), axis=0, keepdims=True)
    e2 = jnp.exp(m2 - m1)
    g1 = 1.0 / (1.0 + e2)
    g2 = e2 / (1.0 + e2)
    route_t = jnp.where(row == 0, i1.astype(F32),
                        jnp.where(row == 1, i2.astype(F32),
                                  jnp.where(row == 2, g1, jnp.where(row == 3, g2, 0.0))))
    pad = jnp.zeros((LANES - N_EXPERTS, route_t.shape[1]), F32)
    route_ref[...] = jnp.concatenate([route_t, pad], axis=0).T


def _merge_body(*refs, with_router):
    if with_router:
        (x_ref, oa_ref, ob_ref, oc_ref, od_ref, g_ref, wg_ref, bg_ref, wb_ref, wo_ref, gf_ref, wr_ref,
         xo_ref, h2_ref, route_ref, hprev_ref) = refs

        @pl.when(pl.program_id(0) == 0)
        def _():
            hprev_ref[...] = jnp.zeros(hprev_ref.shape, F32)

        _route_rows(hprev_ref[...], wr_ref, route_ref)
    else:
        (x_ref, oa_ref, ob_ref, oc_ref, od_ref, g_ref, wg_ref, bg_ref, wb_ref, wo_ref, gf_ref,
         xo_ref, h2_ref) = refs
    x = x_ref[...]
    h = _rms_rows(x, g_ref[...]).astype(BF16)
    merged = None
    for b, o_ref in enumerate((oa_ref, ob_ref, oc_ref, od_ref)):
        gate = jax.nn.sigmoid(jnp.dot(h, wg_ref[b], preferred_element_type=F32) + bg_ref[b])
        term = gate * jnp.dot(o_ref[...], wb_ref[b], preferred_element_type=F32)
        merged = term if merged is None else merged + term
    x_new = x + jnp.dot(merged.astype(BF16), wo_ref[...], preferred_element_type=F32)
    xo_ref[...] = x_new
    h2 = _rms_rows(x_new, gf_ref[...])
    if with_router:
        _tile_rows_store(h2_ref, h2)
        hprev_ref[...] = h2
    else:
        h2_ref[...] = h2.astype(h2_ref.dtype)


def _merge(x2, branches, g, wg, bg, wb, wo, gf, w_router=None):
    n = x2.shape[0]
    tm = min(512, n)
    n_tiles = n // tm
    with_router = w_router is not None
    row = lambda w: pl.BlockSpec((tm, w), lambda i: (jnp.minimum(i, n_tiles - 1), 0))
    in_specs = ([row(D_MODEL)] + [row(512)] * 4
                + [_const_spec((1, D_MODEL)), _const_spec(wg.shape), _const_spec(bg.shape),
                   _const_spec(wb.shape), _const_spec(wo.shape), _const_spec((1, D_MODEL))])
    args = [x2, *branches, g, wg, bg, wb, wo, gf]
    out_specs = [row(D_MODEL), row(D_MODEL)]
    out_shape = [jax.ShapeDtypeStruct((n, D_MODEL), F32), jax.ShapeDtypeStruct((n, D_MODEL), BF16)]
    scratch_shapes = []
    if with_router:
        out_specs[1] = pl.BlockSpec((tm * ROW_TILE, LANES), lambda i: (jnp.minimum(i, n_tiles - 1), 0))
        out_shape[1] = jax.ShapeDtypeStruct((n * ROW_TILE, LANES), F32)
        in_specs.append(_const_spec(w_router.shape))
        args.append(w_router)
        out_specs.append(pl.BlockSpec((tm, LANES), lambda i: (jnp.maximum(i - 1, 0), 0)))
        out_shape.append(jax.ShapeDtypeStruct((n, LANES), F32))
        scratch_shapes.append(pltpu.VMEM((tm, D_MODEL), F32))
    return pl.pallas_call(
        functools.partial(_merge_body, with_router=with_router),
        name="merge_router" if with_router else "merge",
        grid=(n_tiles + 1 if with_router else n_tiles,),
        in_specs=in_specs, out_specs=out_specs, out_shape=out_shape, scratch_shapes=scratch_shapes,
        compiler_params=pltpu.CompilerParams(
            dimension_semantics=("arbitrary",) if with_router else ("parallel",),
            vmem_limit_bytes=VMEM_LIMIT),
    )(*args)


def _swiglu_chunk(xb, w1, w3, w2):
    a = jnp.dot(xb, w1, preferred_element_type=F32)
    b = jnp.dot(xb, w3, preferred_element_type=F32)
    return jnp.dot((jax.nn.silu(a) * b).astype(BF16), w2, preferred_element_type=F32)


def _ffn_body(x_ref, h_ref, w1_ref, w3_ref, w2_ref, o_ref):
    @pl.when(pl.program_id(1) == 0)
    def _():
        o_ref[...] = x_ref[...]

    o_ref[...] += _swiglu_chunk(h_ref[...], w1_ref[...], w3_ref[...], w2_ref[...])


def _dense_ffn(x2, h2, w1, w3, w2):
    n = x2.shape[0]
    tm = min(512, n)
    tf = 1792
    return pl.pallas_call(
        _ffn_body,
        name="dense_ffn",
        grid=(n // tm, FFN_DIM // tf),
        in_specs=[pl.BlockSpec((tm, D_MODEL), lambda i, f: (i, 0)),
                  pl.BlockSpec((tm, D_MODEL), lambda i, f: (i, 0)),
                  pl.BlockSpec((D_MODEL, tf), lambda i, f: (0, f)),
                  pl.BlockSpec((D_MODEL, tf), lambda i, f: (0, f)),
                  pl.BlockSpec((tf, D_MODEL), lambda i, f: (f, 0))],
        out_specs=pl.BlockSpec((tm, D_MODEL), lambda i, f: (i, 0)),
        out_shape=jax.ShapeDtypeStruct((n, D_MODEL), F32),
        compiler_params=pltpu.CompilerParams(dimension_semantics=("parallel", "arbitrary"),
                                             vmem_limit_bytes=VMEM_LIMIT),
    )(x2, h2, w1, w3, w2)


MOE_TF = 1792


ROW_TILE = D_MODEL // LANES


def _tile_rows_load(ref, rows):
    return jnp.concatenate([ref[pl.ds(s, rows, stride=ROW_TILE), :] for s in range(ROW_TILE)], axis=1)


def _tile_rows_store(ref, x):
    for s in range(ROW_TILE):
        ref[pl.ds(s, x.shape[0], stride=ROW_TILE), :] = x[:, s * LANES:(s + 1) * LANES]


def _moe_body(expert_ref, used_ref, src_ref, dst_ref, h_hbm, w1_ref, w3_ref, w2_ref, y_hbm,
              xg_ref, xb_ref, acc_ref, stage_ref, gsem, ssem):
    j = pl.program_id(0)
    f = pl.program_id(1)
    n_f = pl.num_programs(1)
    slot = j % 2
    other = 1 - slot
    rows_per_step = MOE_ROWS // (FFN_DIM // MOE_TF)

    def tile_of(r):
        first = r * ROW_TILE
        return pl.ds(first if isinstance(r, int) else pl.multiple_of(first, ROW_TILE), ROW_TILE)

    def gather_copy(block, r, buf):
        src = pl.multiple_of(src_ref[block * MOE_ROWS + r], ROW_TILE)
        return pltpu.make_async_copy(h_hbm.at[pl.ds(src, ROW_TILE), :], xg_ref.at[buf, tile_of(r), :],
                                     gsem.at[buf])

    def scatter_copy(block, r, buf):
        dst = pl.multiple_of(dst_ref[(block + 1) * MOE_ROWS + r], ROW_TILE)
        return pltpu.make_async_copy(stage_ref.at[buf, tile_of(r), :], y_hbm.at[pl.ds(dst, ROW_TILE), :],
                                     ssem)

    def wait_rows(copy):
        for _ in range(MOE_ROWS):
            copy.wait()

    @pl.when(jnp.logical_and(j == 0, f == 0))
    def _():
        def start(r, c):
            gather_copy(0, r, 0).start()
            return c
        lax.fori_loop(0, MOE_ROWS, start, 0)
        stage_ref[1] = jnp.zeros(stage_ref.shape[1:], F32)

    @pl.when(f == 0)
    def _():
        wait_rows(gather_copy(j, 0, slot))
        xb_ref[...] = _tile_rows_load(xg_ref.at[slot], MOE_ROWS).astype(BF16)

    for step in range(FFN_DIM // MOE_TF):
        @pl.when(f == step)
        def _():
            for r in range(step * rows_per_step, (step + 1) * rows_per_step):
                gather_copy(j + 1, r, other).start()
                scatter_copy(j - 1, r, other).start()

    @pl.when(f == 0)
    def _():
        acc_ref[...] = jnp.zeros((MOE_ROWS, D_MODEL), F32)

    @pl.when(used_ref[j] > 0)
    def _():
        acc_ref[...] += _swiglu_chunk(xb_ref[...], w1_ref[...], w3_ref[...], w2_ref[...])

    @pl.when(f == n_f - 1)
    def _():
        wait_rows(scatter_copy(j - 1, 0, other))
        _tile_rows_store(stage_ref.at[slot], acc_ref[...])

    @pl.when(jnp.logical_and(j == pl.num_programs(0) - 1, f == n_f - 1))
    def _():
        def start(r, c):
            scatter_copy(j, r, slot).start()
            return c
        lax.fori_loop(0, MOE_ROWS, start, 0)
        wait_rows(scatter_copy(j, 0, slot))
        wait_rows(gather_copy(j + 1, 0, other))


def _moe_ffn(h2_tiles, block_expert, block_used, slot_src, slot_dst, w1, w3, w2):
    n = h2_tiles.shape[0] // ROW_TILE
    n_blocks = block_expert.shape[0]
    wspec = lambda shape, imap: pl.BlockSpec(shape, imap)
    grid_spec = pltpu.PrefetchScalarGridSpec(
        num_scalar_prefetch=4,
        grid=(n_blocks, FFN_DIM // MOE_TF),
        in_specs=[pl.BlockSpec(memory_space=pl.ANY),
                  wspec((None, D_MODEL, MOE_TF), lambda j, f, e, u, s, d: (e[j], 0, f)),
                  wspec((None, D_MODEL, MOE_TF), lambda j, f, e, u, s, d: (e[j], 0, f)),
                  wspec((None, MOE_TF, D_MODEL), lambda j, f, e, u, s, d: (e[j], f, 0))],
        out_specs=pl.BlockSpec(memory_space=pl.ANY),
        scratch_shapes=[pltpu.VMEM((2, MOE_ROWS * ROW_TILE, LANES), F32),
                        pltpu.VMEM((MOE_ROWS, D_MODEL), BF16),
                        pltpu.VMEM((MOE_ROWS, D_MODEL), F32),
                        pltpu.VMEM((2, MOE_ROWS * ROW_TILE, LANES), F32),
                        pltpu.SemaphoreType.DMA((2,)),
                        pltpu.SemaphoreType.DMA(())],
    )
    return pl.pallas_call(
        _moe_body,
        name="moe_ffn",
        grid_spec=grid_spec,
        out_shape=jax.ShapeDtypeStruct(((TOP_K * n + MOE_ROWS) * ROW_TILE, LANES), F32),
        compiler_params=pltpu.CompilerParams(dimension_semantics=("arbitrary", "arbitrary"),
                                             vmem_limit_bytes=VMEM_LIMIT),
    )(block_expert, block_used, slot_src, slot_dst, h2_tiles, w1, w3, w2)


def _combine_body(x_ref, y1_ref, y2_ref, route_ref, o_ref):
    g1 = route_ref[:, 2:3]
    g2 = route_ref[:, 3:4]
    rows = x_ref.shape[0]
    o_ref[...] = x_ref[...] + (_tile_rows_load(y1_ref, rows) * g1 + _tile_rows_load(y2_ref, rows) * g2)


def _combine(x2, y, route):
    n = x2.shape[0]
    tm = min(512, n)
    row = lambda w: pl.BlockSpec((tm, w), lambda i: (i, 0))
    return pl.pallas_call(
        _combine_body,
        name="moe_combine",
        grid=(n // tm,),
        in_specs=[row(D_MODEL), pl.BlockSpec((tm * ROW_TILE, LANES), lambda i: (i, 0)),
                  pl.BlockSpec((tm * ROW_TILE, LANES), lambda i: (i + n // tm, 0)),
                  row(LANES)],
        out_specs=row(D_MODEL),
        out_shape=jax.ShapeDtypeStruct((n, D_MODEL), F32),
        compiler_params=pltpu.CompilerParams(dimension_semantics=("parallel",),
                                             vmem_limit_bytes=VMEM_LIMIT),
    )(x2, y, y, route)


def _invert_slots_body(slot_ref, init_hbm, out_ref, sem):
    copy = pltpu.make_async_copy(init_hbm, out_ref, sem)
    copy.start()
    copy.wait()

    def place(a, c):
        out_ref[slot_ref[a]] = a
        return c
    lax.fori_loop(0, slot_ref.shape[0], place, 0, unroll=8)


def _invert_slots(slot, init):
    smem = pl.BlockSpec(memory_space=pltpu.SMEM)
    return pl.pallas_call(
        _invert_slots_body,
        name="invert_slots",
        in_specs=[smem, pl.BlockSpec(memory_space=pl.ANY)],
        out_specs=smem,
        out_shape=jax.ShapeDtypeStruct(init.shape, jnp.int32),
        scratch_shapes=[pltpu.SemaphoreType.DMA(())],
    )(slot, init)


def _route_plan(route, n):
    n_assign = n * TOP_K
    flat_e = route[:, :TOP_K].astype(jnp.int32).reshape(-1)
    onehot = (flat_e[:, None] == jnp.arange(N_EXPERTS, dtype=jnp.int32)[None, :]).astype(jnp.int32)
    ranks = jnp.cumsum(onehot, axis=0) - onehot
    counts = jnp.sum(onehot, axis=0)
    rank = jnp.sum(ranks * onehot, axis=1)
    padded = (counts + MOE_ROWS - 1) // MOE_ROWS * MOE_ROWS
    pad_end = jnp.cumsum(padded)
    pad_start = pad_end - padded
    slot = pad_start[flat_e] + rank
    n_blocks = (n_assign + N_EXPERTS * (MOE_ROWS - 1)) // MOE_ROWS
    pad_ids = n_assign + jnp.arange(MOE_ROWS, dtype=jnp.int32)
    slot_assign = _invert_slots(slot, jnp.tile(pad_ids, n_blocks))
    valid = slot_assign < n_assign
    slot_src = jnp.where(valid, slot_assign // TOP_K, 0)
    slot_dst = jnp.where(valid, (slot_assign % TOP_K) * n + slot_assign // TOP_K, slot_assign)
    block_start = jnp.arange(n_blocks, dtype=jnp.int32) * MOE_ROWS
    block_expert = jnp.minimum(jnp.searchsorted(pad_end, block_start, side='right'),
                               N_EXPERTS - 1).astype(jnp.int32)
    block_used = (block_start < pad_end[-1]).astype(jnp.int32)
    return (block_expert, block_used,
            jnp.concatenate([slot_src, jnp.zeros((MOE_ROWS,), jnp.int32)]) * ROW_TILE,
            jnp.concatenate([pad_ids, slot_dst]) * ROW_TILE)


def _inv_freq(dim):
    return 1.0 / (ROPE_THETA ** (jnp.arange(0, dim, 2, dtype=F32) / dim))


def _rope_table(pos_of_lane, freq_of_lane, first_half, active=None):
    ang = pos_of_lane * freq_of_lane[None, :]
    cos, sin = jnp.cos(ang), jnp.sin(ang) * jnp.where(first_half, -1.0, 1.0)[None, :]
    if active is not None:
        cos = jnp.where(active[None, :], cos, 1.0)
        sin = jnp.where(active[None, :], sin, 0.0)
    return jnp.concatenate([cos, sin], axis=1).astype(F32)


def _rope_tables(seq, mem_len):
    lane = np.arange(LANES)
    pos = jnp.arange(seq, dtype=jnp.int32)
    posf = jnp.broadcast_to(pos.astype(F32)[:, None], (seq, LANES))
    tab_a = _rope_table(posf, _inv_freq(64)[lane % 32], (lane % 64) < 32)
    tab_b = _rope_table(posf, _inv_freq(32)[lane % 16], (lane % 32) < 16, (lane >= 64) & (lane < 96))
    rows = (pos // MEM_GRID_W).astype(F32)[:, None]
    cols = (pos % MEM_GRID_W).astype(F32)[:, None]
    pos_c = jnp.where(((lane % 64) < 32)[None, :], rows, cols)
    tab_c = _rope_table(pos_c, _inv_freq(32)[lane % 16], (lane % 32) < 16)
    f_d = _inv_freq(128)[lane % 64]
    tab_d = _rope_table(posf, f_d, lane < 64)
    mposf = jnp.broadcast_to(jnp.arange(mem_len, dtype=F32)[:, None], (mem_len, LANES))
    tab_m = _rope_table(mposf, f_d, lane < 64)
    return jnp.concatenate([tab_a, tab_b, tab_c, tab_d], axis=1), tab_m


def _score_bound(q_gain, k_gain, dim):
    gq = jnp.max(jnp.abs(q_gain.astype(F32)))
    gk = jnp.max(jnp.abs(k_gain.astype(F32)))
    return (1.02 * LOG2_E * dim ** 0.5 * gq * gk).reshape(1)


def _segment_matrix(width):
    i = np.arange(MXU_DIM)
    return jnp.asarray((i[:, None] // width) == (i[None, :] // width), dtype=BF16)


def _tile_row(g, reps):
    return jnp.tile(g.astype(F32), reps).reshape(1, -1)


def _layer_params(layer, w_in, w_mla_q_up, w_mla_kv_up, w_mem_kv, gains):
    w = w_in[layer]
    z = lambda c: jnp.zeros((D_MODEL, c), F32)
    gk, gv = w[:, 2464:2592], w[:, 2592:2720]
    dup = lambda m: jnp.concatenate([m[:, :64], m[:, :64], m[:, 64:], m[:, 64:]], axis=1)
    w_big = jnp.concatenate([
        w[:, 0:1536], w[:, 1536:1792], w[:, 1792:1920],
        z(64), w[:, 1920:1952], z(32),
        w[:, 1952:2464], dup(gk), dup(gv), w[:, 2720:3232]], axis=1)
    wq = w_mla_q_up[layer].reshape(256, MLA_HEADS, MLA_QK_DIM)
    wq = jnp.pad(wq, ((0, 0), (0, 0), (0, LANES - MLA_QK_DIM))).reshape(256, MLA_HEADS * LANES)
    wkv = w_mla_kv_up[layer].reshape(128, MLA_HEADS, 128)
    wk = jnp.pad(wkv[:, :, :64], ((0, 0), (0, 0), (0, 64))).reshape(128, MLA_HEADS * LANES)
    wv = wkv[:, :, 64:].reshape(128, MLA_HEADS * 64)
    wm = w_mem_kv[layer].reshape(D_MODEL, 4, 256)
    w_mem = jnp.concatenate([wm[:, :, :128].reshape(D_MODEL, 512), wm[:, :, 128:].reshape(D_MODEL, 512)], axis=1)
    pad96 = lambda g: jnp.pad(g.astype(F32), (0, LANES - MLA_QK_DIM)).reshape(1, LANES)
    (dq_g, dk_g, ql_g, kvl_g, mq_g, mk_g, gq_g, gk_g, memq_g) = gains
    gain_rows = jnp.concatenate([
        _tile_row(dq_g, 2), _tile_row(dk_g, 2), ql_g.astype(F32).reshape(2, LANES),
        kvl_g.astype(F32).reshape(1, LANES), pad96(mq_g), pad96(mk_g),
        _tile_row(gq_g, 2), _tile_row(gk_g, 2), memq_g.astype(F32).reshape(1, LANES),
        jnp.zeros((6, LANES), F32)], axis=0)
    return (w_big.astype(BF16), wq.astype(BF16), jnp.concatenate([wk, wv], axis=1).astype(BF16),
            w_mem.astype(BF16), gain_rows)


def kernel(x, mem, norm_attn_g, norm_mem_g, w_in, diff_q_norm_g, diff_k_norm_g, diff_lambda, diff_subln_g,
           mla_q_lat_g, mla_kv_lat_g, w_mla_q_up, w_mla_kv_up, mla_q_norm_g, mla_k_norm_g, gqa_q_norm_g,
           gqa_k_norm_g, w_mem_kv, mem_q_norm_g, mem_k_norm_g, w_branch, w_gate, b_gate, w_out, norm_ffn_g,
           dense_w1, dense_w3, dense_w2, moe_router, moe_w1, moe_w3, moe_w2):
    batch, seq, _ = x.shape
    mem_len = mem.shape[1]
    n = batch * seq
    depth = w_in.shape[0]
    tab, tab_m = _rope_tables(seq, mem_len)
    seg64, seg128 = _segment_matrix(64), _segment_matrix(128)
    x2 = x.reshape(n, D_MODEL)
    mem2 = mem.reshape(batch * mem_len, D_MODEL)
    row = lambda g: g.astype(F32).reshape(1, -1)
    b3 = lambda a, w: a.reshape(batch, -1, w)

    for layer in range(depth):
        lambda_init = 0.8 - 0.6 * math.exp(-0.3 * layer)
        w_big, wq_up, wkv_up, w_mem, gain_rows = _layer_params(
            layer, w_in, w_mla_q_up, w_mla_kv_up, w_mem_kv,
            (diff_q_norm_g[layer], diff_k_norm_g[layer], mla_q_lat_g[layer], mla_kv_lat_g[layer],
             mla_q_norm_g[layer], mla_k_norm_g[layer], gqa_q_norm_g[layer], gqa_k_norm_g[layer],
             mem_q_norm_g[layer]))
        qa, ka, va, qb, kb, vb, qc, kc, vc, qd = _inproj(
            x2, row(norm_attn_g[layer]), w_big, wq_up, wkv_up, seg64, seg128, gain_rows, tab, seq)
        kd, vd = _memprep(mem2, row(norm_mem_g[layer]), w_mem, seg128, row(mem_k_norm_g[layer]), tab_m, mem_len)

        o_a = _attention("attn_diff", [(0, 0, 0, 0), (0, 1, 0, 0), (128, 0, 128, 128), (128, 1, 128, 128)],
                         functools.partial(_finish_diff, lambda_init=lambda_init),
                         _score_bound(diff_q_norm_g[layer], diff_k_norm_g[layer], 64),
                         b3(qa, 512), b3(ka, 512), b3(va, 512), groups=2, q_width=256,
                         k_width=256, v_width=256, o_width=256, tq=1024,
                         extra=(diff_lambda[layer].astype(F32), row(diff_subln_g[layer])))
        o_b = _attention("attn_latent", [(128 * u, None, 128 * u, 128 * (u // 2)) for u in range(4)],
                         _finish_pairs, _score_bound(mla_q_norm_g[layer], mla_k_norm_g[layer], MLA_QK_DIM),
                         b3(qb, 1024), b3(kb, 1024), b3(vb, 512), groups=2,
                         q_width=512, k_width=512, v_width=256, o_width=256, tq=1024, sums_on_mxu=True)
        o_c = _attention("attn_gqa", [(0, 0, 0, 0), (0, 1, 0, 0), (128, 0, 0, 0), (128, 1, 0, 0)],
                         _finish_pairs, _score_bound(gqa_q_norm_g[layer], gqa_k_norm_g[layer], 64),
                         b3(qc, 512), b3(kc, 256), b3(vc, 256), groups=2,
                         q_width=256, k_width=128, v_width=128, o_width=256, tq=1024, sums_on_mxu=True)
        o_d = _attention("attn_mem", [(128 * u, None, 128 * u, 128 * u) for u in range(4)],
                         _finish_tiles, _score_bound(mem_q_norm_g[layer], mem_k_norm_g[layer], 128),
                         b3(qd, 512), b3(kd, 512), b3(vd, 512), groups=1,
                         q_width=512, k_width=512, v_width=512, o_width=512, tq=512)
        branches = [o.reshape(n, 512) for o in (o_a, o_b, o_c, o_d)]

        i = layer // 2
        routed = layer % 2 == 1
        w_router = None
        if routed:
            w_r = jnp.pad(moe_router[i].astype(F32), ((0, 0), (0, LANES - N_EXPERTS)))
            w_r_hi = w_r.astype(BF16)
            w_router = jnp.stack([w_r_hi, (w_r - w_r_hi.astype(F32)).astype(BF16)])
        merged = _merge(x2, branches, row(norm_attn_g[layer]), w_gate[layer].astype(BF16),
                        b_gate[layer].astype(F32).reshape(4, 1, D_MODEL), w_branch[layer].astype(BF16),
                        w_out[layer].astype(BF16), row(norm_ffn_g[layer]), w_router)
        if not routed:
            x_mid, h2 = merged
            x2 = _dense_ffn(x_mid, h2, dense_w1[i].astype(BF16), dense_w3[i].astype(BF16),
                            dense_w2[i].astype(BF16))
        else:
            x_mid, h2, route = merged
            block_expert, block_used, slot_src, slot_dst = _route_plan(route, n)
            y = _moe_ffn(h2, block_expert, block_used, slot_src, slot_dst, moe_w1[i].astype(BF16),
                         moe_w3[i].astype(BF16), moe_w2[i].astype(BF16))
            x2 = _combine(x_mid, y, route)
    return x2.reshape(batch, seq, D_MODEL)
```

```python
import functools
import math

import numpy as np
import jax
import jax.numpy as jnp
from jax import lax
from jax.experimental import pallas as pl
from jax.experimental.pallas import tpu as pltpu

F32 = jnp.float32
BF16 = jnp.bfloat16

D_MODEL = 1024
MEM_GRID_W = 64
ROPE_THETA = 10000.0
NORM_EPS = 1e-6
LOG2_E = math.log2(math.e)
DIFF_HEADS = 4
MLA_HEADS = 8
MLA_QK_DIM = 96
FFN_DIM = 3584
N_EXPERTS = 8
TOP_K = 2
MOE_ROWS = 512
LANES = 128
MXU_DIM = 256
VMEM_LIMIT = 56 * 1024 * 1024

C_DQ, C_DK, C_DV = 0, 512, 1024
C_MQL, C_MKVL, C_KROPE = 1536, 1792, 1920
C_GQ, C_GK, C_GV, C_MEMQ = 2048, 2560, 2816, 3072
IN_COLS_PADDED = 3584
T_A, T_B, T_C, T_D = 0, 256, 512, 768


def _const_spec(shape):
    return pl.BlockSpec(shape, lambda *_: (0,) * len(shape), pipeline_mode=pl.Buffered(1))


def _rms_rows(x, g):
    ms = jnp.mean(x * x, axis=-1, keepdims=True)
    return x * lax.rsqrt(ms + NORM_EPS) * g


def _segment_mean_sq(y, seg_ref, inv_count):
    sq = (y * y).astype(BF16)
    parts = [jnp.dot(sq[:, c:c + MXU_DIM], seg_ref[...], preferred_element_type=F32)
             for c in range(0, y.shape[1], MXU_DIM)]
    ss = parts[0] if len(parts) == 1 else jnp.concatenate(parts, axis=1)
    return ss * inv_count


def _rope_lanes(y, cos, sin_signed, half):
    lane = lax.broadcasted_iota(jnp.int32, y.shape, 1)
    first = (lane % (2 * half)) < half
    rot = jnp.where(first, pltpu.roll(y, LANES - half, 1), pltpu.roll(y, half, 1))
    return y * cos + rot * sin_signed


def _norm_rope_store(y, ms, gain, tab_ref, t_off, half, scale, out_ref):
    yn = y * lax.rsqrt(ms + NORM_EPS)
    cos = tab_ref[:, t_off:t_off + LANES]
    sin = tab_ref[:, t_off + LANES:t_off + 2 * LANES]
    for c in range(0, y.shape[1], LANES):
        r = _rope_lanes(yn[:, c:c + LANES] * gain, cos, sin, half)
        if scale != 1.0:
            r = r * scale
        out_ref[:, c:c + LANES] = r.astype(out_ref.dtype)


def _inproj_body(x_ref, g_ref, w_ref, wq_ref, wkv_ref, seg64_ref, seg128_ref, gains_ref, tab_ref,
                 qa_ref, ka_ref, va_ref, qb_ref, kb_ref, vb_ref, qc_ref, kc_ref, vc_ref, qd_ref):
    h = _rms_rows(x_ref[...], g_ref[...]).astype(BF16)

    def proj(c0, width):
        return jnp.dot(h, w_ref[:, c0:c0 + width], preferred_element_type=F32)

    def gain(row):
        return gains_ref[row:row + 1, :]

    y = proj(C_DQ, 512)
    _norm_rope_store(y, _segment_mean_sq(y, seg64_ref, 1.0 / 64), gain(0), tab_ref, T_A, 32,
                     LOG2_E * 64 ** -0.5, qa_ref)
    y = proj(C_DK, 512)
    _norm_rope_store(y, _segment_mean_sq(y, seg64_ref, 1.0 / 64), gain(1), tab_ref, T_A, 32, 1.0, ka_ref)
    va_ref[...] = proj(C_DV, 512).astype(BF16)

    g_ql = jnp.concatenate([gain(2), gain(3)], axis=1)
    ql = _rms_rows(proj(C_MQL, 256), g_ql).astype(BF16)
    y = jnp.dot(ql, wq_ref[...], preferred_element_type=F32)
    _norm_rope_store(y, _segment_mean_sq(y, seg128_ref, 1.0 / MLA_QK_DIM), gain(5), tab_ref, T_B, 16,
                     LOG2_E * MLA_QK_DIM ** -0.5, qb_ref)
    kvl = _rms_rows(proj(C_MKVL, 128), gain(4)).astype(BF16)
    kv = jnp.dot(kvl, wkv_ref[...], preferred_element_type=F32)
    k_rope = proj(C_KROPE, 128)
    y = kv[:, :1024] + jnp.concatenate([k_rope] * MLA_HEADS, axis=1)
    _norm_rope_store(y, _segment_mean_sq(y, seg128_ref, 1.0 / MLA_QK_DIM), gain(6), tab_ref, T_B, 16,
                     1.0, kb_ref)
    vb_ref[...] = kv[:, 1024:].astype(BF16)

    y = proj(C_GQ, 512)
    _norm_rope_store(y, _segment_mean_sq(y, seg64_ref, 1.0 / 64), gain(7), tab_ref, T_C, 16,
                     LOG2_E * 64 ** -0.5, qc_ref)
    y = proj(C_GK, 256)
    _norm_rope_store(y, _segment_mean_sq(y, seg64_ref, 1.0 / 64), gain(8), tab_ref, T_C, 16, 1.0, kc_ref)
    vc_ref[...] = proj(C_GV, 256).astype(BF16)

    y = proj(C_MEMQ, 512)
    _norm_rope_store(y, _segment_mean_sq(y, seg128_ref, 1.0 / 128), gain(9), tab_ref, T_D, 64,
                     LOG2_E * 128 ** -0.5, qd_ref)


def _inproj(x2, g, w_big, wq_up, wkv_up, seg64, seg128, gains, tab, seq):
    n = x2.shape[0]
    tm = min(1024, seq)
    tiles_per_seq = seq // tm
    row = lambda w: pl.BlockSpec((tm, w), lambda i: (i, 0))
    out_widths = (512, 512, 512, 1024, 1024, 512, 512, 256, 256, 512)
    return pl.pallas_call(
        _inproj_body,
        name="inproj",
        grid=(n // tm,),
        in_specs=[row(D_MODEL), _const_spec((1, D_MODEL)), _const_spec(w_big.shape),
                  _const_spec(wq_up.shape), _const_spec(wkv_up.shape), _const_spec(seg64.shape),
                  _const_spec(seg128.shape), _const_spec(gains.shape),
                  pl.BlockSpec((tm, tab.shape[1]), lambda i: (i % tiles_per_seq, 0))],
        out_specs=[row(w) for w in out_widths],
        out_shape=[jax.ShapeDtypeStruct((n, w), BF16) for w in out_widths],
        compiler_params=pltpu.CompilerParams(dimension_semantics=("parallel",),
                                             vmem_limit_bytes=VMEM_LIMIT),
    )(x2, g, w_big, wq_up, wkv_up, seg64, seg128, gains, tab)


def _memprep_body(m_ref, g_ref, w_ref, seg128_ref, gain_ref, tab_ref, k_ref, v_ref):
    mn = _rms_rows(m_ref[...], g_ref[...]).astype(BF16)
    kv = jnp.dot(mn, w_ref[...], preferred_element_type=F32)
    y = kv[:, :512]
    _norm_rope_store(y, _segment_mean_sq(y, seg128_ref, 1.0 / 128), gain_ref[...], tab_ref, 0, 64, 1.0, k_ref)
    v_ref[...] = kv[:, 512:].astype(BF16)


def _memprep(mem2, g, w_kv, seg128, gain, tab_m, mem_len):
    n = mem2.shape[0]
    row = lambda w: pl.BlockSpec((mem_len, w), lambda i: (i, 0))
    return pl.pallas_call(
        _memprep_body,
        name="memprep",
        grid=(n // mem_len,),
        in_specs=[row(D_MODEL), _const_spec((1, D_MODEL)), _const_spec(w_kv.shape),
                  _const_spec(seg128.shape), _const_spec((1, LANES)), _const_spec(tab_m.shape)],
        out_specs=[row(512), row(512)],
        out_shape=[jax.ShapeDtypeStruct((n, 512), BF16)] * 2,
        compiler_params=pltpu.CompilerParams(dimension_semantics=("parallel",),
                                             vmem_limit_bytes=VMEM_LIMIT),
    )(mem2, g, w_kv, seg128, gain, tab_m)


SAFE_SHIFT_LIMIT = 60.0


def _attend(q, k, v, sums_on_mxu, shift=None):
    s = lax.dot_general(q, k, (((1,), (1,)), ((), ())), preferred_element_type=F32)
    e = jnp.exp2(s - (jnp.max(s, axis=-1, keepdims=True) if shift is None else shift))
    if not sums_on_mxu:
        l = jnp.sum(e, axis=-1, keepdims=True)
        return jnp.dot(e.astype(BF16), v, preferred_element_type=F32) * (1.0 / l)
    v_ones = jnp.concatenate([v, jnp.ones((v.shape[0], LANES), BF16)], axis=1)
    ov = jnp.dot(e.astype(BF16), v_ones, preferred_element_type=F32)
    return ov[:, :LANES] * (1.0 / ov[:, LANES:LANES + 1])


def _split_halves(q):
    lo = lax.broadcasted_iota(jnp.int32, q.shape, 1) < 64
    zero = jnp.zeros_like(q)
    return jnp.where(lo, q, zero), jnp.where(lo, zero, q)


def _join_halves(o_lo, o_hi):
    lo = lax.broadcasted_iota(jnp.int32, o_lo.shape, 1) < 64
    return jnp.where(lo, o_lo, o_hi)


def _attn_body(shift_ref, *refs, units, finish, n_extra, sums_on_mxu):
    extra = refs[:n_extra]
    q_ref, k_ref, v_ref, o_ref = refs[n_extra:]
    shift = shift_ref[0]

    def run(shift):
        halves = {}
        outs = []
        for q_lane, half, k_lane, v_lane in units:
            q = q_ref[:, q_lane:q_lane + LANES]
            if half is not None:
                if q_lane not in halves:
                    halves[q_lane] = _split_halves(q)
                q = halves[q_lane][half]
            outs.append(_attend(q, k_ref[:, k_lane:k_lane + LANES], v_ref[:, v_lane:v_lane + LANES],
                                sums_on_mxu, shift))
        finish(outs, o_ref, *extra)

    @pl.when(shift <= SAFE_SHIFT_LIMIT)
    def _():
        run(shift)

    @pl.when(jnp.logical_not(shift <= SAFE_SHIFT_LIMIT))
    def _():
        run(None)


def _finish_pairs(outs, o_ref):
    for p in range(len(outs) // 2):
        o_ref[:, p * LANES:(p + 1) * LANES] = _join_halves(outs[2 * p], outs[2 * p + 1]).astype(o_ref.dtype)


def _finish_tiles(outs, o_ref):
    for p, o in enumerate(outs):
        o_ref[:, p * LANES:(p + 1) * LANES] = o.astype(o_ref.dtype)


def _finish_diff(outs, o_ref, lam_ref, g_ref, *, lambda_init):
    lp = lam_ref[...]
    lam = (jnp.exp(jnp.sum(lp[0:1] * lp[1:2], keepdims=True))
           - jnp.exp(jnp.sum(lp[2:3] * lp[3:4], keepdims=True)) + lambda_init)
    for p in range(len(outs) // 2):
        o = outs[2 * p] - outs[2 * p + 1] * lam
        o_ref[:, p * LANES:(p + 1) * LANES] = (
            _rms_rows(o, g_ref[...]) * (1.0 - lambda_init)).astype(o_ref.dtype)


def _attention(name, units, finish, shift, q, k, v, *, groups, q_width, k_width, v_width, o_width, tq,
               sums_on_mxu=False, extra=()):
    b, s, _ = q.shape
    kv_len = k.shape[1]
    tq = min(tq, s)
    grid_spec = pltpu.PrefetchScalarGridSpec(
        num_scalar_prefetch=1,
        grid=(b, groups, s // tq),
        in_specs=[_const_spec(e.shape) for e in extra] + [
            pl.BlockSpec((None, tq, q_width), lambda bi, gi, qi, _: (bi, qi, gi)),
            pl.BlockSpec((None, kv_len, k_width), lambda bi, gi, qi, _: (bi, 0, gi)),
            pl.BlockSpec((None, kv_len, v_width), lambda bi, gi, qi, _: (bi, 0, gi)),
        ],
        out_specs=pl.BlockSpec((None, tq, o_width), lambda bi, gi, qi, _: (bi, qi, gi)),
    )
    return pl.pallas_call(
        functools.partial(_attn_body, units=units, finish=finish, n_extra=len(extra),
                          sums_on_mxu=sums_on_mxu),
        name=name,
        grid_spec=grid_spec,
        out_shape=jax.ShapeDtypeStruct((b, s, groups * o_width), BF16),
        compiler_params=pltpu.CompilerParams(
            dimension_semantics=("parallel", "parallel", "arbitrary"), vmem_limit_bytes=VMEM_LIMIT),
    )(shift, *extra, q, k, v)


def _route_rows(h2, wr_ref, route_ref):
    h_hi = h2.astype(BF16)
    h_lo = (h2 - h_hi.astype(F32)).astype(BF16)
    logits = (jnp.dot(h_hi, wr_ref[0], preferred_element_type=F32)
              + (jnp.dot(h_hi, wr_ref[1], preferred_element_type=F32)
                 + jnp.dot(h_lo, wr_ref[0], preferred_element_type=F32)))
    l1 = logits.T[:N_EXPERTS, :]
    row = lax.broadcasted_iota(jnp.int32, l1.shape, 0)
    neg = jnp.float32(-jnp.inf)
    m1 = jnp.max(l1, axis=0, keepdims=True)
    i1 = jnp.min(jnp.where(l1 == m1, row, N_EXPERTS), axis=0, keepdims=True)
    l2 = jnp.where(row == i1, neg, l1)
    m2 = jnp.max(l2, axis=0, keepdims=True)
    i2 = jnp.min(jnp.where(l2 == m2, row, N_EXPERTS), axis=0, keepdims=True)
    e2 = jnp.exp(m2 - m1)
    g1 = 1.0 / (1.0 + e2)
    g2 = e2 / (1.0 + e2)
    route_t = jnp.where(row == 0, i1.astype(F32),
                        jnp.where(row == 1, i2.astype(F32),
                                  jnp.where(row == 2, g1, jnp.where(row == 3, g2, 0.0))))
    pad = jnp.zeros((LANES - N_EXPERTS, route_t.shape[1]), F32)
    route_ref[...] = jnp.concatenate([route_t, pad], axis=0).T


def _merge_body(*refs, with_router):
    if with_router:
        (x_ref, oa_ref, ob_ref, oc_ref, od_ref, g_ref, wg_ref, bg_ref, wb_ref, wo_ref, gf_ref, wr_ref,
         xo_ref, h2_ref, route_ref, hprev_ref) = refs

        @pl.when(pl.program_id(0) == 0)
        def _():
            hprev_ref[...] = jnp.zeros(hprev_ref.shape, F32)

        _route_rows(hprev_ref[...], wr_ref, route_ref)
    else:
        (x_ref, oa_ref, ob_ref, oc_ref, od_ref, g_ref, wg_ref, bg_ref, wb_ref, wo_ref, gf_ref,
         xo_ref, h2_ref) = refs
    x = x_ref[...]
    h = _rms_rows(x, g_ref[...]).astype(BF16)
    merged = None
    for b, o_ref in enumerate((oa_ref, ob_ref, oc_ref, od_ref)):
        gate = jax.nn.sigmoid(jnp.dot(h, wg_ref[b], preferred_element_type=F32) + bg_ref[b])
        term = gate * jnp.dot(o_ref[...], wb_ref[b], preferred_element_type=F32)
        merged = term if merged is None else merged + term
    x_new = x + jnp.dot(merged.astype(BF16), wo_ref[...], preferred_element_type=F32)
    xo_ref[...] = x_new
    h2 = _rms_rows(x_new, gf_ref[...])
    if with_router:
        _tile_rows_store(h2_ref, h2)
        hprev_ref[...] = h2
    else:
        h2_ref[...] = h2.astype(h2_ref.dtype)


def _merge(x2, branches, g, wg, bg, wb, wo, gf, w_router=None):
    n = x2.shape[0]
    tm = min(512, n)
    n_tiles = n // tm
    with_router = w_router is not None
    row = lambda w: pl.BlockSpec((tm, w), lambda i: (jnp.minimum(i, n_tiles - 1), 0))
    in_specs = ([row(D_MODEL)] + [row(512)] * 4
                + [_const_spec((1, D_MODEL)), _const_spec(wg.shape), _const_spec(bg.shape),
                   _const_spec(wb.shape), _const_spec(wo.shape), _const_spec((1, D_MODEL))])
    args = [x2, *branches, g, wg, bg, wb, wo, gf]
    out_specs = [row(D_MODEL), row(D_MODEL)]
    out_shape = [jax.ShapeDtypeStruct((n, D_MODEL), F32), jax.ShapeDtypeStruct((n, D_MODEL), BF16)]
    scratch_shapes = []
    if with_router:
        out_specs[1] = pl.BlockSpec((tm * ROW_TILE, LANES), lambda i: (jnp.minimum(i, n_tiles - 1), 0))
        out_shape[1] = jax.ShapeDtypeStruct((n * ROW_TILE, LANES), F32)
        in_specs.append(_const_spec(w_router.shape))
        args.append(w_router)
        out_specs.append(pl.BlockSpec((tm, LANES), lambda i: (jnp.maximum(i - 1, 0), 0)))
        out_shape.append(jax.ShapeDtypeStruct((n, LANES), F32))
        scratch_shapes.append(pltpu.VMEM((tm, D_MODEL), F32))
    return pl.pallas_call(
        functools.partial(_merge_body, with_router=with_router),
        name="merge_router" if with_router else "merge",
        grid=(n_tiles + 1 if with_router else n_tiles,),
        in_specs=in_specs, out_specs=out_specs, out_shape=out_shape, scratch_shapes=scratch_shapes,
        compiler_params=pltpu.CompilerParams(
            dimension_semantics=("arbitrary",) if with_router else ("parallel",),
            vmem_limit_bytes=VMEM_LIMIT),
    )(*args)


def _swiglu_chunk(xb, w1, w3, w2):
    a = jnp.dot(xb, w1, preferred_element_type=F32)
    b = jnp.dot(xb, w3, preferred_element_type=F32)
    return jnp.dot((jax.nn.silu(a) * b).astype(BF16), w2, preferred_element_type=F32)


def _ffn_body(x_ref, h_ref, w1_ref, w3_ref, w2_ref, o_ref):
    @pl.when(pl.program_id(1) == 0)
    def _():
        o_ref[...] = x_ref[...]

    o_ref[...] += _swiglu_chunk(h_ref[...], w1_ref[...], w3_ref[...], w2_ref[...])


def _dense_ffn(x2, h2, w1, w3, w2):
    n = x2.shape[0]
    tm = min(1024, n)
    tf = 1792
    return pl.pallas_call(
        _ffn_body,
        name="dense_ffn",
        grid=(n // tm, FFN_DIM // tf),
        in_specs=[pl.BlockSpec((tm, D_MODEL), lambda i, f: (i, 0)),
                  pl.BlockSpec((tm, D_MODEL), lambda i, f: (i, 0)),
                  pl.BlockSpec((D_MODEL, tf), lambda i, f: (0, f)),
                  pl.BlockSpec((D_MODEL, tf), lambda i, f: (0, f)),
                  pl.BlockSpec((tf, D_MODEL), lambda i, f: (f, 0))],
        out_specs=pl.BlockSpec((tm, D_MODEL), lambda i, f: (i, 0)),
        out_shape=jax.ShapeDtypeStruct((n, D_MODEL), F32),
        compiler_params=pltpu.CompilerParams(dimension_semantics=("parallel", "arbitrary"),
                                             vmem_limit_bytes=VMEM_LIMIT),
    )(x2, h2, w1, w3, w2)


MOE_TF = 1792


ROW_TILE = D_MODEL // LANES


def _tile_rows_load(ref, rows):
    return jnp.concatenate([ref[pl.ds(s, rows, stride=ROW_TILE), :] for s in range(ROW_TILE)], axis=1)


def _tile_rows_store(ref, x):
    for s in range(ROW_TILE):
        ref[pl.ds(s, x.shape[0], stride=ROW_TILE), :] = x[:, s * LANES:(s + 1) * LANES]


def _moe_body(expert_ref, used_ref, src_ref, dst_ref, h_hbm, w1_ref, w3_ref, w2_ref, y_hbm,
              xg_ref, xb_ref, acc_ref, stage_ref, gsem, ssem):
    j = pl.program_id(0)
    f = pl.program_id(1)
    n_f = pl.num_programs(1)
    slot = j % 2
    other = 1 - slot
    rows_per_step = MOE_ROWS // (FFN_DIM // MOE_TF)

    def tile_of(r):
        first = r * ROW_TILE
        return pl.ds(first if isinstance(r, int) else pl.multiple_of(first, ROW_TILE), ROW_TILE)

    def gather_copy(block, r, buf):
        src = pl.multiple_of(src_ref[block * MOE_ROWS + r], ROW_TILE)
        return pltpu.make_async_copy(h_hbm.at[pl.ds(src, ROW_TILE), :], xg_ref.at[buf, tile_of(r), :],
                                     gsem.at[buf])

    def scatter_copy(block, r, buf):
        dst = pl.multiple_of(dst_ref[(block + 1) * MOE_ROWS + r], ROW_TILE)
        return pltpu.make_async_copy(stage_ref.at[buf, tile_of(r), :], y_hbm.at[pl.ds(dst, ROW_TILE), :],
                                     ssem)

    def wait_rows(copy):
        for _ in range(MOE_ROWS):
            copy.wait()

    @pl.when(jnp.logical_and(j == 0, f == 0))
    def _():
        def start(r, c):
            gather_copy(0, r, 0).start()
            return c
        lax.fori_loop(0, MOE_ROWS, start, 0)
        stage_ref[1] = jnp.zeros(stage_ref.shape[1:], F32)

    @pl.when(f == 0)
    def _():
        wait_rows(gather_copy(j, 0, slot))
        xb_ref[...] = _tile_rows_load(xg_ref.at[slot], MOE_ROWS).astype(BF16)

    for step in range(FFN_DIM // MOE_TF):
        @pl.when(f == step)
        def _():
            for r in range(step * rows_per_step, (step + 1) * rows_per_step):
                gather_copy(j + 1, r, other).start()
                scatter_copy(j - 1, r, other).start()

    @pl.when(f == 0)
    def _():
        acc_ref[...] = jnp.zeros((MOE_ROWS, D_MODEL), F32)

    @pl.when(used_ref[j] > 0)
    def _():
        acc_ref[...] += _swiglu_chunk(xb_ref[...], w1_ref[...], w3_ref[...], w2_ref[...])

    @pl.when(f == n_f - 1)
    def _():
        wait_rows(scatter_copy(j - 1, 0, other))
        _tile_rows_store(stage_ref.at[slot], acc_ref[...])

    @pl.when(jnp.logical_and(j == pl.num_programs(0) - 1, f == n_f - 1))
    def _():
        def start(r, c):
            scatter_copy(j, r, slot).start()
            return c
        lax.fori_loop(0, MOE_ROWS, start, 0)
        wait_rows(scatter_copy(j, 0, slot))
        wait_rows(gather_copy(j + 1, 0, other))


def _moe_ffn(h2_tiles, block_expert, block_used, slot_src, slot_dst, w1, w3, w2):
    n = h2_tiles.shape[0] // ROW_TILE
    n_blocks = block_expert.shape[0]
    wspec = lambda shape, imap: pl.BlockSpec(shape, imap)
    grid_spec = pltpu.PrefetchScalarGridSpec(
        num_scalar_prefetch=4,
        grid=(n_blocks, FFN_DIM // MOE_TF),
        in_specs=[pl.BlockSpec(memory_space=pl.ANY),
                  wspec((None, D_MODEL, MOE_TF), lambda j, f, e, u, s, d: (e[j], 0, f)),
                  wspec((None, D_MODEL, MOE_TF), lambda j, f, e, u, s, d: (e[j], 0, f)),
                  wspec((None, MOE_TF, D_MODEL), lambda j, f, e, u, s, d: (e[j], f, 0))],
        out_specs=pl.BlockSpec(memory_space=pl.ANY),
        scratch_shapes=[pltpu.VMEM((2, MOE_ROWS * ROW_TILE, LANES), F32),
                        pltpu.VMEM((MOE_ROWS, D_MODEL), BF16),
                        pltpu.VMEM((MOE_ROWS, D_MODEL), F32),
                        pltpu.VMEM((2, MOE_ROWS * ROW_TILE, LANES), F32),
                        pltpu.SemaphoreType.DMA((2,)),
                        pltpu.SemaphoreType.DMA(())],
    )
    return pl.pallas_call(
        _moe_body,
        name="moe_ffn",
        grid_spec=grid_spec,
        out_shape=jax.ShapeDtypeStruct(((TOP_K * n + MOE_ROWS) * ROW_TILE, LANES), F32),
        compiler_params=pltpu.CompilerParams(dimension_semantics=("arbitrary", "arbitrary"),
                                             vmem_limit_bytes=VMEM_LIMIT),
    )(block_expert, block_used, slot_src, slot_dst, h2_tiles, w1, w3, w2)


def _combine_body(x_ref, y1_ref, y2_ref, route_ref, o_ref):
    g1 = route_ref[:, 2:3]
    g2 = route_ref[:, 3:4]
    rows = x_ref.shape[0]
    o_ref[...] = x_ref[...] + (_tile_rows_load(y1_ref, rows) * g1 + _tile_rows_load(y2_ref, rows) * g2)


def _combine(x2, y, route):
    n = x2.shape[0]
    tm = min(512, n)
    row = lambda w: pl.BlockSpec((tm, w), lambda i: (i, 0))
    return pl.pallas_call(
        _combine_body,
        name="moe_combine",
        grid=(n // tm,),
        in_specs=[row(D_MODEL), pl.BlockSpec((tm * ROW_TILE, LANES), lambda i: (i, 0)),
                  pl.BlockSpec((tm * ROW_TILE, LANES), lambda i: (i + n // tm, 0)),
                  row(LANES)],
        out_specs=row(D_MODEL),
        out_shape=jax.ShapeDtypeStruct((n, D_MODEL), F32),
        compiler_params=pltpu.CompilerParams(dimension_semantics=("parallel",),
                                             vmem_limit_bytes=VMEM_LIMIT),
    )(x2, y, y, route)


def _invert_slots_body(slot_ref, init_hbm, out_ref, sem):
    copy = pltpu.make_async_copy(init_hbm, out_ref, sem)
    copy.start()
    copy.wait()

    def place(a, c):
        out_ref[slot_ref[a]] = a
        return c
    lax.fori_loop(0, slot_ref.shape[0], place, 0, unroll=8)


def _invert_slots(slot, init):
    smem = pl.BlockSpec(memory_space=pltpu.SMEM)
    return pl.pallas_call(
        _invert_slots_body,
        name="invert_slots",
        in_specs=[smem, pl.BlockSpec(memory_space=pl.ANY)],
        out_specs=smem,
        out_shape=jax.ShapeDtypeStruct(init.shape, jnp.int32),
        scratch_shapes=[pltpu.SemaphoreType.DMA(())],
    )(slot, init)


def _route_plan(route, n):
    n_assign = n * TOP_K
    flat_e = route[:, :TOP_K].astype(jnp.int32).reshape(-1)
    onehot = (flat_e[:, None] == jnp.arange(N_EXPERTS, dtype=jnp.int32)[None, :]).astype(jnp.int32)
    ranks = jnp.cumsum(onehot, axis=0) - onehot
    counts = jnp.sum(onehot, axis=0)
    rank = jnp.sum(ranks * onehot, axis=1)
    padded = (counts + MOE_ROWS - 1) // MOE_ROWS * MOE_ROWS
    pad_end = jnp.cumsum(padded)
    pad_start = pad_end - padded
    slot = pad_start[flat_e] + rank
    n_blocks = (n_assign + N_EXPERTS * (MOE_ROWS - 1)) // MOE_ROWS
    pad_ids = n_assign + jnp.arange(MOE_ROWS, dtype=jnp.int32)
    slot_assign = _invert_slots(slot, jnp.tile(pad_ids, n_blocks))
    valid = slot_assign < n_assign
    slot_src = jnp.where(valid, slot_assign // TOP_K, 0)
    slot_dst = jnp.where(valid, (slot_assign % TOP_K) * n + slot_assign // TOP_K, slot_assign)
    block_start = jnp.arange(n_blocks, dtype=jnp.int32) * MOE_ROWS
    block_expert = jnp.minimum(jnp.searchsorted(pad_end, block_start, side='right'),
                               N_EXPERTS - 1).astype(jnp.int32)
    block_used = (block_start < pad_end[-1]).astype(jnp.int32)
    return (block_expert, block_used,
            jnp.concatenate([slot_src, jnp.zeros((MOE_ROWS,), jnp.int32)]) * ROW_TILE,
            jnp.concatenate([pad_ids, slot_dst]) * ROW_TILE)


def _inv_freq(dim):
    return 1.0 / (ROPE_THETA ** (jnp.arange(0, dim, 2, dtype=F32) / dim))


def _rope_table(pos_of_lane, freq_of_lane, first_half, active=None):
    ang = pos_of_lane * freq_of_lane[None, :]
    cos, sin = jnp.cos(ang), jnp.sin(ang) * jnp.where(first_half, -1.0, 1.0)[None, :]
    if active is not None:
        cos = jnp.where(active[None, :], cos, 1.0)
        sin = jnp.where(active[None, :], sin, 0.0)
    return jnp.concatenate([cos, sin], axis=1).astype(F32)


def _rope_tables(seq, mem_len):
    lane = np.arange(LANES)
    pos = jnp.arange(seq, dtype=jnp.int32)
    posf = jnp.broadcast_to(pos.astype(F32)[:, None], (seq, LANES))
    tab_a = _rope_table(posf, _inv_freq(64)[lane % 32], (lane % 64) < 32)
    tab_b = _rope_table(posf, _inv_freq(32)[lane % 16], (lane % 32) < 16, (lane >= 64) & (lane < 96))
    rows = (pos // MEM_GRID_W).astype(F32)[:, None]
    cols = (pos % MEM_GRID_W).astype(F32)[:, None]
    pos_c = jnp.where(((lane % 64) < 32)[None, :], rows, cols)
    tab_c = _rope_table(pos_c, _inv_freq(32)[lane % 16], (lane % 32) < 16)
    f_d = _inv_freq(128)[lane % 64]
    tab_d = _rope_table(posf, f_d, lane < 64)
    mposf = jnp.broadcast_to(jnp.arange(mem_len, dtype=F32)[:, None], (mem_len, LANES))
    tab_m = _rope_table(mposf, f_d, lane < 64)
    return jnp.concatenate([tab_a, tab_b, tab_c, tab_d], axis=1), tab_m


def _score_bound(q_gain, k_gain, dim):
    gq = jnp.max(jnp.abs(q_gain.astype(F32)))
    gk = jnp.max(jnp.abs(k_gain.astype(F32)))
    return (1.02 * LOG2_E * dim ** 0.5 * gq * gk).reshape(1)


def _segment_matrix(width):
    i = np.arange(MXU_DIM)
    return jnp.asarray((i[:, None] // width) == (i[None, :] // width), dtype=BF16)


def _tile_row(g, reps):
    return jnp.tile(g.astype(F32), reps).reshape(1, -1)


def _layer_params(layer, w_in, w_mla_q_up, w_mla_kv_up, w_mem_kv, gains):
    w = w_in[layer]
    z = lambda c: jnp.zeros((D_MODEL, c), F32)
    gk, gv = w[:, 2464:2592], w[:, 2592:2720]
    dup = lambda m: jnp.concatenate([m[:, :64], m[:, :64], m[:, 64:], m[:, 64:]], axis=1)
    w_big = jnp.concatenate([
        w[:, 0:1536], w[:, 1536:1792], w[:, 1792:1920],
        z(64), w[:, 1920:1952], z(32),
        w[:, 1952:2464], dup(gk), dup(gv), w[:, 2720:3232]], axis=1)
    wq = w_mla_q_up[layer].reshape(256, MLA_HEADS, MLA_QK_DIM)
    wq = jnp.pad(wq, ((0, 0), (0, 0), (0, LANES - MLA_QK_DIM))).reshape(256, MLA_HEADS * LANES)
    wkv = w_mla_kv_up[layer].reshape(128, MLA_HEADS, 128)
    wk = jnp.pad(wkv[:, :, :64], ((0, 0), (0, 0), (0, 64))).reshape(128, MLA_HEADS * LANES)
    wv = wkv[:, :, 64:].reshape(128, MLA_HEADS * 64)
    wm = w_mem_kv[layer].reshape(D_MODEL, 4, 256)
    w_mem = jnp.concatenate([wm[:, :, :128].reshape(D_MODEL, 512), wm[:, :, 128:].reshape(D_MODEL, 512)], axis=1)
    pad96 = lambda g: jnp.pad(g.astype(F32), (0, LANES - MLA_QK_DIM)).reshape(1, LANES)
    (dq_g, dk_g, ql_g, kvl_g, mq_g, mk_g, gq_g, gk_g, memq_g) = gains
    gain_rows = jnp.concatenate([
        _tile_row(dq_g, 2), _tile_row(dk_g, 2), ql_g.astype(F32).reshape(2, LANES),
        kvl_g.astype(F32).reshape(1, LANES), pad96(mq_g), pad96(mk_g),
        _tile_row(gq_g, 2), _tile_row(gk_g, 2), memq_g.astype(F32).reshape(1, LANES),
        jnp.zeros((6, LANES), F32)], axis=0)
    return (w_big.astype(BF16), wq.astype(BF16), jnp.concatenate([wk, wv], axis=1).astype(BF16),
            w_mem.astype(BF16), gain_rows)


def kernel(x, mem, norm_attn_g, norm_mem_g, w_in, diff_q_norm_g, diff_k_norm_g, diff_lambda, diff_subln_g,
           mla_q_lat_g, mla_kv_lat_g, w_mla_q_up, w_mla_kv_up, mla_q_norm_g, mla_k_norm_g, gqa_q_norm_g,
           gqa_k_norm_g, w_mem_kv, mem_q_norm_g, mem_k_norm_g, w_branch, w_gate, b_gate, w_out, norm_ffn_g,
           dense_w1, dense_w3, dense_w2, moe_router, moe_w1, moe_w3, moe_w2):
    batch, seq, _ = x.shape
    mem_len = mem.shape[1]
    n = batch * seq
    depth = w_in.shape[0]
    tab, tab_m = _rope_tables(seq, mem_len)
    seg64, seg128 = _segment_matrix(64), _segment_matrix(128)
    x2 = x.reshape(n, D_MODEL)
    mem2 = mem.reshape(batch * mem_len, D_MODEL)
    row = lambda g: g.astype(F32).reshape(1, -1)
    b3 = lambda a, w: a.reshape(batch, -1, w)

    for layer in range(depth):
        lambda_init = 0.8 - 0.6 * math.exp(-0.3 * layer)
        w_big, wq_up, wkv_up, w_mem, gain_rows = _layer_params(
            layer, w_in, w_mla_q_up, w_mla_kv_up, w_mem_kv,
            (diff_q_norm_g[layer], diff_k_norm_g[layer], mla_q_lat_g[layer], mla_kv_lat_g[layer],
             mla_q_norm_g[layer], mla_k_norm_g[layer], gqa_q_norm_g[layer], gqa_k_norm_g[layer],
             mem_q_norm_g[layer]))
        qa, ka, va, qb, kb, vb, qc, kc, vc, qd = _inproj(
            x2, row(norm_attn_g[layer]), w_big, wq_up, wkv_up, seg64, seg128, gain_rows, tab, seq)
        kd, vd = _memprep(mem2, row(norm_mem_g[layer]), w_mem, seg128, row(mem_k_norm_g[layer]), tab_m, mem_len)

        o_a = _attention("attn_diff", [(0, 0, 0, 0), (0, 1, 0, 0), (128, 0, 128, 128), (128, 1, 128, 128)],
                         functools.partial(_finish_diff, lambda_init=lambda_init),
                         _score_bound(diff_q_norm_g[layer], diff_k_norm_g[layer], 64),
                         b3(qa, 512), b3(ka, 512), b3(va, 512), groups=2, q_width=256,
                         k_width=256, v_width=256, o_width=256, tq=1024,
                         extra=(diff_lambda[layer].astype(F32), row(diff_subln_g[layer])))
        o_b = _attention("attn_latent", [(128 * u, None, 128 * u, 128 * (u // 2)) for u in range(4)],
                         _finish_pairs, _score_bound(mla_q_norm_g[layer], mla_k_norm_g[layer], MLA_QK_DIM),
                         b3(qb, 1024), b3(kb, 1024), b3(vb, 512), groups=2,
                         q_width=512, k_width=512, v_width=256, o_width=256, tq=1024, sums_on_mxu=True)
        o_c = _attention("attn_gqa", [(0, 0, 0, 0), (0, 1, 0, 0), (128, 0, 0, 0), (128, 1, 0, 0)],
                         _finish_pairs, _score_bound(gqa_q_norm_g[layer], gqa_k_norm_g[layer], 64),
                         b3(qc, 512), b3(kc, 256), b3(vc, 256), groups=2,
                         q_width=256, k_width=128, v_width=128, o_width=256, tq=1024, sums_on_mxu=True)
        o_d = _attention("attn_mem", [(128 * u, None, 128 * u, 128 * u) for u in range(4)],
                         _finish_tiles, _score_bound(mem_q_norm_g[layer], mem_k_norm_g[layer], 128),
                         b3(qd, 512), b3(kd, 512), b3(vd, 512), groups=1,
                         q_width=512, k_width=512, v_width=512, o_width=512, tq=2048)
        branches = [o.reshape(n, 512) for o in (o_a, o_b, o_c, o_d)]

        i = layer // 2
        routed = layer % 2 == 1
        w_router = None
        if routed:
            w_r = jnp.pad(moe_router[i].astype(F32), ((0, 0), (0, LANES - N_EXPERTS)))
            w_r_hi = w_r.astype(BF16)
            w_router = jnp.stack([w_r_hi, (w_r - w_r_hi.astype(F32)).astype(BF16)])
        merged = _merge(x2, branches, row(norm_attn_g[layer]), w_gate[layer].astype(BF16),
                        b_gate[layer].astype(F32).reshape(4, 1, D_MODEL), w_branch[layer].astype(BF16),
                        w_out[layer].astype(BF16), row(norm_ffn_g[layer]), w_router)
        if not routed:
            x_mid, h2 = merged
            x2 = _dense_ffn(x_mid, h2, dense_w1[i].astype(BF16), dense_w3[i].astype(BF16),
                            dense_w2[i].astype(BF16))
        else:
            x_mid, h2, route = merged
            block_expert, block_used, slot_src, slot_dst = _route_plan(route, n)
            y = _moe_ffn(h2, block_expert, block_used, slot_src, slot_dst, moe_w1[i].astype(BF16),
                         moe_w3[i].astype(BF16), moe_w2[i].astype(BF16))
            x2 = _combine(x_mid, y, route)
    return x2.reshape(batch, seq, D_MODEL)
```

```python
import functools
import math

import numpy as np
import jax
import jax.numpy as jnp
from jax import lax
from jax.experimental import pallas as pl
from jax.experimental.pallas import tpu as pltpu

F32 = jnp.float32
BF16 = jnp.bfloat16

D_MODEL = 1024
MEM_GRID_W = 64
ROPE_THETA = 10000.0
NORM_EPS = 1e-6
LOG2_E = math.log2(math.e)
DIFF_HEADS = 4
MLA_HEADS = 8
MLA_QK_DIM = 96
FFN_DIM = 3584
N_EXPERTS = 8
TOP_K = 2
MOE_ROWS = 512
LANES = 128
MXU_DIM = 256
VMEM_LIMIT = 56 * 1024 * 1024

C_DQ, C_DK, C_DV = 0, 512, 1024
C_MQL, C_MKVL, C_KROPE = 1536, 1792, 1920
C_GQ, C_GK, C_GV, C_MEMQ = 2048, 2560, 2816, 3072
IN_COLS_PADDED = 3584
T_A, T_B, T_C, T_D = 0, 256, 512, 768


def _const_spec(shape):
    return pl.BlockSpec(shape, lambda *_: (0,) * len(shape), pipeline_mode=pl.Buffered(1))


def _rms_rows(x, g):
    ms = jnp.mean(x * x, axis=-1, keepdims=True)
    return x * lax.rsqrt(ms + NORM_EPS) * g


def _segment_mean_sq(y, seg_ref, inv_count):
    sq = (y * y).astype(BF16)
    parts = [jnp.dot(sq[:, c:c + MXU_DIM], seg_ref[...], preferred_element_type=F32)
             for c in range(0, y.shape[1], MXU_DIM)]
    ss = parts[0] if len(parts) == 1 else jnp.concatenate(parts, axis=1)
    return ss * inv_count


def _rope_lanes(y, cos, sin_signed, half):
    lane = lax.broadcasted_iota(jnp.int32, y.shape, 1)
    first = (lane % (2 * half)) < half
    rot = jnp.where(first, pltpu.roll(y, LANES - half, 1), pltpu.roll(y, half, 1))
    return y * cos + rot * sin_signed


def _norm_rope_store(y, ms, gain, tab_ref, t_off, half, scale, out_ref):
    yn = y * lax.rsqrt(ms + NORM_EPS)
    cos = tab_ref[:, t_off:t_off + LANES]
    sin = tab_ref[:, t_off + LANES:t_off + 2 * LANES]
    for c in range(0, y.shape[1], LANES):
        r = _rope_lanes(yn[:, c:c + LANES] * gain, cos, sin, half)
        if scale != 1.0:
            r = r * scale
        out_ref[:, c:c + LANES] = r.astype(out_ref.dtype)


def _inproj_body(x_ref, g_ref, w_ref, wq_ref, wkv_ref, seg64_ref, seg128_ref, gains_ref, tab_ref,
                 qa_ref, ka_ref, va_ref, qb_ref, kb_ref, vb_ref, qc_ref, kc_ref, vc_ref, qd_ref):
    h = _rms_rows(x_ref[...], g_ref[...]).astype(BF16)

    def proj(c0, width):
        return jnp.dot(h, w_ref[:, c0:c0 + width], preferred_element_type=F32)

    def gain(row):
        return gains_ref[row:row + 1, :]

    y = proj(C_DQ, 512)
    _norm_rope_store(y, _segment_mean_sq(y, seg64_ref, 1.0 / 64), gain(0), tab_ref, T_A, 32,
                     LOG2_E * 64 ** -0.5, qa_ref)
    y = proj(C_DK, 512)
    _norm_rope_store(y, _segment_mean_sq(y, seg64_ref, 1.0 / 64), gain(1), tab_ref, T_A, 32, 1.0, ka_ref)
    va_ref[...] = proj(C_DV, 512).astype(BF16)

    g_ql = jnp.concatenate([gain(2), gain(3)], axis=1)
    ql = _rms_rows(proj(C_MQL, 256), g_ql).astype(BF16)
    y = jnp.dot(ql, wq_ref[...], preferred_element_type=F32)
    _norm_rope_store(y, _segment_mean_sq(y, seg128_ref, 1.0 / MLA_QK_DIM), gain(5), tab_ref, T_B, 16,
                     LOG2_E * MLA_QK_DIM ** -0.5, qb_ref)
    kvl = _rms_rows(proj(C_MKVL, 128), gain(4)).astype(BF16)
    kv = jnp.dot(kvl, wkv_ref[...], preferred_element_type=F32)
    k_rope = proj(C_KROPE, 128)
    y = kv[:, :1024] + jnp.concatenate([k_rope] * MLA_HEADS, axis=1)
    _norm_rope_store(y, _segment_mean_sq(y, seg128_ref, 1.0 / MLA_QK_DIM), gain(6), tab_ref, T_B, 16,
                     1.0, kb_ref)
    vb_ref[...] = kv[:, 1024:].astype(BF16)

    y = proj(C_GQ, 512)
    _norm_rope_store(y, _segment_mean_sq(y, seg64_ref, 1.0 / 64), gain(7), tab_ref, T_C, 16,
                     LOG2_E * 64 ** -0.5, qc_ref)
    y = proj(C_GK, 256)
    _norm_rope_store(y, _segment_mean_sq(y, seg64_ref, 1.0 / 64), gain(8), tab_ref, T_C, 16, 1.0, kc_ref)
    vc_ref[...] = proj(C_GV, 256).astype(BF16)

    y = proj(C_MEMQ, 512)
    _norm_rope_store(y, _segment_mean_sq(y, seg128_ref, 1.0 / 128), gain(9), tab_ref, T_D, 64,
                     LOG2_E * 128 ** -0.5, qd_ref)


def _inproj(x2, g, w_big, wq_up, wkv_up, seg64, seg128, gains, tab, seq):
    n = x2.shape[0]
    tm = min(1024, seq)
    tiles_per_seq = seq // tm
    row = lambda w: pl.BlockSpec((tm, w), lambda i: (i, 0))
    out_widths = (512, 512, 512, 1024, 1024, 512, 512, 256, 256, 512)
    return pl.pallas_call(
        _inproj_body,
        name="inproj",
        grid=(n // tm,),
        in_specs=[row(D_MODEL), _const_spec((1, D_MODEL)), _const_spec(w_big.shape),
                  _const_spec(wq_up.shape), _const_spec(wkv_up.shape), _const_spec(seg64.shape),
                  _const_spec(seg128.shape), _const_spec(gains.shape),
                  pl.BlockSpec((tm, tab.shape[1]), lambda i: (i % tiles_per_seq, 0))],
        out_specs=[row(w) for w in out_widths],
        out_shape=[jax.ShapeDtypeStruct((n, w), BF16) for w in out_widths],
        compiler_params=pltpu.CompilerParams(dimension_semantics=("parallel",),
                                             vmem_limit_bytes=VMEM_LIMIT),
    )(x2, g, w_big, wq_up, wkv_up, seg64, seg128, gains, tab)


def _memprep_body(m_ref, g_ref, w_ref, seg128_ref, gain_ref, tab_ref, k_ref, v_ref):
    mn = _rms_rows(m_ref[...], g_ref[...]).astype(BF16)
    kv = jnp.dot(mn, w_ref[...], preferred_element_type=F32)
    y = kv[:, :512]
    _norm_rope_store(y, _segment_mean_sq(y, seg128_ref, 1.0 / 128), gain_ref[...], tab_ref, 0, 64, 1.0, k_ref)
    v_ref[...] = kv[:, 512:].astype(BF16)


def _memprep(mem2, g, w_kv, seg128, gain, tab_m, mem_len):
    n = mem2.shape[0]
    row = lambda w: pl.BlockSpec((mem_len, w), lambda i: (i, 0))
    return pl.pallas_call(
        _memprep_body,
        name="memprep",
        grid=(n // mem_len,),
        in_specs=[row(D_MODEL), _const_spec((1, D_MODEL)), _const_spec(w_kv.shape),
                  _const_spec(seg128.shape), _const_spec((1, LANES)), _const_spec(tab_m.shape)],
        out_specs=[row(512), row(512)],
        out_shape=[jax.ShapeDtypeStruct((n, 512), BF16)] * 2,
        compiler_params=pltpu.CompilerParams(dimension_semantics=("parallel",),
                                             vmem_limit_bytes=VMEM_LIMIT),
    )(mem2, g, w_kv, seg128, gain, tab_m)


SAFE_SHIFT_LIMIT = 60.0


def _attend(q, k, v, sums_on_mxu, shift=None):
    s = lax.dot_general(q, k, (((1,), (1,)), ((), ())), preferred_element_type=F32)
    e = jnp.exp2(s - (jnp.max(s, axis=-1, keepdims=True) if shift is None else shift))
    if not sums_on_mxu:
        l = jnp.sum(e, axis=-1, keepdims=True)
        return jnp.dot(e.astype(BF16), v, preferred_element_type=F32) * (1.0 / l)
    v_ones = jnp.concatenate([v, jnp.ones((v.shape[0], LANES), BF16)], axis=1)
    ov = jnp.dot(e.astype(BF16), v_ones, preferred_element_type=F32)
    return ov[:, :LANES] * (1.0 / ov[:, LANES:LANES + 1])


def _split_halves(q):
    lo = lax.broadcasted_iota(jnp.int32, q.shape, 1) < 64
    zero = jnp.zeros_like(q)
    return jnp.where(lo, q, zero), jnp.where(lo, zero, q)


def _join_halves(o_lo, o_hi):
    lo = lax.broadcasted_iota(jnp.int32, o_lo.shape, 1) < 64
    return jnp.where(lo, o_lo, o_hi)


def _attn_body(shift_ref, *refs, units, finish, n_extra, sums_on_mxu):
    extra = refs[:n_extra]
    q_ref, k_ref, v_ref, o_ref = refs[n_extra:]
    shift = shift_ref[0]

    def run(shift):
        halves = {}
        outs = []
        for q_lane, half, k_lane, v_lane in units:
            q = q_ref[:, q_lane:q_lane + LANES]
            if half is not None:
                if q_lane not in halves:
                    halves[q_lane] = _split_halves(q)
                q = halves[q_lane][half]
            outs.append(_attend(q, k_ref[:, k_lane:k_lane + LANES], v_ref[:, v_lane:v_lane + LANES],
                                sums_on_mxu, shift))
        finish(outs, o_ref, *extra)

    @pl.when(shift <= SAFE_SHIFT_LIMIT)
    def _():
        run(shift)

    @pl.when(jnp.logical_not(shift <= SAFE_SHIFT_LIMIT))
    def _():
        run(None)


def _finish_pairs(outs, o_ref):
    for p in range(len(outs) // 2):
        o_ref[:, p * LANES:(p + 1) * LANES] = _join_halves(outs[2 * p], outs[2 * p + 1]).astype(o_ref.dtype)


def _finish_tiles(outs, o_ref):
    for p, o in enumerate(outs):
        o_ref[:, p * LANES:(p + 1) * LANES] = o.astype(o_ref.dtype)


def _finish_diff(outs, o_ref, lam_ref, g_ref, *, lambda_init):
    lp = lam_ref[...]
    lam = (jnp.exp(jnp.sum(lp[0:1] * lp[1:2], keepdims=True))
           - jnp.exp(jnp.sum(lp[2:3] * lp[3:4], keepdims=True)) + lambda_init)
    for p in range(len(outs) // 2):
        o = outs[2 * p] - outs[2 * p + 1] * lam
        o_ref[:, p * LANES:(p + 1) * LANES] = (
            _rms_rows(o, g_ref[...]) * (1.0 - lambda_init)).astype(o_ref.dtype)


def _attention(name, units, finish, shift, q, k, v, *, groups, q_width, k_width, v_width, o_width, tq,
               sums_on_mxu=False, extra=()):
    b, s, _ = q.shape
    kv_len = k.shape[1]
    tq = min(tq, s)
    grid_spec = pltpu.PrefetchScalarGridSpec(
        num_scalar_prefetch=1,
        grid=(b, groups, s // tq),
        in_specs=[_const_spec(e.shape) for e in extra] + [
            pl.BlockSpec((None, tq, q_width), lambda bi, gi, qi, _: (bi, qi, gi)),
            pl.BlockSpec((None, kv_len, k_width), lambda bi, gi, qi, _: (bi, 0, gi)),
            pl.BlockSpec((None, kv_len, v_width), lambda bi, gi, qi, _: (bi, 0, gi)),
        ],
        out_specs=pl.BlockSpec((None, tq, o_width), lambda bi, gi, qi, _: (bi, qi, gi)),
    )
    return pl.pallas_call(
        functools.partial(_attn_body, units=units, finish=finish, n_extra=len(extra),
                          sums_on_mxu=sums_on_mxu),
        name=name,
        grid_spec=grid_spec,
        out_shape=jax.ShapeDtypeStruct((b, s, groups * o_width), BF16),
        compiler_params=pltpu.CompilerParams(
            dimension_semantics=("parallel", "parallel", "arbitrary"), vmem_limit_bytes=VMEM_LIMIT),
    )(shift, *extra, q, k, v)


def _route_rows(h2, wr_ref, route_ref):
    h_hi = h2.astype(BF16)
    h_lo = (h2 - h_hi.astype(F32)).astype(BF16)
    logits = (jnp.dot(h_hi, wr_ref[0], preferred_element_type=F32)
              + (jnp.dot(h_hi, wr_ref[1], preferred_element_type=F32)
                 + jnp.dot(h_lo, wr_ref[0], preferred_element_type=F32)))
    l1 = logits.T[:N_EXPERTS, :]
    row = lax.broadcasted_iota(jnp.int32, l1.shape, 0)
    neg = jnp.float32(-jnp.inf)
    m1 = jnp.max(l1, axis=0, keepdims=True)
    i1 = jnp.min(jnp.where(l1 == m1, row, N_EXPERTS), axis=0, keepdims=True)
    l2 = jnp.where(row == i1, neg, l1)
    m2 = jnp.max(l2, axis=0, keepdims=True)
    i2 = jnp.min(jnp.where(l2 == m2, row, N_EXPERTS), axis=0, keepdims=True)
    e2 = jnp.exp(m2 - m1)
    g1 = 1.0 / (1.0 + e2)
    g2 = e2 / (1.0 + e2)
    route_t = jnp.where(row == 0, i1.astype(F32),
                        jnp.where(row == 1, i2.astype(F32),
                                  jnp.where(row == 2, g1, jnp.where(row == 3, g2, 0.0))))
    pad = jnp.zeros((LANES - N_EXPERTS, route_t.shape[1]), F32)
    route_ref[...] = jnp.concatenate([route_t, pad], axis=0).T


def _merge_body(*refs, with_router):
    if with_router:
        (x_ref, oa_ref, ob_ref, oc_ref, od_ref, g_ref, wg_ref, bg_ref, wb_ref, wo_ref, gf_ref, wr_ref,
         xo_ref, h2_ref, route_ref, hprev_ref) = refs

        @pl.when(pl.program_id(0) == 0)
        def _():
            hprev_ref[...] = jnp.zeros(hprev_ref.shape, F32)

        _route_rows(hprev_ref[...], wr_ref, route_ref)
    else:
        (x_ref, oa_ref, ob_ref, oc_ref, od_ref, g_ref, wg_ref, bg_ref, wb_ref, wo_ref, gf_ref,
         xo_ref, h2_ref) = refs
    x = x_ref[...]
    h = _rms_rows(x, g_ref[...]).astype(BF16)
    merged = None
    for b, o_ref in enumerate((oa_ref, ob_ref, oc_ref, od_ref)):
        gate = jax.nn.sigmoid(jnp.dot(h, wg_ref[b], preferred_element_type=F32) + bg_ref[b])
        term = gate * jnp.dot(o_ref[...], wb_ref[b], preferred_element_type=F32)
        merged = term if merged is None else merged + term
    x_new = x + jnp.dot(merged.astype(BF16), wo_ref[...], preferred_element_type=F32)
    xo_ref[...] = x_new
    h2 = _rms_rows(x_new, gf_ref[...])
    if with_router:
        _tile_rows_store(h2_ref, h2)
        hprev_ref[...] = h2
    else:
        h2_ref[...] = h2.astype(h2_ref.dtype)


def _merge(x2, branches, g, wg, bg, wb, wo, gf, w_router=None):
    n = x2.shape[0]
    tm = min(512, n)
    n_tiles = n // tm
    with_router = w_router is not None
    row = lambda w: pl.BlockSpec((tm, w), lambda i: (jnp.minimum(i, n_tiles - 1), 0))
    in_specs = ([row(D_MODEL)] + [row(512)] * 4
                + [_const_spec((1, D_MODEL)), _const_spec(wg.shape), _const_spec(bg.shape),
                   _const_spec(wb.shape), _const_spec(wo.shape), _const_spec((1, D_MODEL))])
    args = [x2, *branches, g, wg, bg, wb, wo, gf]
    out_specs = [row(D_MODEL), row(D_MODEL)]
    out_shape = [jax.ShapeDtypeStruct((n, D_MODEL), F32), jax.ShapeDtypeStruct((n, D_MODEL), BF16)]
    scratch_shapes = []
    if with_router:
        out_specs[1] = pl.BlockSpec((tm * ROW_TILE, LANES), lambda i: (jnp.minimum(i, n_tiles - 1), 0))
        out_shape[1] = jax.ShapeDtypeStruct((n * ROW_TILE, LANES), F32)
        in_specs.append(_const_spec(w_router.shape))
        args.append(w_router)
        out_specs.append(pl.BlockSpec((tm, LANES), lambda i: (jnp.maximum(i - 1, 0), 0)))
        out_shape.append(jax.ShapeDtypeStruct((n, LANES), F32))
        scratch_shapes.append(pltpu.VMEM((tm, D_MODEL), F32))
    return pl.pallas_call(
        functools.partial(_merge_body, with_router=with_router),
        name="merge_router" if with_router else "merge",
        grid=(n_tiles + 1 if with_router else n_tiles,),
        in_specs=in_specs, out_specs=out_specs, out_shape=out_shape, scratch_shapes=scratch_shapes,
        compiler_params=pltpu.CompilerParams(
            dimension_semantics=("arbitrary",) if with_router else ("parallel",),
            vmem_limit_bytes=VMEM_LIMIT),
    )(*args)


def _swiglu_chunk(xb, w1, w3, w2):
    a = jnp.dot(xb, w1, preferred_element_type=F32)
    b = jnp.dot(xb, w3, preferred_element_type=F32)
    return jnp.dot((jax.nn.silu(a) * b).astype(BF16), w2, preferred_element_type=F32)


def _ffn_body(x_ref, h_ref, w1_ref, w3_ref, w2_ref, o_ref):
    @pl.when(pl.program_id(1) == 0)
    def _():
        o_ref[...] = x_ref[...]

    o_ref[...] += _swiglu_chunk(h_ref[...], w1_ref[...], w3_ref[...], w2_ref[...])


def _dense_ffn(x2, h2, w1, w3, w2):
    n = x2.shape[0]
    tm = min(1024, n)
    tf = 1792
    return pl.pallas_call(
        _ffn_body,
        name="dense_ffn",
        grid=(n // tm, FFN_DIM // tf),
        in_specs=[pl.BlockSpec((tm, D_MODEL), lambda i, f: (i, 0)),
                  pl.BlockSpec((tm, D_MODEL), lambda i, f: (i, 0)),
                  pl.BlockSpec((D_MODEL, tf), lambda i, f: (0, f)),
                  pl.BlockSpec((D_MODEL, tf), lambda i, f: (0, f)),
                  pl.BlockSpec((tf, D_MODEL), lambda i, f: (f, 0))],
        out_specs=pl.BlockSpec((tm, D_MODEL), lambda i, f: (i, 0)),
        out_shape=jax.ShapeDtypeStruct((n, D_MODEL), F32),
        compiler_params=pltpu.CompilerParams(dimension_semantics=("parallel", "arbitrary"),
                                             vmem_limit_bytes=VMEM_LIMIT),
    )(x2, h2, w1, w3, w2)


MOE_TF = 1792


ROW_TILE = D_MODEL // LANES


def _tile_rows_load(ref, rows):
    return jnp.concatenate([ref[pl.ds(s, rows, stride=ROW_TILE), :] for s in range(ROW_TILE)], axis=1)


def _tile_rows_store(ref, x):
    for s in range(ROW_TILE):
        ref[pl.ds(s, x.shape[0], stride=ROW_TILE), :] = x[:, s * LANES:(s + 1) * LANES]


def _moe_body(expert_ref, used_ref, src_ref, dst_ref, h_hbm, w1_ref, w3_ref, w2_ref, y_hbm,
              xg_ref, xb_ref, acc_ref, stage_ref, gsem, ssem):
    j = pl.program_id(0)
    f = pl.program_id(1)
    n_f = pl.num_programs(1)
    slot = j % 2
    other = 1 - slot
    rows_per_step = MOE_ROWS // (FFN_DIM // MOE_TF)

    def tile_of(r):
        first = r * ROW_TILE
        return pl.ds(first if isinstance(r, int) else pl.multiple_of(first, ROW_TILE), ROW_TILE)

    def gather_copy(block, r, buf):
        src = pl.multiple_of(src_ref[block * MOE_ROWS + r], ROW_TILE)
        return pltpu.make_async_copy(h_hbm.at[pl.ds(src, ROW_TILE), :], xg_ref.at[buf, tile_of(r), :],
                                     gsem.at[buf])

    def scatter_copy(block, r, buf):
        dst = pl.multiple_of(dst_ref[(block + 1) * MOE_ROWS + r], ROW_TILE)
        return pltpu.make_async_copy(stage_ref.at[buf, tile_of(r), :], y_hbm.at[pl.ds(dst, ROW_TILE), :],
                                     ssem)

    def wait_rows(copy):
        for _ in range(MOE_ROWS):
            copy.wait()

    @pl.when(jnp.logical_and(j == 0, f == 0))
    def _():
        def start(r, c):
            gather_copy(0, r, 0).start()
            return c
        lax.fori_loop(0, MOE_ROWS, start, 0)
        stage_ref[1] = jnp.zeros(stage_ref.shape[1:], F32)

    @pl.when(f == 0)
    def _():
        wait_rows(gather_copy(j, 0, slot))
        xb_ref[...] = _tile_rows_load(xg_ref.at[slot], MOE_ROWS).astype(BF16)

    for step in range(FFN_DIM // MOE_TF):
        @pl.when(f == step)
        def _():
            for r in range(step * rows_per_step, (step + 1) * rows_per_step):
                gather_copy(j + 1, r, other).start()
                scatter_copy(j - 1, r, other).start(priority=1)

    @pl.when(f == 0)
    def _():
        acc_ref[...] = jnp.zeros((MOE_ROWS, D_MODEL), F32)

    @pl.when(used_ref[j] > 0)
    def _():
        acc_ref[...] += _swiglu_chunk(xb_ref[...], w1_ref[...], w3_ref[...], w2_ref[...])

    @pl.when(f == n_f - 1)
    def _():
        wait_rows(scatter_copy(j - 1, 0, other))
        _tile_rows_store(stage_ref.at[slot], acc_ref[...])

    @pl.when(jnp.logical_and(j == pl.num_programs(0) - 1, f == n_f - 1))
    def _():
        def start(r, c):
            scatter_copy(j, r, slot).start()
            return c
        lax.fori_loop(0, MOE_ROWS, start, 0)
        wait_rows(scatter_copy(j, 0, slot))
        wait_rows(gather_copy(j + 1, 0, other))


def _moe_ffn(h2_tiles, block_expert, block_used, slot_src, slot_dst, w1, w3, w2):
    n = h2_tiles.shape[0] // ROW_TILE
    n_blocks = block_expert.shape[0]
    wspec = lambda shape, imap: pl.BlockSpec(shape, imap)
    grid_spec = pltpu.PrefetchScalarGridSpec(
        num_scalar_prefetch=4,
        grid=(n_blocks, FFN_DIM // MOE_TF),
        in_specs=[pl.BlockSpec(memory_space=pl.ANY),
                  wspec((None, D_MODEL, MOE_TF), lambda j, f, e, u, s, d: (e[j], 0, f)),
                  wspec((None, D_MODEL, MOE_TF), lambda j, f, e, u, s, d: (e[j], 0, f)),
                  wspec((None, MOE_TF, D_MODEL), lambda j, f, e, u, s, d: (e[j], f, 0))],
        out_specs=pl.BlockSpec(memory_space=pl.ANY),
        scratch_shapes=[pltpu.VMEM((2, MOE_ROWS * ROW_TILE, LANES), F32),
                        pltpu.VMEM((MOE_ROWS, D_MODEL), BF16),
                        pltpu.VMEM((MOE_ROWS, D_MODEL), F32),
                        pltpu.VMEM((2, MOE_ROWS * ROW_TILE, LANES), F32),
                        pltpu.SemaphoreType.DMA((2,)),
                        pltpu.SemaphoreType.DMA(())],
    )
    return pl.pallas_call(
        _moe_body,
        name="moe_ffn",
        grid_spec=grid_spec,
        out_shape=jax.ShapeDtypeStruct(((TOP_K * n + MOE_ROWS) * ROW_TILE, LANES), F32),
        compiler_params=pltpu.CompilerParams(dimension_semantics=("arbitrary", "arbitrary"),
                                             vmem_limit_bytes=VMEM_LIMIT),
    )(block_expert, block_used, slot_src, slot_dst, h2_tiles, w1, w3, w2)


def _combine_body(x_ref, y1_ref, y2_ref, route_ref, o_ref):
    g1 = route_ref[:, 2:3]
    g2 = route_ref[:, 3:4]
    rows = x_ref.shape[0]
    o_ref[...] = x_ref[...] + (_tile_rows_load(y1_ref, rows) * g1 + _tile_rows_load(y2_ref, rows) * g2)


def _combine(x2, y, route):
    n = x2.shape[0]
    tm = min(512, n)
    row = lambda w: pl.BlockSpec((tm, w), lambda i: (i, 0))
    return pl.pallas_call(
        _combine_body,
        name="moe_combine",
        grid=(n // tm,),
        in_specs=[row(D_MODEL), pl.BlockSpec((tm * ROW_TILE, LANES), lambda i: (i, 0)),
                  pl.BlockSpec((tm * ROW_TILE, LANES), lambda i: (i + n // tm, 0)),
                  row(LANES)],
        out_specs=row(D_MODEL),
        out_shape=jax.ShapeDtypeStruct((n, D_MODEL), F32),
        compiler_params=pltpu.CompilerParams(dimension_semantics=("parallel",),
                                             vmem_limit_bytes=VMEM_LIMIT),
    )(x2, y, y, route)


def _invert_slots_body(slot_ref, init_hbm, out_ref, sem):
    copy = pltpu.make_async_copy(init_hbm, out_ref, sem)
    copy.start()
    copy.wait()

    def place(a, c):
        out_ref[slot_ref[a]] = a
        return c
    lax.fori_loop(0, slot_ref.shape[0], place, 0, unroll=8)


def _invert_slots(slot, init):
    smem = pl.BlockSpec(memory_space=pltpu.SMEM)
    return pl.pallas_call(
        _invert_slots_body,
        name="invert_slots",
        in_specs=[smem, pl.BlockSpec(memory_space=pl.ANY)],
        out_specs=smem,
        out_shape=jax.ShapeDtypeStruct(init.shape, jnp.int32),
        scratch_shapes=[pltpu.SemaphoreType.DMA(())],
    )(slot, init)


def _route_plan(route, n):
    n_assign = n * TOP_K
    flat_e = route[:, :TOP_K].astype(jnp.int32).reshape(-1)
    onehot = (flat_e[:, None] == jnp.arange(N_EXPERTS, dtype=jnp.int32)[None, :]).astype(jnp.int32)
    ranks = jnp.cumsum(onehot, axis=0) - onehot
    counts = jnp.sum(onehot, axis=0)
    rank = jnp.sum(ranks * onehot, axis=1)
    padded = (counts + MOE_ROWS - 1) // MOE_ROWS * MOE_ROWS
    pad_end = jnp.cumsum(padded)
    pad_start = pad_end - padded
    slot = pad_start[flat_e] + rank
    n_blocks = (n_assign + N_EXPERTS * (MOE_ROWS - 1)) // MOE_ROWS
    pad_ids = n_assign + jnp.arange(MOE_ROWS, dtype=jnp.int32)
    slot_assign = _invert_slots(slot, jnp.tile(pad_ids, n_blocks))
    valid = slot_assign < n_assign
    slot_src = jnp.where(valid, slot_assign // TOP_K, 0)
    slot_dst = jnp.where(valid, (slot_assign % TOP_K) * n + slot_assign // TOP_K, slot_assign)
    block_start = jnp.arange(n_blocks, dtype=jnp.int32) * MOE_ROWS
    block_expert = jnp.minimum(jnp.searchsorted(pad_end, block_start, side='right'),
                               N_EXPERTS - 1).astype(jnp.int32)
    block_used = (block_start < pad_end[-1]).astype(jnp.int32)
    return (block_expert, block_used,
            jnp.concatenate([slot_src, jnp.zeros((MOE_ROWS,), jnp.int32)]) * ROW_TILE,
            jnp.concatenate([pad_ids, slot_dst]) * ROW_TILE)


def _inv_freq(dim):
    return 1.0 / (ROPE_THETA ** (jnp.arange(0, dim, 2, dtype=F32) / dim))


def _rope_table(pos_of_lane, freq_of_lane, first_half, active=None):
    ang = pos_of_lane * freq_of_lane[None, :]
    cos, sin = jnp.cos(ang), jnp.sin(ang) * jnp.where(first_half, -1.0, 1.0)[None, :]
    if active is not None:
        cos = jnp.where(active[None, :], cos, 1.0)
        sin = jnp.where(active[None, :], sin, 0.0)
    return jnp.concatenate([cos, sin], axis=1).astype(F32)


def _rope_tables(seq, mem_len):
    lane = np.arange(LANES)
    pos = jnp.arange(seq, dtype=jnp.int32)
    posf = jnp.broadcast_to(pos.astype(F32)[:, None], (seq, LANES))
    tab_a = _rope_table(posf, _inv_freq(64)[lane % 32], (lane % 64) < 32)
    tab_b = _rope_table(posf, _inv_freq(32)[lane % 16], (lane % 32) < 16, (lane >= 64) & (lane < 96))
    rows = (pos // MEM_GRID_W).astype(F32)[:, None]
    cols = (pos % MEM_GRID_W).astype(F32)[:, None]
    pos_c = jnp.where(((lane % 64) < 32)[None, :], rows, cols)
    tab_c = _rope_table(pos_c, _inv_freq(32)[lane % 16], (lane % 32) < 16)
    f_d = _inv_freq(128)[lane % 64]
    tab_d = _rope_table(posf, f_d, lane < 64)
    mposf = jnp.broadcast_to(jnp.arange(mem_len, dtype=F32)[:, None], (mem_len, LANES))
    tab_m = _rope_table(mposf, f_d, lane < 64)
    return jnp.concatenate([tab_a, tab_b, tab_c, tab_d], axis=1), tab_m


def _score_bound(q_gain, k_gain, dim):
    gq = jnp.max(jnp.abs(q_gain.astype(F32)))
    gk = jnp.max(jnp.abs(k_gain.astype(F32)))
    return (1.02 * LOG2_E * dim ** 0.5 * gq * gk).reshape(1)


def _segment_matrix(width):
    i = np.arange(MXU_DIM)
    return jnp.asarray((i[:, None] // width) == (i[None, :] // width), dtype=BF16)


def _tile_row(g, reps):
    return jnp.tile(g.astype(F32), reps).reshape(1, -1)


def _layer_params(layer, w_in, w_mla_q_up, w_mla_kv_up, w_mem_kv, gains):
    w = w_in[layer]
    z = lambda c: jnp.zeros((D_MODEL, c), F32)
    gk, gv = w[:, 2464:2592], w[:, 2592:2720]
    dup = lambda m: jnp.concatenate([m[:, :64], m[:, :64], m[:, 64:], m[:, 64:]], axis=1)
    w_big = jnp.concatenate([
        w[:, 0:1536], w[:, 1536:1792], w[:, 1792:1920],
        z(64), w[:, 1920:1952], z(32),
        w[:, 1952:2464], dup(gk), dup(gv), w[:, 2720:3232]], axis=1)
    wq = w_mla_q_up[layer].reshape(256, MLA_HEADS, MLA_QK_DIM)
    wq = jnp.pad(wq, ((0, 0), (0, 0), (0, LANES - MLA_QK_DIM))).reshape(256, MLA_HEADS * LANES)
    wkv = w_mla_kv_up[layer].reshape(128, MLA_HEADS, 128)
    wk = jnp.pad(wkv[:, :, :64], ((0, 0), (0, 0), (0, 64))).reshape(128, MLA_HEADS * LANES)
    wv = wkv[:, :, 64:].reshape(128, MLA_HEADS * 64)
    wm = w_mem_kv[layer].reshape(D_MODEL, 4, 256)
    w_mem = jnp.concatenate([wm[:, :, :128].reshape(D_MODEL, 512), wm[:, :, 128:].reshape(D_MODEL, 512)], axis=1)
    pad96 = lambda g: jnp.pad(g.astype(F32), (0, LANES - MLA_QK_DIM)).reshape(1, LANES)
    (dq_g, dk_g, ql_g, kvl_g, mq_g, mk_g, gq_g, gk_g, memq_g) = gains
    gain_rows = jnp.concatenate([
        _tile_row(dq_g, 2), _tile_row(dk_g, 2), ql_g.astype(F32).reshape(2, LANES),
        kvl_g.astype(F32).reshape(1, LANES), pad96(mq_g), pad96(mk_g),
        _tile_row(gq_g, 2), _tile_row(gk_g, 2), memq_g.astype(F32).reshape(1, LANES),
        jnp.zeros((6, LANES), F32)], axis=0)
    return (w_big.astype(BF16), wq.astype(BF16), jnp.concatenate([wk, wv], axis=1).astype(BF16),
            w_mem.astype(BF16), gain_rows)


def kernel(x, mem, norm_attn_g, norm_mem_g, w_in, diff_q_norm_g, diff_k_norm_g, diff_lambda, diff_subln_g,
           mla_q_lat_g, mla_kv_lat_g, w_mla_q_up, w_mla_kv_up, mla_q_norm_g, mla_k_norm_g, gqa_q_norm_g,
           gqa_k_norm_g, w_mem_kv, mem_q_norm_g, mem_k_norm_g, w_branch, w_gate, b_gate, w_out, norm_ffn_g,
           dense_w1, dense_w3, dense_w2, moe_router, moe_w1, moe_w3, moe_w2):
    batch, seq, _ = x.shape
    mem_len = mem.shape[1]
    n = batch * seq
    depth = w_in.shape[0]
    tab, tab_m = _rope_tables(seq, mem_len)
    seg64, seg128 = _segment_matrix(64), _segment_matrix(128)
    x2 = x.reshape(n, D_MODEL)
    mem2 = mem.reshape(batch * mem_len, D_MODEL)
    row = lambda g: g.astype(F32).reshape(1, -1)
    b3 = lambda a, w: a.reshape(batch, -1, w)

    for layer in range(depth):
        lambda_init = 0.8 - 0.6 * math.exp(-0.3 * layer)
        w_big, wq_up, wkv_up, w_mem, gain_rows = _layer_params(
            layer, w_in, w_mla_q_up, w_mla_kv_up, w_mem_kv,
            (diff_q_norm_g[layer], diff_k_norm_g[layer], mla_q_lat_g[layer], mla_kv_lat_g[layer],
             mla_q_norm_g[layer], mla_k_norm_g[layer], gqa_q_norm_g[layer], gqa_k_norm_g[layer],
             mem_q_norm_g[layer]))
        qa, ka, va, qb, kb, vb, qc, kc, vc, qd = _inproj(
            x2, row(norm_attn_g[layer]), w_big, wq_up, wkv_up, seg64, seg128, gain_rows, tab, seq)
        kd, vd = _memprep(mem2, row(norm_mem_g[layer]), w_mem, seg128, row(mem_k_norm_g[layer]), tab_m, mem_len)

        o_a = _attention("attn_diff", [(0, 0, 0, 0), (0, 1, 0, 0), (128, 0, 128, 128), (128, 1, 128, 128)],
                         functools.partial(_finish_diff, lambda_init=lambda_init),
                         _score_bound(diff_q_norm_g[layer], diff_k_norm_g[layer], 64),
                         b3(qa, 512), b3(ka, 512), b3(va, 512), groups=2, q_width=256,
                         k_width=256, v_width=256, o_width=256, tq=1024,
                         extra=(diff_lambda[layer].astype(F32), row(diff_subln_g[layer])))
        o_b = _attention("attn_latent", [(128 * u, None, 128 * u, 128 * (u // 2)) for u in range(4)],
                         _finish_pairs, _score_bound(mla_q_norm_g[layer], mla_k_norm_g[layer], MLA_QK_DIM),
                         b3(qb, 1024), b3(kb, 1024), b3(vb, 512), groups=2,
                         q_width=512, k_width=512, v_width=256, o_width=256, tq=1024, sums_on_mxu=True)
        o_c = _attention("attn_gqa", [(0, 0, 0, 0), (0, 1, 0, 0), (128, 0, 0, 0), (128, 1, 0, 0)],
                         _finish_pairs, _score_bound(gqa_q_norm_g[layer], gqa_k_norm_g[layer], 64),
                         b3(qc, 512), b3(kc, 256), b3(vc, 256), groups=2,
                         q_width=256, k_width=128, v_width=128, o_width=256, tq=1024, sums_on_mxu=True)
        o_d = _attention("attn_mem", [(128 * u, None, 128 * u, 128 * u) for u in range(4)],
                         _finish_tiles, _score_bound(mem_q_norm_g[layer], mem_k_norm_g[layer], 128),
                         b3(qd, 512), b3(kd, 512), b3(vd, 512), groups=1,
                         q_width=512, k_width=512, v_width=512, o_width=512, tq=2048)
        branches = [o.reshape(n, 512) for o in (o_a, o_b, o_c, o_d)]

        i = layer // 2
        routed = layer % 2 == 1
        w_router = None
        if routed:
            w_r = jnp.pad(moe_router[i].astype(F32), ((0, 0), (0, LANES - N_EXPERTS)))
            w_r_hi = w_r.astype(BF16)
            w_router = jnp.stack([w_r_hi, (w_r - w_r_hi.astype(F32)).astype(BF16)])
        merged = _merge(x2, branches, row(norm_attn_g[layer]), w_gate[layer].astype(BF16),
                        b_gate[layer].astype(F32).reshape(4, 1, D_MODEL), w_branch[layer].astype(BF16),
                        w_out[layer].astype(BF16), row(norm_ffn_g[layer]), w_router)
        if not routed:
            x_mid, h2 = merged
            x2 = _dense_ffn(x_mid, h2, dense_w1[i].astype(BF16), dense_w3[i].astype(BF16),
                            dense_w2[i].astype(BF16))
        else:
            x_mid, h2, route = merged
            block_expert, block_used, slot_src, slot_dst = _route_plan(route, n)
            y = _moe_ffn(h2, block_expert, block_used, slot_src, slot_dst, moe_w1[i].astype(BF16),
                         moe_w3[i].astype(BF16), moe_w2[i].astype(BF16))
            x2 = _combine(x_mid, y, route)
    return x2.reshape(batch, seq, D_MODEL)
```
